```python
import jax, jax.numpy as jnp
from jax import lax
import numpy as np

D_MODEL = 1024
BATCH = 4
SEQ = 4096
DEPTH = 4

D_MIX = D_MODEL
GLA_HEADS = 4
GLA_DK = 64
GLA_DV = 128
GLA_WIDTH = GLA_HEADS * GLA_DV
GLA_KEY_WIDTH = GLA_HEADS * GLA_DK
GLA_GATE_RANK = 16
GLA_GATE_TEMP = 16.0
GLA_CHUNK = 64
MLA_HEADS = 4
MLA_NOPE = 128
MLA_ROPE = 64
MLA_QK = MLA_NOPE + MLA_ROPE
MLA_DV = 128
MLA_WIDTH = MLA_HEADS * MLA_DV
MLA_Q_RANK = 256
MLA_KV_RANK = 128
ROPE_THETA = 10000.0
Q_BLOCK = 128
EPS = 1e-6

IN_SPLITS = (GLA_KEY_WIDTH, GLA_KEY_WIDTH, GLA_WIDTH, GLA_GATE_RANK, GLA_WIDTH,
             MLA_Q_RANK, MLA_KV_RANK, MLA_ROPE, MLA_WIDTH)
D_IN = sum(IN_SPLITS)

kernel_name = "hymba_gla_mla_hybrid_trunk"


def rmsnorm(t, g):
    t32 = t.astype(jnp.float32)
    y = t32 * lax.rsqrt(jnp.mean(t32 * t32, axis=-1, keepdims=True) + EPS)
    return (y * g.astype(jnp.float32)).astype(t.dtype)


def apply_rope(t, cos, sin):
    half = t.shape[-1] // 2
    t32 = t.astype(jnp.float32)
    t1, t2 = t32[..., :half], t32[..., half:]
    return jnp.concatenate([t1 * cos - t2 * sin, t1 * sin + t2 * cos], axis=-1).astype(t.dtype)


def gla_mix(q, k, v, log_a, norm_g):
    B, S = q.shape[0], q.shape[1]
    n_chunks = S // GLA_CHUNK

    def to_chunks(t):
        return t.astype(jnp.float32).reshape(B, n_chunks, GLA_CHUNK, GLA_HEADS, -1).transpose(0, 3, 1, 2, 4)

    qc = to_chunks(q) * (GLA_DK ** -0.5)
    kc, vc, lac = to_chunks(k), to_chunks(v), to_chunks(log_a)
    b = jnp.cumsum(lac, axis=3)
    b_last = b[:, :, :, -1:, :]
    q_dec = qc * jnp.exp(b)
    k_inv = kc * jnp.exp(-b)
    k_end = kc * jnp.exp(b_last - b)

    causal = jnp.tril(jnp.ones((GLA_CHUNK, GLA_CHUNK), dtype=bool))
    a_intra = jnp.where(causal, jnp.einsum('bhnik,bhnjk->bhnij', q_dec, k_inv), 0.0)
    o_intra = jnp.einsum('bhnij,bhnjv->bhniv', a_intra, vc)

    chunk_update = jnp.einsum('bhnjk,bhnjv->bhnkv', k_end, vc)
    chunk_decay = jnp.exp(b_last[:, :, :, 0, :])

    def step(state, inp):
        decay, upd = inp
        return decay[..., None] * state + upd, state

    state0 = jnp.zeros((B, GLA_HEADS, GLA_DK, GLA_DV), jnp.float32)
    _, state_prev = lax.scan(step, state0,
                             (jnp.moveaxis(chunk_decay, 2, 0), jnp.moveaxis(chunk_update, 2, 0)))
    state_prev = jnp.moveaxis(state_prev, 0, 2)
    o_inter = jnp.einsum('bhnik,bhnkv->bhniv', q_dec, state_prev)

    o = (o_intra + o_inter).transpose(0, 2, 3, 1, 4).reshape(B, S, GLA_HEADS, GLA_DV)
    o = rmsnorm(o.astype(q.dtype), norm_g)
    return o.reshape(B, S, GLA_WIDTH)


def block_causal_attention(q, k, v):
    B, S = q.shape[0], q.shape[1]
    n_blocks = S // Q_BLOCK
    scale = MLA_QK ** -0.5
    kt = k.transpose(0, 2, 1, 3)
    vt = v.transpose(0, 2, 1, 3)
    qb = q.transpose(0, 2, 1, 3).reshape(B, MLA_HEADS, n_blocks, Q_BLOCK, MLA_QK).transpose(2, 0, 1, 3, 4)
    key_idx = jnp.arange(S)

    def one_block(args):
        q_blk, i = args
        s = jnp.einsum('bhqd,bhkd->bhqk', q_blk, kt).astype(jnp.float32) * scale
        q_idx = i * Q_BLOCK + jnp.arange(Q_BLOCK)
        s = jnp.where(key_idx[None, :] <= q_idx[:, None], s, -jnp.inf)
        p = jax.nn.softmax(s, axis=-1).astype(v.dtype)
        return jnp.einsum('bhqk,bhkd->bhqd', p, vt)

    o = lax.map(one_block, (qb, jnp.arange(n_blocks)))
    return o.transpose(1, 0, 3, 2, 4).reshape(B, S, MLA_HEADS * MLA_DV)


def setup_inputs(seed: int = 0) -> dict:
    key = jax.random.key(seed)
    ks = jax.random.split(key, 16)
    f32 = jnp.float32

    def w(k, shape, fan_in):
        return jax.random.normal(k, shape, f32) * (fan_in ** -0.5)

    def gain(k, shape):
        return 1.0 + 0.02 * jax.random.normal(k, shape, f32)

    x = jax.random.normal(ks[0], (BATCH, SEQ, D_MODEL), f32)
    offsets = jax.random.randint(ks[1], (BATCH, 1), 0, 1024, dtype=jnp.int32)
    positions = offsets + jnp.arange(SEQ, dtype=jnp.int32)[None, :]
    return {
        "x": x,
        "positions": positions,
        "norm_g": gain(ks[2], (DEPTH, D_MODEL)),
        "w_in": w(ks[3], (DEPTH, D_MODEL, D_IN), D_MODEL),
        "w_gla_gate_up": w(ks[4], (DEPTH, GLA_GATE_RANK, GLA_KEY_WIDTH), GLA_GATE_RANK),
        "b_gla_gate": 0.01 * jax.random.normal(ks[5], (DEPTH, GLA_KEY_WIDTH), f32),
        "gla_norm_g": gain(ks[6], (DEPTH, GLA_DV)),
        "mla_q_norm_g": gain(ks[7], (DEPTH, MLA_Q_RANK)),
        "w_uq": w(ks[8], (DEPTH, MLA_Q_RANK, MLA_HEADS * MLA_QK), MLA_Q_RANK),
        "mla_kv_norm_g": gain(ks[9], (DEPTH, MLA_KV_RANK)),
        "w_ukv": w(ks[10], (DEPTH, MLA_KV_RANK, MLA_HEADS * (MLA_NOPE + MLA_DV)), MLA_KV_RANK),
        "q_head_g": gain(ks[11], (DEPTH, MLA_QK)),
        "k_head_g": gain(ks[12], (DEPTH, MLA_QK)),
        "w_out": w(ks[13], (DEPTH, D_MIX, D_MODEL), D_MIX),
    }


def reference(x, positions, norm_g, w_in, w_gla_gate_up, b_gla_gate, gla_norm_g,
              mla_q_norm_g, w_uq, mla_kv_norm_g, w_ukv, q_head_g, k_head_g, w_out):
    B, S = x.shape[0], x.shape[1]
    split_idx = [int(v) for v in np.cumsum(IN_SPLITS)[:-1]]

    inv_freq = ROPE_THETA ** (-jnp.arange(0, MLA_ROPE, 2, dtype=jnp.float32) / MLA_ROPE)
    ang = positions.astype(jnp.float32)[..., None] * inv_freq
    cos = jnp.cos(ang)[:, :, None, :]
    sin = jnp.sin(ang)[:, :, None, :]

    for l in range(DEPTH):
        h = rmsnorm(x, norm_g[l])
        z = h @ w_in[l]
        (g_q, g_k, g_v, g_lr, g_gate,
         c_q, c_kv, k_pe, m_gate) = jnp.split(z, split_idx, axis=-1)

        gate_logit = (g_lr @ w_gla_gate_up[l] + b_gla_gate[l]).astype(jnp.float32)
        log_a = jax.nn.log_sigmoid(gate_logit) / GLA_GATE_TEMP
        o_gla = gla_mix(g_q.reshape(B, S, GLA_HEADS, GLA_DK),
                        g_k.reshape(B, S, GLA_HEADS, GLA_DK),
                        g_v.reshape(B, S, GLA_HEADS, GLA_DV),
                        log_a.reshape(B, S, GLA_HEADS, GLA_DK),
                        gla_norm_g[l])
        o_gla = o_gla * jax.nn.silu(g_gate)

        q = (rmsnorm(c_q, mla_q_norm_g[l]) @ w_uq[l]).reshape(B, S, MLA_HEADS, MLA_QK)
        kv = (rmsnorm(c_kv, mla_kv_norm_g[l]) @ w_ukv[l]).reshape(B, S, MLA_HEADS, MLA_NOPE + MLA_DV)
        k_nope, v = kv[..., :MLA_NOPE], kv[..., MLA_NOPE:]
        k_rope = jnp.broadcast_to(k_pe[:, :, None, :], (B, S, MLA_HEADS, MLA_ROPE))
        k = jnp.concatenate([k_nope, k_rope], axis=-1)
        q = rmsnorm(q, q_head_g[l])
        k = rmsnorm(k, k_head_g[l])
        q = jnp.concatenate([q[..., :MLA_NOPE], apply_rope(q[..., MLA_NOPE:], cos, sin)], axis=-1)
        k = jnp.concatenate([k[..., :MLA_NOPE], apply_rope(k[..., MLA_NOPE:], cos, sin)], axis=-1)
        o_mla = block_causal_attention(q, k, v) * jax.nn.silu(m_gate)

        x = x + jnp.concatenate([o_gla, o_mla], axis=-1) @ w_out[l]
    return x
```

```python
import functools

import jax
import jax.numpy as jnp
from jax import lax
from jax.experimental import pallas as pl
from jax.experimental.pallas import tpu as pltpu

F32 = jnp.float32
BF16 = jnp.bfloat16

D_MODEL = 1024
GLA_HEADS = 4
GLA_DK = 64
GLA_DV = 128
GLA_KEY_WIDTH = GLA_HEADS * GLA_DK
GLA_WIDTH = GLA_HEADS * GLA_DV
GLA_GATE_RANK = 16
GLA_GATE_TEMP = 16.0
GLA_CHUNK = 64
MLA_HEADS = 4
MLA_NOPE = 128
MLA_ROPE = 64
MLA_QK = MLA_NOPE + MLA_ROPE
MLA_DV = 128
MLA_WIDTH = MLA_HEADS * MLA_DV
MLA_Q_RANK = 256
MLA_KV_RANK = 128
ROPE_THETA = 10000.0
EPS = 1e-6

LANES = 128
QK_PAD = 2 * LANES

OFF_GQ = 0
OFF_GK = 256
OFF_GV = 512
OFF_GGATE = 1024
OFF_CQ = 1536
OFF_CKV = 1792
OFF_MISC = 1920
OFF_MGATE = 2048
D_IN_PAD = 2560

PROJ_TM = 512
GLA_TM = 512
ATTN_TQ = 512
ATTN_TK = 512
OUT_TM = 512
CUMSUM_BLOCK = 256
VMEM_LIMIT = 56 * 1024 * 1024


def _rms(t, g):
    return t * lax.rsqrt(jnp.mean(t * t, axis=-1, keepdims=True) + EPS) * g


def _silu(t):
    return t * jax.nn.sigmoid(t)


def _split_dot(mat_bf16, t):
    hi = t.astype(BF16)
    lo = (t - hi.astype(F32)).astype(BF16)
    return (jnp.dot(mat_bf16, hi, preferred_element_type=F32)
            + jnp.dot(mat_bf16, lo, preferred_element_type=F32))


def _swap_halves(t):
    lane = lax.broadcasted_iota(jnp.int32, t.shape, 1)
    return jnp.where(lane % MLA_ROPE < MLA_ROPE // 2,
                     pltpu.roll(t, LANES - MLA_ROPE // 2, axis=1),
                     pltpu.roll(t, MLA_ROPE // 2, axis=1))


def _proj_kernel(x_ref, cos_ref, sin_ref, ng_ref, win_ref, wmisc_ref, bgate_ref,
                 qng_ref, wuq_ref, kvng_ref, wukv_ref, qgn_ref, qgr_ref, kgn_ref, kgr_ref,
                 qd_ref, ki_ref, ke_ref, gv_ref, dec_ref, gg_ref,
                 q_ref, k_ref, v_ref, mg_ref):
    tm = x_ref.shape[0]
    x = x_ref[...]
    h = _rms(x, ng_ref[...]).astype(BF16)
    z = jnp.dot(h, win_ref[...], preferred_element_type=F32)

    misc = z[:, OFF_MISC:OFF_MISC + LANES]
    logit = jnp.dot(misc.astype(BF16), wmisc_ref[...], preferred_element_type=F32) + bgate_ref[...]
    log_a = (jnp.minimum(logit, 0.0) - jnp.log1p(jnp.exp(-jnp.abs(logit)))) * (1.0 / GLA_GATE_TEMP)

    cb = CUMSUM_BLOCK
    r = lax.broadcasted_iota(jnp.int32, (cb, cb), 0)
    c = lax.broadcasted_iota(jnp.int32, (cb, cb), 1)
    same = (r // GLA_CHUNK) == (c // GLA_CHUNK)
    lower = jnp.where(same & (c <= r), 1.0, 0.0).astype(BF16)
    upper = jnp.where(same & (c > r), 1.0, 0.0).astype(BF16)
    nck = cb // GLA_CHUNK
    sr = lax.broadcasted_iota(jnp.int32, (8, cb), 0)
    sc = lax.broadcasted_iota(jnp.int32, (8, cb), 1)
    chunk_sum = jnp.where((sc // GLA_CHUNK) == sr, 1.0, 0.0).astype(BF16)

    gq = z[:, OFF_GQ:OFF_GQ + GLA_KEY_WIDTH] * (GLA_DK ** -0.5)
    gk = z[:, OFF_GK:OFF_GK + GLA_KEY_WIDTH]
    for blk in range(tm // cb):
        rows = slice(blk * cb, (blk + 1) * cb)
        la = log_a[rows]
        b = _split_dot(lower, la)
        rest = _split_dot(upper, la)
        tot = _split_dot(chunk_sum, la)
        qd_ref[rows, :] = (gq[rows] * jnp.exp(b)).astype(BF16)
        ki_ref[rows, :] = (gk[rows] * jnp.exp(-b)).astype(BF16)
        ke_ref[rows, :] = (gk[rows] * jnp.exp(rest)).astype(BF16)
        dec_ref[blk * nck:(blk + 1) * nck, :] = jnp.exp(tot[:nck])
    gv_ref[...] = z[:, OFF_GV:OFF_GV + GLA_WIDTH].astype(BF16)
    gg_ref[...] = _silu(z[:, OFF_GGATE:OFF_GGATE + GLA_WIDTH]).astype(BF16)

    cos = cos_ref[...]
    sin = sin_ref[...]
    lane = lax.broadcasted_iota(jnp.int32, (tm, LANES), 1)
    low64 = lane < MLA_ROPE

    cqn = _rms(z[:, OFF_CQ:OFF_CQ + MLA_Q_RANK], qng_ref[...]).astype(BF16)
    qraw = jnp.dot(cqn, wuq_ref[...], preferred_element_type=F32)
    ckvn = _rms(z[:, OFF_CKV:OFF_CKV + MLA_KV_RANK], kvng_ref[...]).astype(BF16)
    kvraw = jnp.dot(ckvn, wukv_ref[...], preferred_element_type=F32)

    scale = MLA_QK ** -0.5
    nope_w = MLA_HEADS * MLA_NOPE
    for pair in range(MLA_HEADS // 2):
        t = qraw[:, nope_w + pair * LANES:nope_w + (pair + 1) * LANES]
        t2 = t * t
        ssq_lo = jnp.sum(jnp.where(low64, t2, 0.0), axis=-1, keepdims=True)
        ssq_hi = jnp.sum(jnp.where(low64, 0.0, t2), axis=-1, keepdims=True)
        tg = t * qgr_ref[...]
        rot = tg * cos + _swap_halves(tg) * sin
        for sub in range(2):
            hd = 2 * pair + sub
            qn = qraw[:, hd * MLA_NOPE:(hd + 1) * MLA_NOPE]
            ssq = jnp.sum(qn * qn, axis=-1, keepdims=True) + (ssq_lo if sub == 0 else ssq_hi)
            rinv = lax.rsqrt(ssq * (1.0 / MLA_QK) + EPS) * scale
            q_ref[hd, :, 0:LANES] = (qn * rinv * qgn_ref[...]).astype(BF16)
            rsel = rot if sub == 0 else pltpu.roll(rot, MLA_ROPE, axis=1)
            q_ref[hd, :, LANES:QK_PAD] = jnp.where(low64, rsel * rinv, 0.0).astype(BF16)

    kpe2 = misc * misc
    ssq_pe = jnp.sum(jnp.where(low64, kpe2, 0.0), axis=-1, keepdims=True)
    kg = misc * kgr_ref[...]
    krot = kg * cos + _swap_halves(kg) * sin
    for hd in range(MLA_HEADS):
        kn = kvraw[:, hd * MLA_NOPE:(hd + 1) * MLA_NOPE]
        ssq = jnp.sum(kn * kn, axis=-1, keepdims=True) + ssq_pe
        rinv = lax.rsqrt(ssq * (1.0 / MLA_QK) + EPS)
        k_ref[hd, :, 0:LANES] = (kn * rinv * kgn_ref[...]).astype(BF16)
        k_ref[hd, :, LANES:QK_PAD] = (krot * rinv).astype(BF16)
        v_ref[hd, :, :] = kvraw[:, nope_w + hd * MLA_DV:nope_w + (hd + 1) * MLA_DV].astype(BF16)
    mg_ref[...] = _silu(z[:, OFF_MGATE:OFF_MGATE + MLA_WIDTH]).astype(BF16)


def _gla_kernel(qd_ref, ki_ref, ke_ref, v_ref, dec_ref, gate_ref, g_ref, o_ref, state_ref, *, steps_per_seq):
    tm = qd_ref.shape[0]

    @pl.when(pl.program_id(0) % steps_per_seq == 0)
    def _():
        state_ref[...] = jnp.zeros_like(state_ref)

    C = GLA_CHUNK
    klane_head = lax.broadcasted_iota(jnp.int32, (C, GLA_KEY_WIDTH), 1) // GLA_DK
    vlane_head = lax.broadcasted_iota(jnp.int32, (C, GLA_WIDTH), 1) // GLA_DV
    row = lax.broadcasted_iota(jnp.int32, (C, GLA_KEY_WIDTH), 0)
    col_in_head = lax.broadcasted_iota(jnp.int32, (C, GLA_KEY_WIDTH), 1) % GLA_DK
    causal = row >= col_in_head
    sr = lax.broadcasted_iota(jnp.int32, (GLA_WIDTH, GLA_KEY_WIDTH), 0) // GLA_DV
    sc = lax.broadcasted_iota(jnp.int32, (GLA_WIDTH, GLA_KEY_WIDTH), 1) // GLA_DK
    diag = sr == sc
    nt = (((1,), (1,)), ((), ()))
    tn = (((0,), (0,)), ((), ()))
    g = g_ref[...]

    for ck in range(tm // C):
        rows = slice(ck * C, (ck + 1) * C)
        qd = qd_ref[rows, :]
        ki = ki_ref[rows, :]
        ke = ke_ref[rows, :]
        v = v_ref[rows, :]
        zk = jnp.zeros_like(ki)
        zv = jnp.zeros_like(v)
        kstack = jnp.concatenate([jnp.where(klane_head == hd, ki, zk) for hd in range(GLA_HEADS)], axis=0)
        vstack = jnp.concatenate([jnp.where(vlane_head == hd, v, zv) for hd in range(GLA_HEADS)], axis=0)
        a = lax.dot_general(qd, kstack, nt, preferred_element_type=F32)
        a = jnp.where(causal, a, 0.0).astype(BF16)
        state = state_ref[...]
        o = (jnp.dot(a, vstack, preferred_element_type=F32)
             + lax.dot_general(qd, state.astype(BF16), nt, preferred_element_type=F32))
        upd = lax.dot_general(v, ke, tn, preferred_element_type=F32)
        state_ref[...] = state * dec_ref[ck:ck + 1, :] + jnp.where(diag, upd, 0.0)
        for hd in range(GLA_HEADS):
            cols = slice(hd * GLA_DV, (hd + 1) * GLA_DV)
            oh = o[:, cols]
            on = oh * lax.rsqrt(jnp.mean(oh * oh, axis=-1, keepdims=True) + EPS) * g
            o_ref[rows, cols] = (on * gate_ref[rows, cols].astype(F32)).astype(BF16)


def _attn_kernel(q_ref, k_ref, v_ref, gate_ref, o_ref, m_ref, l_ref, acc_ref):
    tq = q_ref.shape[0]
    tk = ATTN_TK
    i = pl.program_id(2)
    q = q_ref[...]
    nt = (((1,), (1,)), ((), ()))
    m_ref[...] = jnp.full_like(m_ref, -jnp.inf)
    l_ref[...] = jnp.zeros_like(l_ref)
    acc_ref[...] = jnp.zeros_like(acc_ref)

    def step(j, masked):
        start = pl.multiple_of(j * tk, tk)
        k = k_ref[pl.ds(start, tk), :]
        v = v_ref[pl.ds(start, tk), :]
        s = lax.dot_general(q, k, nt, preferred_element_type=F32)
        if masked:
            rr = lax.broadcasted_iota(jnp.int32, s.shape, 0)
            cc = lax.broadcasted_iota(jnp.int32, s.shape, 1)
            s = jnp.where(cc <= rr, s, -jnp.inf)
        m_prev = m_ref[...]
        m_new = jnp.maximum(m_prev, jnp.max(s, axis=-1, keepdims=True))
        alpha = jnp.exp(m_prev - m_new)
        p = jnp.exp(s - m_new)
        l_ref[...] = alpha * l_ref[...] + jnp.sum(p, axis=-1, keepdims=True)
        acc_ref[...] = alpha * acc_ref[...] + jnp.dot(p.astype(BF16), v, preferred_element_type=F32)
        m_ref[...] = m_new

    def body(j, carry):
        step(j, False)
        return carry

    lax.fori_loop(0, i * (tq // tk), body, 0)
    step(i * (tq // tk), True)
    o = acc_ref[...] / l_ref[...]
    o_ref[...] = (o * gate_ref[...].astype(F32)).astype(BF16)


def _out_kernel(x_ref, og_ref, om_ref, w_ref, o_ref):
    w = w_ref[...]
    o_ref[...] = (x_ref[...]
                  + jnp.dot(og_ref[...], w[:GLA_WIDTH], preferred_element_type=F32)
                  + jnp.dot(om_ref[...], w[GLA_WIDTH:], preferred_element_type=F32))


def _full(shape):
    return pl.BlockSpec(shape, lambda *idx: (0,) * len(shape))


def _params(n_axes):
    return pltpu.CompilerParams(dimension_semantics=("arbitrary",) * n_axes,
                                vmem_limit_bytes=VMEM_LIMIT)


def _proj_call(x2, cos_t, sin_t, lw, B, S):
    T = x2.shape[0]
    tm = PROJ_TM
    ns = S // tm
    rows = lambda w: pl.BlockSpec((tm, w), lambda i: (i, 0))
    heads = lambda w: pl.BlockSpec((None, MLA_HEADS, tm, w), lambda i: (i // ns, 0, i % ns, 0))
    out_shape = (
        jax.ShapeDtypeStruct((T, GLA_KEY_WIDTH), BF16),
        jax.ShapeDtypeStruct((T, GLA_KEY_WIDTH), BF16),
        jax.ShapeDtypeStruct((T, GLA_KEY_WIDTH), BF16),
        jax.ShapeDtypeStruct((T, GLA_WIDTH), BF16),
        jax.ShapeDtypeStruct((T // GLA_CHUNK, GLA_KEY_WIDTH), F32),
        jax.ShapeDtypeStruct((T, GLA_WIDTH), BF16),
        jax.ShapeDtypeStruct((B, MLA_HEADS, S, QK_PAD), BF16),
        jax.ShapeDtypeStruct((B, MLA_HEADS, S, QK_PAD), BF16),
        jax.ShapeDtypeStruct((B, MLA_HEADS, S, MLA_DV), BF16),
        jax.ShapeDtypeStruct((T, MLA_WIDTH), BF16),
    )
    out_specs = (rows(GLA_KEY_WIDTH), rows(GLA_KEY_WIDTH), rows(GLA_KEY_WIDTH), rows(GLA_WIDTH),
                 pl.BlockSpec((tm // GLA_CHUNK, GLA_KEY_WIDTH), lambda i: (i, 0)),
                 rows(GLA_WIDTH), heads(QK_PAD), heads(QK_PAD), heads(MLA_DV), rows(MLA_WIDTH))
    in_specs = [rows(D_MODEL), rows(LANES), rows(LANES), _full((1, D_MODEL)),
                _full((D_MODEL, D_IN_PAD)), _full((LANES, GLA_KEY_WIDTH)), _full((1, GLA_KEY_WIDTH)),
                _full((1, MLA_Q_RANK)), _full((MLA_Q_RANK, MLA_HEADS * MLA_QK)),
                _full((1, MLA_KV_RANK)), _full((MLA_KV_RANK, MLA_HEADS * (MLA_NOPE + MLA_DV))),
                _full((1, LANES)), _full((1, LANES)), _full((1, LANES)), _full((1, LANES))]
    return pl.pallas_call(
        _proj_kernel, grid=(T // tm,), in_specs=in_specs, out_specs=out_specs, out_shape=out_shape,
        compiler_params=_params(1), name="proj",
    )(x2, cos_t, sin_t, lw["norm_g"], lw["w_in"], lw["w_misc"], lw["b_gate"],
      lw["q_norm_g"], lw["w_uq"], lw["kv_norm_g"], lw["w_ukv"],
      lw["qg_nope"], lw["qg_rope"], lw["kg_nope"], lw["kg_rope"])


def _gla_call(qd, ki, ke, gv, dec, gg, gla_g, S):
    T = qd.shape[0]
    tm = GLA_TM
    rows = lambda w: pl.BlockSpec((tm, w), lambda i: (i, 0))
    return pl.pallas_call(
        functools.partial(_gla_kernel, steps_per_seq=S // tm),
        grid=(T // tm,),
        in_specs=[rows(GLA_KEY_WIDTH), rows(GLA_KEY_WIDTH), rows(GLA_KEY_WIDTH), rows(GLA_WIDTH),
                  pl.BlockSpec((tm // GLA_CHUNK, GLA_KEY_WIDTH), lambda i: (i, 0)),
                  rows(GLA_WIDTH), _full((1, GLA_DV))],
        out_specs=rows(GLA_WIDTH),
        out_shape=jax.ShapeDtypeStruct((T, GLA_WIDTH), BF16),
        scratch_shapes=[pltpu.VMEM((GLA_WIDTH, GLA_KEY_WIDTH), F32)],
        compiler_params=_params(1), name="gla",
    )(qd, ki, ke, gv, dec, gg, gla_g)


def _attn_call(q, k, v, mg):
    B, H, S, _ = q.shape
    tq = ATTN_TQ
    nq = S // tq
    return pl.pallas_call(
        _attn_kernel,
        grid=(B, H, nq),
        in_specs=[pl.BlockSpec((None, None, tq, QK_PAD), lambda b, h, i: (b, h, i, 0)),
                  pl.BlockSpec((None, None, S, QK_PAD), lambda b, h, i: (b, h, 0, 0)),
                  pl.BlockSpec((None, None, S, MLA_DV), lambda b, h, i: (b, h, 0, 0)),
                  pl.BlockSpec((tq, MLA_DV), lambda b, h, i: (b * nq + i, h))],
        out_specs=pl.BlockSpec((tq, MLA_DV), lambda b, h, i: (b * nq + i, h)),
        out_shape=jax.ShapeDtypeStruct((B * S, MLA_WIDTH), BF16),
        scratch_shapes=[pltpu.VMEM((tq, 1), F32), pltpu.VMEM((tq, 1), F32),
                        pltpu.VMEM((tq, MLA_DV), F32)],
        compiler_params=_params(3), name="attn",
    )(q, k, v, mg)


def _out_call(x2, og, om, w_out):
    T = x2.shape[0]
    tm = OUT_TM
    rows = lambda w: pl.BlockSpec((tm, w), lambda i: (i, 0))
    return pl.pallas_call(
        _out_kernel, grid=(T // tm,),
        in_specs=[rows(D_MODEL), rows(GLA_WIDTH), rows(MLA_WIDTH), _full((D_MODEL, D_MODEL))],
        out_specs=rows(D_MODEL),
        out_shape=jax.ShapeDtypeStruct((T, D_MODEL), F32),
        compiler_params=_params(1), name="outproj",
    )(x2, og, om, w_out)


def _prep_weights(norm_g, w_in, w_gla_gate_up, b_gla_gate, gla_norm_g, mla_q_norm_g, w_uq,
                  mla_kv_norm_g, w_ukv, q_head_g, k_head_g, w_out):
    depth = w_in.shape[0]
    o = [0]
    for wdt in (GLA_KEY_WIDTH, GLA_KEY_WIDTH, GLA_WIDTH, GLA_GATE_RANK, GLA_WIDTH,
                MLA_Q_RANK, MLA_KV_RANK, MLA_ROPE, MLA_WIDTH):
        o.append(o[-1] + wdt)
    gq, gk, gv, glr, ggate, cq, ckv, kpe, mgate = [w_in[:, :, o[n]:o[n + 1]] for n in range(9)]
    pad = jnp.zeros((depth, D_MODEL, LANES - MLA_ROPE - GLA_GATE_RANK), w_in.dtype)
    w_in_p = jnp.concatenate([gq, gk, gv, ggate, cq, ckv, kpe, glr, pad, mgate], axis=-1).astype(BF16)

    w_misc = jnp.zeros((depth, LANES, GLA_KEY_WIDTH), F32)
    w_misc = w_misc.at[:, MLA_ROPE:MLA_ROPE + GLA_GATE_RANK, :].set(w_gla_gate_up).astype(BF16)

    wq = w_uq.reshape(depth, MLA_Q_RANK, MLA_HEADS, MLA_QK)
    w_uq_p = jnp.concatenate([wq[..., :MLA_NOPE].reshape(depth, MLA_Q_RANK, -1),
                              wq[..., MLA_NOPE:].reshape(depth, MLA_Q_RANK, -1)], axis=-1).astype(BF16)
    wkv = w_ukv.reshape(depth, MLA_KV_RANK, MLA_HEADS, MLA_NOPE + MLA_DV)
    w_ukv_p = jnp.concatenate([wkv[..., :MLA_NOPE].reshape(depth, MLA_KV_RANK, -1),
                               wkv[..., MLA_NOPE:].reshape(depth, MLA_KV_RANK, -1)], axis=-1).astype(BF16)

    zeros64 = jnp.zeros((depth, 1, LANES - MLA_ROPE), F32)
    row = lambda a: a[:, None, :]
    return dict(
        norm_g=row(norm_g), w_in=w_in_p, w_misc=w_misc, b_gate=row(b_gla_gate),
        gla_g=row(gla_norm_g), q_norm_g=row(mla_q_norm_g), w_uq=w_uq_p,
        kv_norm_g=row(mla_kv_norm_g), w_ukv=w_ukv_p,
        qg_nope=row(q_head_g[:, :MLA_NOPE]),
        qg_rope=jnp.concatenate([row(q_head_g[:, MLA_NOPE:])] * 2, axis=-1),
        kg_nope=row(k_head_g[:, :MLA_NOPE]),
        kg_rope=jnp.concatenate([row(k_head_g[:, MLA_NOPE:]), zeros64], axis=-1),
        w_out=w_out.astype(BF16),
    )


def _rope_tables(positions):
    inv_freq = ROPE_THETA ** (-jnp.arange(0, MLA_ROPE, 2, dtype=F32) / MLA_ROPE)
    ang = positions.astype(F32)[..., None] * inv_freq
    cos = jnp.cos(ang).reshape(-1, MLA_ROPE // 2)
    sin = jnp.sin(ang).reshape(-1, MLA_ROPE // 2)
    return (jnp.concatenate([cos, cos, cos, cos], axis=-1),
            jnp.concatenate([-sin, sin, -sin, sin], axis=-1))


def kernel(x, positions, norm_g, w_in, w_gla_gate_up, b_gla_gate, gla_norm_g, mla_q_norm_g, w_uq,
           mla_kv_norm_g, w_ukv, q_head_g, k_head_g, w_out):
    B, S, D = x.shape
    assert D == D_MODEL and S % max(PROJ_TM, GLA_TM, ATTN_TQ, OUT_TM) == 0
    depth = w_in.shape[0]
    weights = _prep_weights(norm_g, w_in, w_gla_gate_up, b_gla_gate, gla_norm_g, mla_q_norm_g, w_uq,
                            mla_kv_norm_g, w_ukv, q_head_g, k_head_g, w_out)
    cos_t, sin_t = _rope_tables(positions)
    x2 = x.reshape(B * S, D)
    for l in range(depth):
        lw = {name: w[l] for name, w in weights.items()}
        qd, ki, ke, gv, dec, gg, q, k, v, mg = _proj_call(x2, cos_t, sin_t, lw, B, S)
        o_gla = _gla_call(qd, ki, ke, gv, dec, gg, lw["gla_g"], S)
        o_mla = _attn_call(q, k, v, mg)
        x2 = _out_call(x2, o_gla, o_mla, lw["w_out"])
    return x2.reshape(B, S, D)
```

```python
import functools

import jax
import jax.numpy as jnp
from jax import lax
from jax.experimental import pallas as pl
from jax.experimental.pallas import tpu as pltpu

F32 = jnp.float32
BF16 = jnp.bfloat16

D_MODEL = 1024
GLA_HEADS = 4
GLA_DK = 64
GLA_DV = 128
GLA_KEY_WIDTH = GLA_HEADS * GLA_DK
GLA_WIDTH = GLA_HEADS * GLA_DV
GLA_GATE_RANK = 16
GLA_GATE_TEMP = 16.0
GLA_CHUNK = 64
MLA_HEADS = 4
MLA_NOPE = 128
MLA_ROPE = 64
MLA_QK = MLA_NOPE + MLA_ROPE
MLA_DV = 128
MLA_WIDTH = MLA_HEADS * MLA_DV
MLA_Q_RANK = 256
MLA_KV_RANK = 128
ROPE_THETA = 10000.0
EPS = 1e-6
LOG2E = 1.4426950408889634
SCORE_BOUND_MAX_LOG2 = 40.0 * LOG2E
SCORE_BOUND_SLACK = 1.02

LANES = 128
QK_PAD = 2 * LANES

OFF_GQ = 0
OFF_GK = 256
OFF_GV = 512
OFF_GGATE = 1024
OFF_CQ = 1536
OFF_CKV = 1792
OFF_MISC = 1920
OFF_MGATE = 2048
D_IN_PAD = 2560

PROJ_TM = 512
GLA_TM = 512
ATTN_TQ = 512
ATTN_TK = 512
OUT_TM = 512
CUMSUM_BLOCK = 256
VMEM_LIMIT = 56 * 1024 * 1024


def _rms(t, g):
    return t * lax.rsqrt(jnp.mean(t * t, axis=-1, keepdims=True) + EPS) * g


def _silu(t):
    return t * jax.nn.sigmoid(t)


def _split_dot(mat_bf16, t):
    hi = t.astype(BF16)
    lo = (t - hi.astype(F32)).astype(BF16)
    return (jnp.dot(mat_bf16, hi, preferred_element_type=F32)
            + jnp.dot(mat_bf16, lo, preferred_element_type=F32))


def _swap_halves(t):
    lane = lax.broadcasted_iota(jnp.int32, t.shape, 1)
    return jnp.where(lane % MLA_ROPE < MLA_ROPE // 2,
                     pltpu.roll(t, LANES - MLA_ROPE // 2, axis=1),
                     pltpu.roll(t, MLA_ROPE // 2, axis=1))


def _proj_kernel(x_ref, cos_ref, sin_ref, ng_ref, win_ref, wmisc_ref, bgate_ref,
                 qng_ref, wuq_ref, kvng_ref, wukv_ref, qgn_ref, qgr_ref, kgn_ref, kgr_ref, kbias_ref,
                 qd_ref, ki_ref, ke_ref, gv_ref, dec_ref, gg_ref,
                 q_ref, k_ref, v_ref, mg_ref):
    tm = x_ref.shape[0]
    x = x_ref[...]
    h = _rms(x, ng_ref[...]).astype(BF16)
    z = jnp.dot(h, win_ref[...], preferred_element_type=F32)

    misc = z[:, OFF_MISC:OFF_MISC + LANES]
    logit = jnp.dot(misc.astype(BF16), wmisc_ref[...], preferred_element_type=F32) + bgate_ref[...]
    log_a = (jnp.minimum(logit, 0.0) - jnp.log1p(jnp.exp(-jnp.abs(logit)))) * (1.0 / GLA_GATE_TEMP)

    cb = CUMSUM_BLOCK
    r = lax.broadcasted_iota(jnp.int32, (cb, cb), 0)
    c = lax.broadcasted_iota(jnp.int32, (cb, cb), 1)
    same = (r // GLA_CHUNK) == (c // GLA_CHUNK)
    lower = jnp.where(same & (c <= r), 1.0, 0.0).astype(BF16)
    upper = jnp.where(same & (c > r), 1.0, 0.0).astype(BF16)
    nck = cb // GLA_CHUNK
    sr = lax.broadcasted_iota(jnp.int32, (8, cb), 0)
    sc = lax.broadcasted_iota(jnp.int32, (8, cb), 1)
    chunk_sum = jnp.where((sc // GLA_CHUNK) == sr, 1.0, 0.0).astype(BF16)

    gq = z[:, OFF_GQ:OFF_GQ + GLA_KEY_WIDTH] * (GLA_DK ** -0.5)
    gk = z[:, OFF_GK:OFF_GK + GLA_KEY_WIDTH]
    for blk in range(tm // cb):
        rows = slice(blk * cb, (blk + 1) * cb)
        la = log_a[rows]
        b = _split_dot(lower, la)
        rest = _split_dot(upper, la)
        tot = _split_dot(chunk_sum, la)
        qd_ref[rows, :] = (gq[rows] * jnp.exp(b)).astype(BF16)
        ki_ref[rows, :] = (gk[rows] * jnp.exp(-b)).astype(BF16)
        ke_ref[rows, :] = (gk[rows] * jnp.exp(rest)).astype(BF16)
        dec_ref[blk * nck:(blk + 1) * nck, :] = jnp.exp(tot[:nck])
    gv_ref[...] = z[:, OFF_GV:OFF_GV + GLA_WIDTH].astype(BF16)
    gg_ref[...] = _silu(z[:, OFF_GGATE:OFF_GGATE + GLA_WIDTH]).astype(BF16)

    cos = cos_ref[...]
    sin = sin_ref[...]
    lane = lax.broadcasted_iota(jnp.int32, (tm, LANES), 1)
    low64 = lane < MLA_ROPE

    cqn = _rms(z[:, OFF_CQ:OFF_CQ + MLA_Q_RANK], qng_ref[...]).astype(BF16)
    qraw = jnp.dot(cqn, wuq_ref[...], preferred_element_type=F32)
    ckvn = _rms(z[:, OFF_CKV:OFF_CKV + MLA_KV_RANK], kvng_ref[...]).astype(BF16)
    kvraw = jnp.dot(ckvn, wukv_ref[...], preferred_element_type=F32)

    scale = MLA_QK ** -0.5 * LOG2E
    one_col = jnp.where(lane == MLA_ROPE, 1.0, 0.0)
    nope_w = MLA_HEADS * MLA_NOPE
    for pair in range(MLA_HEADS // 2):
        t = qraw[:, nope_w + pair * LANES:nope_w + (pair + 1) * LANES]
        t2 = t * t
        ssq_lo = jnp.sum(jnp.where(low64, t2, 0.0), axis=-1, keepdims=True)
        ssq_hi = jnp.sum(jnp.where(low64, 0.0, t2), axis=-1, keepdims=True)
        tg = t * qgr_ref[...]
        rot = tg * cos + _swap_halves(tg) * sin
        for sub in range(2):
            hd = 2 * pair + sub
            qn = qraw[:, hd * MLA_NOPE:(hd + 1) * MLA_NOPE]
            ssq = jnp.sum(qn * qn, axis=-1, keepdims=True) + (ssq_lo if sub == 0 else ssq_hi)
            rinv = lax.rsqrt(ssq * (1.0 / MLA_QK) + EPS) * scale
            q_ref[hd, :, 0:LANES] = (qn * rinv * qgn_ref[...]).astype(BF16)
            rsel = rot if sub == 0 else pltpu.roll(rot, MLA_ROPE, axis=1)
            q_ref[hd, :, LANES:QK_PAD] = jnp.where(low64, rsel * rinv, one_col).astype(BF16)

    kpe2 = misc * misc
    ssq_pe = jnp.sum(jnp.where(low64, kpe2, 0.0), axis=-1, keepdims=True)
    kg = misc * kgr_ref[...]
    krot = kg * cos + _swap_halves(kg) * sin
    for hd in range(MLA_HEADS):
        kn = kvraw[:, hd * MLA_NOPE:(hd + 1) * MLA_NOPE]
        ssq = jnp.sum(kn * kn, axis=-1, keepdims=True) + ssq_pe
        rinv = lax.rsqrt(ssq * (1.0 / MLA_QK) + EPS)
        k_ref[hd, :, 0:LANES] = (kn * rinv * kgn_ref[...]).astype(BF16)
        k_ref[hd, :, LANES:QK_PAD] = (krot * rinv + kbias_ref[...]).astype(BF16)
        v_ref[hd, :, :] = kvraw[:, nope_w + hd * MLA_DV:nope_w + (hd + 1) * MLA_DV].astype(BF16)
    mg_ref[...] = _silu(z[:, OFF_MGATE:OFF_MGATE + MLA_WIDTH]).astype(BF16)


def _gla_kernel(qd_ref, ki_ref, ke_ref, v_ref, dec_ref, gate_ref, g_ref, o_ref, state_ref, *, steps_per_seq):
    tm = qd_ref.shape[0]

    @pl.when(pl.program_id(0) % steps_per_seq == 0)
    def _():
        state_ref[...] = jnp.zeros_like(state_ref)

    C = GLA_CHUNK
    klane_head = lax.broadcasted_iota(jnp.int32, (C, GLA_KEY_WIDTH), 1) // GLA_DK
    vlane_head = lax.broadcasted_iota(jnp.int32, (C, GLA_WIDTH), 1) // GLA_DV
    row = lax.broadcasted_iota(jnp.int32, (C, GLA_KEY_WIDTH), 0)
    col_in_head = lax.broadcasted_iota(jnp.int32, (C, GLA_KEY_WIDTH), 1) % GLA_DK
    causal = row >= col_in_head
    sr = lax.broadcasted_iota(jnp.int32, (GLA_WIDTH, GLA_KEY_WIDTH), 0) // GLA_DV
    sc = lax.broadcasted_iota(jnp.int32, (GLA_WIDTH, GLA_KEY_WIDTH), 1) // GLA_DK
    diag = sr == sc
    nt = (((1,), (1,)), ((), ()))
    tn = (((0,), (0,)), ((), ()))
    g = g_ref[...]

    for ck in range(tm // C):
        rows = slice(ck * C, (ck + 1) * C)
        qd = qd_ref[rows, :]
        ki = ki_ref[rows, :]
        ke = ke_ref[rows, :]
        v = v_ref[rows, :]
        zk = jnp.zeros_like(ki)
        zv = jnp.zeros_like(v)
        kstack = jnp.concatenate([jnp.where(klane_head == hd, ki, zk) for hd in range(GLA_HEADS)], axis=0)
        vstack = jnp.concatenate([jnp.where(vlane_head == hd, v, zv) for hd in range(GLA_HEADS)], axis=0)
        a = lax.dot_general(qd, kstack, nt, preferred_element_type=F32)
        a = jnp.where(causal, a, 0.0).astype(BF16)
        state = state_ref[...]
        o = (jnp.dot(a, vstack, preferred_element_type=F32)
             + lax.dot_general(qd, state.astype(BF16), nt, preferred_element_type=F32))
        upd = lax.dot_general(v, ke, tn, preferred_element_type=F32)
        state_ref[...] = state * dec_ref[ck:ck + 1, :] + jnp.where(diag, upd, 0.0)
        for hd in range(GLA_HEADS):
            cols = slice(hd * GLA_DV, (hd + 1) * GLA_DV)
            oh = o[:, cols]
            on = oh * lax.rsqrt(jnp.mean(oh * oh, axis=-1, keepdims=True) + EPS) * g
            o_ref[rows, cols] = (on * gate_ref[rows, cols].astype(F32)).astype(BF16)


def _attn_kernel(q_ref, k_ref, v_ref, gate_ref, o_ref, m_ref, l_ref, acc_ref):
    tq = q_ref.shape[0]
    tk = ATTN_TK
    i = pl.program_id(2)
    q = q_ref[...]
    nt = (((1,), (1,)), ((), ()))
    m_ref[...] = jnp.full_like(m_ref, -jnp.inf)
    l_ref[...] = jnp.zeros_like(l_ref)
    acc_ref[...] = jnp.zeros_like(acc_ref)

    def step(j, masked):
        start = pl.multiple_of(j * tk, tk)
        k = k_ref[pl.ds(start, tk), :]
        v = v_ref[pl.ds(start, tk), :]
        s = lax.dot_general(q, k, nt, preferred_element_type=F32)
        if masked:
            rr = lax.broadcasted_iota(jnp.int32, s.shape, 0)
            cc = lax.broadcasted_iota(jnp.int32, s.shape, 1)
            s = jnp.where(cc <= rr, s, -jnp.inf)
        m_prev = m_ref[...]
        m_new = jnp.maximum(m_prev, jnp.max(s, axis=-1, keepdims=True))
        alpha = jnp.exp2(m_prev - m_new)
        p = jnp.exp2(s - m_new)
        l_ref[...] = alpha * l_ref[...] + jnp.sum(p, axis=-1, keepdims=True)
        acc_ref[...] = alpha * acc_ref[...] + jnp.dot(p.astype(BF16), v, preferred_element_type=F32)
        m_ref[...] = m_new

    def body(j, carry):
        step(j, False)
        return carry

    lax.fori_loop(0, i * (tq // tk), body, 0)
    step(i * (tq // tk), True)
    o = acc_ref[...] / l_ref[...]
    o_ref[...] = (o * gate_ref[...].astype(F32)).astype(BF16)


def _attn_bounded_kernel(q_ref, k_ref, v_ref, gate_ref, o_ref, acc_ref):
    tq = q_ref.shape[0]
    tk = ATTN_TK
    i = pl.program_id(2)
    q = q_ref[...]
    nt = (((1,), (1,)), ((), ()))
    acc_ref[...] = jnp.zeros_like(acc_ref)
    ones_col = jnp.where(lax.broadcasted_iota(jnp.int32, (tk, LANES), 1) == 0, 1.0, 0.0).astype(BF16)

    def step(j, masked):
        start = pl.multiple_of(j * tk, tk)
        k = k_ref[pl.ds(start, tk), :]
        v_aug = jnp.concatenate([v_ref[pl.ds(start, tk), :], ones_col], axis=1)
        p = jnp.exp2(lax.dot_general(q, k, nt, preferred_element_type=F32))
        if masked:
            rr = lax.broadcasted_iota(jnp.int32, p.shape, 0)
            cc = lax.broadcasted_iota(jnp.int32, p.shape, 1)
            p = jnp.where(cc <= rr, p, 0.0)
        acc_ref[...] += jnp.dot(p.astype(BF16), v_aug, preferred_element_type=F32)

    def body(j, carry):
        step(j, False)
        return carry

    lax.fori_loop(0, i * (tq // tk), body, 0)
    step(i * (tq // tk), True)
    acc = acc_ref[...]
    o = acc[:, :MLA_DV] / acc[:, MLA_DV:MLA_DV + 1]
    o_ref[...] = (o * gate_ref[...].astype(F32)).astype(BF16)


def _out_kernel(x_ref, og_ref, om_ref, w_ref, o_ref):
    w = w_ref[...]
    o_ref[...] = (x_ref[...]
                  + jnp.dot(og_ref[...], w[:GLA_WIDTH], preferred_element_type=F32)
                  + jnp.dot(om_ref[...], w[GLA_WIDTH:], preferred_element_type=F32))


def _full(shape):
    return pl.BlockSpec(shape, lambda *idx: (0,) * len(shape))


def _params(n_axes):
    return pltpu.CompilerParams(dimension_semantics=("arbitrary",) * n_axes,
                                vmem_limit_bytes=VMEM_LIMIT)


def _proj_call(x2, cos_t, sin_t, lw, B, S):
    T = x2.shape[0]
    tm = PROJ_TM
    ns = S // tm
    rows = lambda w: pl.BlockSpec((tm, w), lambda i: (i, 0))
    heads = lambda w: pl.BlockSpec((None, MLA_HEADS, tm, w), lambda i: (i // ns, 0, i % ns, 0))
    out_shape = (
        jax.ShapeDtypeStruct((T, GLA_KEY_WIDTH), BF16),
        jax.ShapeDtypeStruct((T, GLA_KEY_WIDTH), BF16),
        jax.ShapeDtypeStruct((T, GLA_KEY_WIDTH), BF16),
        jax.ShapeDtypeStruct((T, GLA_WIDTH), BF16),
        jax.ShapeDtypeStruct((T // GLA_CHUNK, GLA_KEY_WIDTH), F32),
        jax.ShapeDtypeStruct((T, GLA_WIDTH), BF16),
        jax.ShapeDtypeStruct((B, MLA_HEADS, S, QK_PAD), BF16),
        jax.ShapeDtypeStruct((B, MLA_HEADS, S, QK_PAD), BF16),
        jax.ShapeDtypeStruct((B, MLA_HEADS, S, MLA_DV), BF16),
        jax.ShapeDtypeStruct((T, MLA_WIDTH), BF16),
    )
    out_specs = (rows(GLA_KEY_WIDTH), rows(GLA_KEY_WIDTH), rows(GLA_KEY_WIDTH), rows(GLA_WIDTH),
                 pl.BlockSpec((tm // GLA_CHUNK, GLA_KEY_WIDTH), lambda i: (i, 0)),
                 rows(GLA_WIDTH), heads(QK_PAD), heads(QK_PAD), heads(MLA_DV), rows(MLA_WIDTH))
    in_specs = [rows(D_MODEL), rows(LANES), rows(LANES), _full((1, D_MODEL)),
                _full((D_MODEL, D_IN_PAD)), _full((LANES, GLA_KEY_WIDTH)), _full((1, GLA_KEY_WIDTH)),
                _full((1, MLA_Q_RANK)), _full((MLA_Q_RANK, MLA_HEADS * MLA_QK)),
                _full((1, MLA_KV_RANK)), _full((MLA_KV_RANK, MLA_HEADS * (MLA_NOPE + MLA_DV))),
                _full((1, LANES)), _full((1, LANES)), _full((1, LANES)), _full((1, LANES)),
                _full((1, LANES))]
    return pl.pallas_call(
        _proj_kernel, grid=(T // tm,), in_specs=in_specs, out_specs=out_specs, out_shape=out_shape,
        compiler_params=_params(1), name="proj",
    )(x2, cos_t, sin_t, lw["norm_g"], lw["w_in"], lw["w_misc"], lw["b_gate"],
      lw["q_norm_g"], lw["w_uq"], lw["kv_norm_g"], lw["w_ukv"],
      lw["qg_nope"], lw["qg_rope"], lw["kg_nope"], lw["kg_rope"], lw["k_bias"])


def _gla_call(qd, ki, ke, gv, dec, gg, gla_g, S):
    T = qd.shape[0]
    tm = GLA_TM
    rows = lambda w: pl.BlockSpec((tm, w), lambda i: (i, 0))
    return pl.pallas_call(
        functools.partial(_gla_kernel, steps_per_seq=S // tm),
        grid=(T // tm,),
        in_specs=[rows(GLA_KEY_WIDTH), rows(GLA_KEY_WIDTH), rows(GLA_KEY_WIDTH), rows(GLA_WIDTH),
                  pl.BlockSpec((tm // GLA_CHUNK, GLA_KEY_WIDTH), lambda i: (i, 0)),
                  rows(GLA_WIDTH), _full((1, GLA_DV))],
        out_specs=rows(GLA_WIDTH),
        out_shape=jax.ShapeDtypeStruct((T, GLA_WIDTH), BF16),
        scratch_shapes=[pltpu.VMEM((GLA_WIDTH, GLA_KEY_WIDTH), F32)],
        compiler_params=_params(1), name="gla",
    )(qd, ki, ke, gv, dec, gg, gla_g)


def _attn_call(q, k, v, mg, bounded):
    B, H, S, _ = q.shape
    tq = ATTN_TQ
    nq = S // tq
    if bounded:
        body = _attn_bounded_kernel
        scratch = [pltpu.VMEM((tq, 2 * LANES), F32)]
    else:
        body = _attn_kernel
        scratch = [pltpu.VMEM((tq, 1), F32), pltpu.VMEM((tq, 1), F32), pltpu.VMEM((tq, MLA_DV), F32)]
    return pl.pallas_call(
        body,
        grid=(B, H, nq),
        in_specs=[pl.BlockSpec((None, None, tq, QK_PAD), lambda b, h, i: (b, h, i, 0)),
                  pl.BlockSpec((None, None, S, QK_PAD), lambda b, h, i: (b, h, 0, 0)),
                  pl.BlockSpec((None, None, S, MLA_DV), lambda b, h, i: (b, h, 0, 0)),
                  pl.BlockSpec((tq, MLA_DV), lambda b, h, i: (b * nq + i, h))],
        out_specs=pl.BlockSpec((tq, MLA_DV), lambda b, h, i: (b * nq + i, h)),
        out_shape=jax.ShapeDtypeStruct((B * S, MLA_WIDTH), BF16),
        scratch_shapes=scratch,
        compiler_params=_params(3), name="attn_bounded" if bounded else "attn_online",
    )(q, k, v, mg)


def _out_call(x2, og, om, w_out):
    T = x2.shape[0]
    tm = OUT_TM
    rows = lambda w: pl.BlockSpec((tm, w), lambda i: (i, 0))
    return pl.pallas_call(
        _out_kernel, grid=(T // tm,),
        in_specs=[rows(D_MODEL), rows(GLA_WIDTH), rows(MLA_WIDTH), _full((D_MODEL, D_MODEL))],
        out_specs=rows(D_MODEL),
        out_shape=jax.ShapeDtypeStruct((T, D_MODEL), F32),
        compiler_params=_params(1), name="outproj",
    )(x2, og, om, w_out)


def _prep_weights(norm_g, w_in, w_gla_gate_up, b_gla_gate, gla_norm_g, mla_q_norm_g, w_uq,
                  mla_kv_norm_g, w_ukv, q_head_g, k_head_g, w_out):
    depth = w_in.shape[0]
    o = [0]
    for wdt in (GLA_KEY_WIDTH, GLA_KEY_WIDTH, GLA_WIDTH, GLA_GATE_RANK, GLA_WIDTH,
                MLA_Q_RANK, MLA_KV_RANK, MLA_ROPE, MLA_WIDTH):
        o.append(o[-1] + wdt)
    gq, gk, gv, glr, ggate, cq, ckv, kpe, mgate = [w_in[:, :, o[n]:o[n + 1]] for n in range(9)]
    pad = jnp.zeros((depth, D_MODEL, LANES - MLA_ROPE - GLA_GATE_RANK), w_in.dtype)
    w_in_p = jnp.concatenate([gq, gk, gv, ggate, cq, ckv, kpe, glr, pad, mgate], axis=-1).astype(BF16)

    w_misc = jnp.zeros((depth, LANES, GLA_KEY_WIDTH), F32)
    w_misc = w_misc.at[:, MLA_ROPE:MLA_ROPE + GLA_GATE_RANK, :].set(w_gla_gate_up).astype(BF16)

    wq = w_uq.reshape(depth, MLA_Q_RANK, MLA_HEADS, MLA_QK)
    w_uq_p = jnp.concatenate([wq[..., :MLA_NOPE].reshape(depth, MLA_Q_RANK, -1),
                              wq[..., MLA_NOPE:].reshape(depth, MLA_Q_RANK, -1)], axis=-1).astype(BF16)
    wkv = w_ukv.reshape(depth, MLA_KV_RANK, MLA_HEADS, MLA_NOPE + MLA_DV)
    w_ukv_p = jnp.concatenate([wkv[..., :MLA_NOPE].reshape(depth, MLA_KV_RANK, -1),
                               wkv[..., MLA_NOPE:].reshape(depth, MLA_KV_RANK, -1)], axis=-1).astype(BF16)

    zeros64 = jnp.zeros((depth, 1, LANES - MLA_ROPE), F32)
    row = lambda a: a[:, None, :]
    return dict(
        norm_g=row(norm_g), w_in=w_in_p, w_misc=w_misc, b_gate=row(b_gla_gate),
        gla_g=row(gla_norm_g), q_norm_g=row(mla_q_norm_g), w_uq=w_uq_p,
        kv_norm_g=row(mla_kv_norm_g), w_ukv=w_ukv_p,
        qg_nope=row(q_head_g[:, :MLA_NOPE]),
        qg_rope=jnp.concatenate([row(q_head_g[:, MLA_NOPE:])] * 2, axis=-1),
        kg_nope=row(k_head_g[:, :MLA_NOPE]),
        kg_rope=jnp.concatenate([row(k_head_g[:, MLA_NOPE:]), zeros64], axis=-1),
        w_out=w_out.astype(BF16),
    )


def _score_bounds(q_head_g, k_head_g):
    bound = (MLA_QK ** 0.5 * LOG2E * SCORE_BOUND_SLACK
             * jnp.max(jnp.abs(q_head_g), axis=-1) * jnp.max(jnp.abs(k_head_g), axis=-1))
    use = bound <= SCORE_BOUND_MAX_LOG2
    depth = q_head_g.shape[0]
    bias = jnp.zeros((depth, 1, LANES), F32).at[:, 0, MLA_ROPE].set(jnp.where(use, -bound, 0.0))
    return use, bias


def _rope_tables(positions):
    inv_freq = ROPE_THETA ** (-jnp.arange(0, MLA_ROPE, 2, dtype=F32) / MLA_ROPE)
    ang = positions.astype(F32)[..., None] * inv_freq
    cos = jnp.cos(ang).reshape(-1, MLA_ROPE // 2)
    sin = jnp.sin(ang).reshape(-1, MLA_ROPE // 2)
    return (jnp.concatenate([cos, cos, cos, cos], axis=-1),
            jnp.concatenate([-sin, sin, -sin, sin], axis=-1))


def kernel(x, positions, norm_g, w_in, w_gla_gate_up, b_gla_gate, gla_norm_g, mla_q_norm_g, w_uq,
           mla_kv_norm_g, w_ukv, q_head_g, k_head_g, w_out):
    B, S, D = x.shape
    assert D == D_MODEL and S % max(PROJ_TM, GLA_TM, ATTN_TQ, OUT_TM) == 0
    depth = w_in.shape[0]
    weights = _prep_weights(norm_g, w_in, w_gla_gate_up, b_gla_gate, gla_norm_g, mla_q_norm_g, w_uq,
                            mla_kv_norm_g, w_ukv, q_head_g, k_head_g, w_out)
    use_bounded, weights["k_bias"] = _score_bounds(q_head_g, k_head_g)
    cos_t, sin_t = _rope_tables(positions)
    x2 = x.reshape(B * S, D)
    for l in range(depth):
        lw = {name: w[l] for name, w in weights.items()}
        qd, ki, ke, gv, dec, gg, q, k, v, mg = _proj_call(x2, cos_t, sin_t, lw, B, S)
        o_gla = _gla_call(qd, ki, ke, gv, dec, gg, lw["gla_g"], S)
        o_mla = lax.cond(use_bounded[l],
                         functools.partial(_attn_call, bounded=True),
                         functools.partial(_attn_call, bounded=False), q, k, v, mg)
        x2 = _out_call(x2, o_gla, o_mla, lw["w_out"])
    return x2.reshape(B, S, D)
```

```python
import functools

import jax
import jax.numpy as jnp
from jax import lax
from jax.experimental import pallas as pl
from jax.experimental.pallas import tpu as pltpu

F32 = jnp.float32
BF16 = jnp.bfloat16

D_MODEL = 1024
GLA_HEADS = 4
GLA_DK = 64
GLA_DV = 128
GLA_KEY_WIDTH = GLA_HEADS * GLA_DK
GLA_WIDTH = GLA_HEADS * GLA_DV
GLA_GATE_RANK = 16
GLA_GATE_TEMP = 16.0
GLA_CHUNK = 64
MLA_HEADS = 4
MLA_NOPE = 128
MLA_ROPE = 64
MLA_QK = MLA_NOPE + MLA_ROPE
MLA_DV = 128
MLA_WIDTH = MLA_HEADS * MLA_DV
MLA_Q_RANK = 256
MLA_KV_RANK = 128
ROPE_THETA = 10000.0
EPS = 1e-6
LOG2E = 1.4426950408889634
SCORE_BOUND_MAX_LOG2 = 40.0 * LOG2E
SCORE_BOUND_SLACK = 1.02

LANES = 128
QK_PAD = 2 * LANES

OFF_GQ = 0
OFF_GK = 256
OFF_GV = 512
OFF_GGATE = 1024
OFF_CQ = 1536
OFF_CKV = 1792
OFF_MISC = 1920
OFF_MGATE = 2048
D_IN_PAD = 2560

PROJ_TM = 512
GLA_TM = 512
ATTN_TQ = 512
ATTN_TK = 512
OUT_TM = 512
CUMSUM_BLOCK = 256
VMEM_LIMIT = 56 * 1024 * 1024


def _rms(t, g):
    return t * lax.rsqrt(jnp.mean(t * t, axis=-1, keepdims=True) + EPS) * g


def _silu(t):
    return t * jax.nn.sigmoid(t)


def _split_dot(mat_bf16, t):
    hi = t.astype(BF16)
    lo = (t - hi.astype(F32)).astype(BF16)
    return (jnp.dot(mat_bf16, hi, preferred_element_type=F32)
            + jnp.dot(mat_bf16, lo, preferred_element_type=F32))


def _swap_halves(t):
    lane = lax.broadcasted_iota(jnp.int32, t.shape, 1)
    return jnp.where(lane % MLA_ROPE < MLA_ROPE // 2,
                     pltpu.roll(t, LANES - MLA_ROPE // 2, axis=1),
                     pltpu.roll(t, MLA_ROPE // 2, axis=1))


def _proj_kernel(x_ref, cos_ref, sin_ref, ng_ref, win_ref, wmisc_ref, bgate_ref,
                 qng_ref, wuq_ref, kvng_ref, wukv_ref, qgn_ref, qgr_ref, kgn_ref, kgr_ref, kbias_ref,
                 qd_ref, ki_ref, ke_ref, gv_ref, dec_ref, gg_ref,
                 q_ref, k_ref, v_ref, mg_ref):
    tm = x_ref.shape[0]
    x = x_ref[...]
    h = _rms(x, ng_ref[...]).astype(BF16)
    z = jnp.dot(h, win_ref[...], preferred_element_type=F32)

    misc = z[:, OFF_MISC:OFF_MISC + LANES]
    logit = jnp.dot(misc.astype(BF16), wmisc_ref[...], preferred_element_type=F32) + bgate_ref[...]
    log_a = (jnp.minimum(logit, 0.0) - jnp.log1p(jnp.exp(-jnp.abs(logit)))) * (1.0 / GLA_GATE_TEMP)

    cb = CUMSUM_BLOCK
    r = lax.broadcasted_iota(jnp.int32, (cb, cb), 0)
    c = lax.broadcasted_iota(jnp.int32, (cb, cb), 1)
    same = (r // GLA_CHUNK) == (c // GLA_CHUNK)
    lower = jnp.where(same & (c <= r), 1.0, 0.0).astype(BF16)
    upper = jnp.where(same & (c > r), 1.0, 0.0).astype(BF16)
    nck = cb // GLA_CHUNK
    sr = lax.broadcasted_iota(jnp.int32, (8, cb), 0)
    sc = lax.broadcasted_iota(jnp.int32, (8, cb), 1)
    chunk_sum = jnp.where((sc // GLA_CHUNK) == sr, 1.0, 0.0).astype(BF16)

    gq = z[:, OFF_GQ:OFF_GQ + GLA_KEY_WIDTH] * (GLA_DK ** -0.5)
    gk = z[:, OFF_GK:OFF_GK + GLA_KEY_WIDTH]
    for blk in range(tm // cb):
        rows = slice(blk * cb, (blk + 1) * cb)
        la = log_a[rows]
        b = _split_dot(lower, la)
        rest = _split_dot(upper, la)
        tot = _split_dot(chunk_sum, la)
        qd_ref[rows, :] = (gq[rows] * jnp.exp(b)).astype(BF16)
        ki_ref[rows, :] = (gk[rows] * jnp.exp(-b)).astype(BF16)
        ke_ref[rows, :] = (gk[rows] * jnp.exp(rest)).astype(BF16)
        dec_ref[blk * nck:(blk + 1) * nck, :] = jnp.exp(tot[:nck])
    gv_ref[...] = z[:, OFF_GV:OFF_GV + GLA_WIDTH].astype(BF16)
    gg_ref[...] = _silu(z[:, OFF_GGATE:OFF_GGATE + GLA_WIDTH]).astype(BF16)

    cos = cos_ref[...]
    sin = sin_ref[...]
    lane = lax.broadcasted_iota(jnp.int32, (tm, LANES), 1)
    low64 = lane < MLA_ROPE

    cqn = _rms(z[:, OFF_CQ:OFF_CQ + MLA_Q_RANK], qng_ref[...]).astype(BF16)
    qraw = jnp.dot(cqn, wuq_ref[...], preferred_element_type=F32)
    ckvn = _rms(z[:, OFF_CKV:OFF_CKV + MLA_KV_RANK], kvng_ref[...]).astype(BF16)
    kvraw = jnp.dot(ckvn, wukv_ref[...], preferred_element_type=F32)

    scale = MLA_QK ** -0.5 * LOG2E
    one_col = jnp.where(lane == MLA_ROPE, 1.0, 0.0)
    nope_w = MLA_HEADS * MLA_NOPE
    for pair in range(MLA_HEADS // 2):
        t = qraw[:, nope_w + pair * LANES:nope_w + (pair + 1) * LANES]
        t2 = t * t
        ssq_lo = jnp.sum(jnp.where(low64, t2, 0.0), axis=-1, keepdims=True)
        ssq_hi = jnp.sum(jnp.where(low64, 0.0, t2), axis=-1, keepdims=True)
        tg = t * qgr_ref[...]
        rot = tg * cos + _swap_halves(tg) * sin
        for sub in range(2):
            hd = 2 * pair + sub
            qn = qraw[:, hd * MLA_NOPE:(hd + 1) * MLA_NOPE]
            ssq = jnp.sum(qn * qn, axis=-1, keepdims=True) + (ssq_lo if sub == 0 else ssq_hi)
            rinv = lax.rsqrt(ssq * (1.0 / MLA_QK) + EPS) * scale
            q_ref[hd, :, 0:LANES] = (qn * rinv * qgn_ref[...]).astype(BF16)
            rsel = rot if sub == 0 else pltpu.roll(rot, MLA_ROPE, axis=1)
            q_ref[hd, :, LANES:QK_PAD] = jnp.where(low64, rsel * rinv, one_col).astype(BF16)

    kpe2 = misc * misc
    ssq_pe = jnp.sum(jnp.where(low64, kpe2, 0.0), axis=-1, keepdims=True)
    kg = misc * kgr_ref[...]
    krot = kg * cos + _swap_halves(kg) * sin
    for hd in range(MLA_HEADS):
        kn = kvraw[:, hd * MLA_NOPE:(hd + 1) * MLA_NOPE]
        ssq = jnp.sum(kn * kn, axis=-1, keepdims=True) + ssq_pe
        rinv = lax.rsqrt(ssq * (1.0 / MLA_QK) + EPS)
        k_ref[hd, :, 0:LANES] = (kn * rinv * kgn_ref[...]).astype(BF16)
        k_ref[hd, :, LANES:QK_PAD] = (krot * rinv + kbias_ref[...]).astype(BF16)
        v_ref[hd, :, :] = kvraw[:, nope_w + hd * MLA_DV:nope_w + (hd + 1) * MLA_DV].astype(BF16)
    mg_ref[...] = _silu(z[:, OFF_MGATE:OFF_MGATE + MLA_WIDTH]).astype(BF16)


def _gla_kernel(qd_ref, ki_ref, ke_ref, v_ref, dec_ref, gate_ref, g_ref, o_ref, state_ref, *, steps_per_seq):
    tm = qd_ref.shape[0]

    @pl.when(pl.program_id(0) % steps_per_seq == 0)
    def _():
        state_ref[...] = jnp.zeros_like(state_ref)

    C = GLA_CHUNK
    klane_head = lax.broadcasted_iota(jnp.int32, (C, GLA_KEY_WIDTH), 1) // GLA_DK
    vlane_head = lax.broadcasted_iota(jnp.int32, (C, GLA_WIDTH), 1) // GLA_DV
    row = lax.broadcasted_iota(jnp.int32, (C, GLA_KEY_WIDTH), 0)
    col_in_head = lax.broadcasted_iota(jnp.int32, (C, GLA_KEY_WIDTH), 1) % GLA_DK
    causal = row >= col_in_head
    sr = lax.broadcasted_iota(jnp.int32, (GLA_WIDTH, GLA_KEY_WIDTH), 0) // GLA_DV
    sc = lax.broadcasted_iota(jnp.int32, (GLA_WIDTH, GLA_KEY_WIDTH), 1) // GLA_DK
    diag = sr == sc
    nt = (((1,), (1,)), ((), ()))
    tn = (((0,), (0,)), ((), ()))
    g = g_ref[...]

    for ck in range(tm // C):
        rows = slice(ck * C, (ck + 1) * C)
        qd = qd_ref[rows, :]
        ki = ki_ref[rows, :]
        ke = ke_ref[rows, :]
        v = v_ref[rows, :]
        zk = jnp.zeros_like(ki)
        zv = jnp.zeros_like(v)
        kstack = jnp.concatenate([jnp.where(klane_head == hd, ki, zk) for hd in range(GLA_HEADS)], axis=0)
        vstack = jnp.concatenate([jnp.where(vlane_head == hd, v, zv) for hd in range(GLA_HEADS)], axis=0)
        a = lax.dot_general(qd, kstack, nt, preferred_element_type=F32)
        a = jnp.where(causal, a, 0.0).astype(BF16)
        state = state_ref[...]
        o = (jnp.dot(a, vstack, preferred_element_type=F32)
             + lax.dot_general(qd, state.astype(BF16), nt, preferred_element_type=F32))
        upd = lax.dot_general(v, ke, tn, preferred_element_type=F32)
        state_ref[...] = state * dec_ref[ck:ck + 1, :] + jnp.where(diag, upd, 0.0)
        for hd in range(GLA_HEADS):
            cols = slice(hd * GLA_DV, (hd + 1) * GLA_DV)
            oh = o[:, cols]
            on = oh * lax.rsqrt(jnp.mean(oh * oh, axis=-1, keepdims=True) + EPS) * g
            o_ref[rows, cols] = (on * gate_ref[rows, cols].astype(F32)).astype(BF16)


def _attn_kernel(q_ref, k_ref, v_ref, gate_ref, o_ref, m_ref, l_ref, acc_ref):
    tq = q_ref.shape[0]
    tk = ATTN_TK
    pair = pl.program_id(2)
    i = jnp.where(pl.program_id(3) == 0, pair, pl.num_programs(2) * 2 - 1 - pair)
    q = q_ref[...]
    nt = (((1,), (1,)), ((), ()))
    m_ref[...] = jnp.full_like(m_ref, -jnp.inf)
    l_ref[...] = jnp.zeros_like(l_ref)
    acc_ref[...] = jnp.zeros_like(acc_ref)

    def step(j, masked):
        start = pl.multiple_of(j * tk, tk)
        k = k_ref[pl.ds(start, tk), :]
        v = v_ref[pl.ds(start, tk), :]
        s = lax.dot_general(q, k, nt, preferred_element_type=F32)
        if masked:
            rr = lax.broadcasted_iota(jnp.int32, s.shape, 0)
            cc = lax.broadcasted_iota(jnp.int32, s.shape, 1)
            s = jnp.where(cc <= rr, s, -jnp.inf)
        m_prev = m_ref[...]
        m_new = jnp.maximum(m_prev, jnp.max(s, axis=-1, keepdims=True))
        alpha = jnp.exp2(m_prev - m_new)
        p = jnp.exp2(s - m_new)
        l_ref[...] = alpha * l_ref[...] + jnp.sum(p, axis=-1, keepdims=True)
        acc_ref[...] = alpha * acc_ref[...] + jnp.dot(p.astype(BF16), v, preferred_element_type=F32)
        m_ref[...] = m_new

    def body(j, carry):
        step(j, False)
        return carry

    lax.fori_loop(0, i * (tq // tk), body, 0)
    step(i * (tq // tk), True)
    o = acc_ref[...] / l_ref[...]
    o_ref[...] = (o * gate_ref[...].astype(F32)).astype(BF16)


def _attn_bounded_kernel(q_ref, k_ref, v_ref, gate_ref, o_ref):
    t = q_ref.shape[1]
    n_tiles = k_ref.shape[0] // t
    nt = (((1,), (1,)), ((), ()))
    ones_col = jnp.where(lax.broadcasted_iota(jnp.int32, (t, LANES), 1) == 0, 1.0, 0.0).astype(BF16)
    causal = (lax.broadcasted_iota(jnp.int32, (t, t), 1) <= lax.broadcasted_iota(jnp.int32, (t, t), 0))

    def key_tile(q, j, masked):
        k = k_ref[j * t:(j + 1) * t, :]
        v_aug = jnp.concatenate([v_ref[j * t:(j + 1) * t, :], ones_col], axis=1)
        p = jnp.exp2(lax.dot_general(q, k, nt, preferred_element_type=F32))
        if masked:
            p = jnp.where(causal, p, 0.0)
        return jnp.dot(p.astype(BF16), v_aug, preferred_element_type=F32)

    def query_tile(slot, idx):
        q = q_ref[slot]
        acc = key_tile(q, idx, True)
        for j in range(idx):
            acc = acc + key_tile(q, j, False)
        o = acc[:, :MLA_DV] * (1.0 / acc[:, MLA_DV:MLA_DV + 1])
        o_ref[slot] = (o * gate_ref[slot].astype(F32)).astype(BF16)

    for pair in range(n_tiles // 2):
        @pl.when(pl.program_id(2) == pair)
        def _(pair=pair):
            query_tile(0, pair)
            query_tile(1, n_tiles - 1 - pair)


def _out_kernel(x_ref, og_ref, om_ref, w_ref, o_ref):
    w = w_ref[...]
    o_ref[...] = (x_ref[...]
                  + jnp.dot(og_ref[...], w[:GLA_WIDTH], preferred_element_type=F32)
                  + jnp.dot(om_ref[...], w[GLA_WIDTH:], preferred_element_type=F32))


def _full(shape):
    return pl.BlockSpec(shape, lambda *idx: (0,) * len(shape))


def _pair_slot(n_tiles):
    half = n_tiles // 2
    pair = lambda t: jnp.where(t < half, t, n_tiles - 1 - t)
    slot = lambda t: jnp.where(t < half, 0, 1)
    return pair, slot


def _params(n_axes):
    return pltpu.CompilerParams(dimension_semantics=("arbitrary",) * n_axes,
                                vmem_limit_bytes=VMEM_LIMIT)


def _proj_call(x2, cos_t, sin_t, lw, B, S):
    T = x2.shape[0]
    tm = PROJ_TM
    ns = S // tm
    rows = lambda w: pl.BlockSpec((tm, w), lambda i: (i, 0))
    heads = lambda w: pl.BlockSpec((None, MLA_HEADS, tm, w), lambda i: (i // ns, 0, i % ns, 0))
    pair, slot = _pair_slot(ns)
    out_shape = (
        jax.ShapeDtypeStruct((T, GLA_KEY_WIDTH), BF16),
        jax.ShapeDtypeStruct((T, GLA_KEY_WIDTH), BF16),
        jax.ShapeDtypeStruct((T, GLA_KEY_WIDTH), BF16),
        jax.ShapeDtypeStruct((T, GLA_WIDTH), BF16),
        jax.ShapeDtypeStruct((T // GLA_CHUNK, GLA_KEY_WIDTH), F32),
        jax.ShapeDtypeStruct((T, GLA_WIDTH), BF16),
        jax.ShapeDtypeStruct((B, MLA_HEADS, ns // 2, 2, tm, QK_PAD), BF16),
        jax.ShapeDtypeStruct((B, MLA_HEADS, S, QK_PAD), BF16),
        jax.ShapeDtypeStruct((B, MLA_HEADS, S, MLA_DV), BF16),
        jax.ShapeDtypeStruct((B, ns // 2, 2, tm, MLA_WIDTH), BF16),
    )
    q_spec = pl.BlockSpec((None, MLA_HEADS, None, None, tm, QK_PAD),
                          lambda i: (i // ns, 0, pair(i % ns), slot(i % ns), 0, 0))
    mg_spec = pl.BlockSpec((None, None, None, tm, MLA_WIDTH),
                           lambda i: (i // ns, pair(i % ns), slot(i % ns), 0, 0))
    out_specs = (rows(GLA_KEY_WIDTH), rows(GLA_KEY_WIDTH), rows(GLA_KEY_WIDTH), rows(GLA_WIDTH),
                 pl.BlockSpec((tm // GLA_CHUNK, GLA_KEY_WIDTH), lambda i: (i, 0)),
                 rows(GLA_WIDTH), q_spec, heads(QK_PAD), heads(MLA_DV), mg_spec)
    in_specs = [rows(D_MODEL), rows(LANES), rows(LANES), _full((1, D_MODEL)),
                _full((D_MODEL, D_IN_PAD)), _full((LANES, GLA_KEY_WIDTH)), _full((1, GLA_KEY_WIDTH)),
                _full((1, MLA_Q_RANK)), _full((MLA_Q_RANK, MLA_HEADS * MLA_QK)),
                _full((1, MLA_KV_RANK)), _full((MLA_KV_RANK, MLA_HEADS * (MLA_NOPE + MLA_DV))),
                _full((1, LANES)), _full((1, LANES)), _full((1, LANES)), _full((1, LANES)),
                _full((1, LANES))]
    return pl.pallas_call(
        _proj_kernel, grid=(T // tm,), in_specs=in_specs, out_specs=out_specs, out_shape=out_shape,
        compiler_params=_params(1), name="proj",
    )(x2, cos_t, sin_t, lw["norm_g"], lw["w_in"], lw["w_misc"], lw["b_gate"],
      lw["q_norm_g"], lw["w_uq"], lw["kv_norm_g"], lw["w_ukv"],
      lw["qg_nope"], lw["qg_rope"], lw["kg_nope"], lw["kg_rope"], lw["k_bias"])


def _gla_call(qd, ki, ke, gv, dec, gg, gla_g, S):
    T = qd.shape[0]
    tm = GLA_TM
    rows = lambda w: pl.BlockSpec((tm, w), lambda i: (i, 0))
    return pl.pallas_call(
        functools.partial(_gla_kernel, steps_per_seq=S // tm),
        grid=(T // tm,),
        in_specs=[rows(GLA_KEY_WIDTH), rows(GLA_KEY_WIDTH), rows(GLA_KEY_WIDTH), rows(GLA_WIDTH),
                  pl.BlockSpec((tm // GLA_CHUNK, GLA_KEY_WIDTH), lambda i: (i, 0)),
                  rows(GLA_WIDTH), _full((1, GLA_DV))],
        out_specs=rows(GLA_WIDTH),
        out_shape=jax.ShapeDtypeStruct((T, GLA_WIDTH), BF16),
        scratch_shapes=[pltpu.VMEM((GLA_WIDTH, GLA_KEY_WIDTH), F32)],
        compiler_params=_params(1), name="gla",
    )(qd, ki, ke, gv, dec, gg, gla_g)


def _attn_call(q, k, v, mg, bounded):
    B, H, n_pairs, _, tq, _ = q.shape
    S = k.shape[2]
    out_shape = jax.ShapeDtypeStruct(mg.shape, BF16)
    kv_specs = [pl.BlockSpec((None, None, S, QK_PAD), lambda b, h, *_: (b, h, 0, 0)),
                pl.BlockSpec((None, None, S, MLA_DV), lambda b, h, *_: (b, h, 0, 0))]
    if bounded:
        q_spec = pl.BlockSpec((None, None, None, 2, tq, QK_PAD), lambda b, h, p: (b, h, p, 0, 0, 0))
        io_spec = pl.BlockSpec((None, None, 2, tq, MLA_DV), lambda b, h, p: (b, p, 0, 0, h))
        return pl.pallas_call(
            _attn_bounded_kernel, grid=(B, H, n_pairs),
            in_specs=[q_spec] + kv_specs + [io_spec], out_specs=io_spec, out_shape=out_shape,
            compiler_params=_params(3), name="attn_bounded",
        )(q, k, v, mg)
    q_spec = pl.BlockSpec((None, None, None, None, tq, QK_PAD), lambda b, h, p, s: (b, h, p, s, 0, 0))
    io_spec = pl.BlockSpec((None, None, None, tq, MLA_DV), lambda b, h, p, s: (b, p, s, 0, h))
    return pl.pallas_call(
        _attn_kernel, grid=(B, H, n_pairs, 2),
        in_specs=[q_spec] + kv_specs + [io_spec], out_specs=io_spec, out_shape=out_shape,
        scratch_shapes=[pltpu.VMEM((tq, 1), F32), pltpu.VMEM((tq, 1), F32), pltpu.VMEM((tq, MLA_DV), F32)],
        compiler_params=_params(4), name="attn_online",
    )(q, k, v, mg)


def _out_call(x2, og, om, w_out):
    T = x2.shape[0]
    tm = OUT_TM
    ns = om.shape[1] * 2
    rows = lambda w: pl.BlockSpec((tm, w), lambda i: (i, 0))
    pair, slot = _pair_slot(ns)
    om_spec = pl.BlockSpec((None, None, None, tm, MLA_WIDTH),
                           lambda i: (i // ns, pair(i % ns), slot(i % ns), 0, 0))
    return pl.pallas_call(
        _out_kernel, grid=(T // tm,),
        in_specs=[rows(D_MODEL), rows(GLA_WIDTH), om_spec, _full((D_MODEL, D_MODEL))],
        out_specs=rows(D_MODEL),
        out_shape=jax.ShapeDtypeStruct((T, D_MODEL), F32),
        compiler_params=_params(1), name="outproj",
    )(x2, og, om, w_out)


def _prep_weights(norm_g, w_in, w_gla_gate_up, b_gla_gate, gla_norm_g, mla_q_norm_g, w_uq,
                  mla_kv_norm_g, w_ukv, q_head_g, k_head_g, w_out):
    depth = w_in.shape[0]
    o = [0]
    for wdt in (GLA_KEY_WIDTH, GLA_KEY_WIDTH, GLA_WIDTH, GLA_GATE_RANK, GLA_WIDTH,
                MLA_Q_RANK, MLA_KV_RANK, MLA_ROPE, MLA_WIDTH):
        o.append(o[-1] + wdt)
    gq, gk, gv, glr, ggate, cq, ckv, kpe, mgate = [w_in[:, :, o[n]:o[n + 1]] for n in range(9)]
    pad = jnp.zeros((depth, D_MODEL, LANES - MLA_ROPE - GLA_GATE_RANK), w_in.dtype)
    w_in_p = jnp.concatenate([gq, gk, gv, ggate, cq, ckv, kpe, glr, pad, mgate], axis=-1).astype(BF16)

    w_misc = jnp.zeros((depth, LANES, GLA_KEY_WIDTH), F32)
    w_misc = w_misc.at[:, MLA_ROPE:MLA_ROPE + GLA_GATE_RANK, :].set(w_gla_gate_up).astype(BF16)

    wq = w_uq.reshape(depth, MLA_Q_RANK, MLA_HEADS, MLA_QK)
    w_uq_p = jnp.concatenate([wq[..., :MLA_NOPE].reshape(depth, MLA_Q_RANK, -1),
                              wq[..., MLA_NOPE:].reshape(depth, MLA_Q_RANK, -1)], axis=-1).astype(BF16)
    wkv = w_ukv.reshape(depth, MLA_KV_RANK, MLA_HEADS, MLA_NOPE + MLA_DV)
    w_ukv_p = jnp.concatenate([wkv[..., :MLA_NOPE].reshape(depth, MLA_KV_RANK, -1),
                               wkv[..., MLA_NOPE:].reshape(depth, MLA_KV_RANK, -1)], axis=-1).astype(BF16)

    zeros64 = jnp.zeros((depth, 1, LANES - MLA_ROPE), F32)
    row = lambda a: a[:, None, :]
    return dict(
        norm_g=row(norm_g), w_in=w_in_p, w_misc=w_misc, b_gate=row(b_gla_gate),
        gla_g=row(gla_norm_g), q_norm_g=row(mla_q_norm_g), w_uq=w_uq_p,
        kv_norm_g=row(mla_kv_norm_g), w_ukv=w_ukv_p,
        qg_nope=row(q_head_g[:, :MLA_NOPE]),
        qg_rope=jnp.concatenate([row(q_head_g[:, MLA_NOPE:])] * 2, axis=-1),
        kg_nope=row(k_head_g[:, :MLA_NOPE]),
        kg_rope=jnp.concatenate([row(k_head_g[:, MLA_NOPE:]), zeros64], axis=-1),
        w_out=w_out.astype(BF16),
    )


def _score_bounds(q_head_g, k_head_g):
    bound = (MLA_QK ** 0.5 * LOG2E * SCORE_BOUND_SLACK
             * jnp.max(jnp.abs(q_head_g), axis=-1) * jnp.max(jnp.abs(k_head_g), axis=-1))
    use = bound <= SCORE_BOUND_MAX_LOG2
    depth = q_head_g.shape[0]
    bias = jnp.zeros((depth, 1, LANES), F32).at[:, 0, MLA_ROPE].set(jnp.where(use, -bound, 0.0))
    return use, bias


def _rope_tables(positions):
    inv_freq = ROPE_THETA ** (-jnp.arange(0, MLA_ROPE, 2, dtype=F32) / MLA_ROPE)
    ang = positions.astype(F32)[..., None] * inv_freq
    cos = jnp.cos(ang).reshape(-1, MLA_ROPE // 2)
    sin = jnp.sin(ang).reshape(-1, MLA_ROPE // 2)
    return (jnp.concatenate([cos, cos, cos, cos], axis=-1),
            jnp.concatenate([-sin, sin, -sin, sin], axis=-1))


def kernel(x, positions, norm_g, w_in, w_gla_gate_up, b_gla_gate, gla_norm_g, mla_q_norm_g, w_uq,
           mla_kv_norm_g, w_ukv, q_head_g, k_head_g, w_out):
    B, S, D = x.shape
    assert D == D_MODEL and S % GLA_TM == 0 and S % (2 * ATTN_TQ) == 0
    assert PROJ_TM == ATTN_TQ == OUT_TM and ATTN_TQ % ATTN_TK == 0
    depth = w_in.shape[0]
    weights = _prep_weights(norm_g, w_in, w_gla_gate_up, b_gla_gate, gla_norm_g, mla_q_norm_g, w_uq,
                            mla_kv_norm_g, w_ukv, q_head_g, k_head_g, w_out)
    use_bounded, weights["k_bias"] = _score_bounds(q_head_g, k_head_g)
    cos_t, sin_t = _rope_tables(positions)
    x2 = x.reshape(B * S, D)
    for l in range(depth):
        lw = {name: w[l] for name, w in weights.items()}
        qd, ki, ke, gv, dec, gg, q, k, v, mg = _proj_call(x2, cos_t, sin_t, lw, B, S)
        o_gla = _gla_call(qd, ki, ke, gv, dec, gg, lw["gla_g"], S)
        o_mla = lax.cond(use_bounded[l],
                         functools.partial(_attn_call, bounded=True),
                         functools.partial(_attn_call, bounded=False), q, k, v, mg)
        x2 = _out_call(x2, o_gla, o_mla, lw["w_out"])
    return x2.reshape(B, S, D)
```

```python
import functools

import jax
import jax.numpy as jnp
from jax import lax
from jax.experimental import pallas as pl
from jax.experimental.pallas import tpu as pltpu

F32 = jnp.float32
BF16 = jnp.bfloat16

D_MODEL = 1024
GLA_HEADS = 4
GLA_DK = 64
GLA_DV = 128
GLA_KEY_WIDTH = GLA_HEADS * GLA_DK
GLA_WIDTH = GLA_HEADS * GLA_DV
GLA_GATE_RANK = 16
GLA_GATE_TEMP = 16.0
GLA_CHUNK = 64
MLA_HEADS = 4
MLA_NOPE = 128
MLA_ROPE = 64
MLA_QK = MLA_NOPE + MLA_ROPE
MLA_DV = 128
MLA_WIDTH = MLA_HEADS * MLA_DV
MLA_Q_RANK = 256
MLA_KV_RANK = 128
ROPE_THETA = 10000.0
EPS = 1e-6
LOG2E = 1.4426950408889634
SCORE_BOUND_MAX_LOG2 = 40.0 * LOG2E
SCORE_BOUND_SLACK = 1.02

LANES = 128
QK_PAD = 2 * LANES

OFF_GQ = 0
OFF_GK = 256
OFF_GV = 512
OFF_GGATE = 1024
OFF_CQ = 1536
OFF_CKV = 1792
OFF_MISC = 1920
OFF_MGATE = 2048
D_IN_PAD = 2560

PROJ_TM = 512
GLA_TM = 512
ATTN_TQ = 512
ATTN_TK = 512
OUT_TM = 512
CUMSUM_BLOCK = 256
PROJ_CHUNK = 256
VMEM_LIMIT = 56 * 1024 * 1024


def _rms(t, g):
    return t * lax.rsqrt(jnp.mean(t * t, axis=-1, keepdims=True) + EPS) * g


def _silu(t):
    return t * jax.nn.sigmoid(t)


def _split_dot(mat_bf16, t):
    hi = t.astype(BF16)
    lo = (t - hi.astype(F32)).astype(BF16)
    return (jnp.dot(mat_bf16, hi, preferred_element_type=F32)
            + jnp.dot(mat_bf16, lo, preferred_element_type=F32))


def _swap_halves(t):
    lane = lax.broadcasted_iota(jnp.int32, t.shape, 1)
    return jnp.where(lane % MLA_ROPE < MLA_ROPE // 2,
                     pltpu.roll(t, LANES - MLA_ROPE // 2, axis=1),
                     pltpu.roll(t, MLA_ROPE // 2, axis=1))


N_POST_IN = 13
N_POST_OUT = 10


def _proj_kernel(*refs, fuse_out):
    refs = list(refs)
    x_ref = refs.pop(0)
    if fuse_out:
        og_ref, om_ref, wout_ref = refs[:3]
        del refs[:3]
    ng_ref, win_ref = refs[:2]
    post_in = refs[2:2 + N_POST_IN]
    del refs[:2 + N_POST_IN]
    if fuse_out:
        xo_ref = refs.pop(0)
    post_out = refs[:N_POST_OUT]
    z0_ref, z1_ref = refs[N_POST_OUT:]
    i = pl.program_id(0)

    @pl.when(i == 0)
    def _():
        z1_ref[...] = jnp.zeros_like(z1_ref)

    def step(z_read, z_write):
        post = _proj_post(z_read, *post_in, *post_out)
        x = x_ref[...]
        if fuse_out:
            w = wout_ref[...]
            x = (x + jnp.dot(og_ref[...], w[:GLA_WIDTH], preferred_element_type=F32)
                 + jnp.dot(om_ref[...], w[GLA_WIDTH:], preferred_element_type=F32))
            xo_ref[...] = x
        h = _rms(x, ng_ref[...]).astype(BF16)
        n_chunks = D_IN_PAD // PROJ_CHUNK
        for c in range(n_chunks):
            if c in (0, n_chunks // 2):
                next(post)
            cols = slice(c * PROJ_CHUNK, (c + 1) * PROJ_CHUNK)
            z_write[:, cols] = jnp.dot(h, win_ref[:, cols], preferred_element_type=F32)
        for _ in post:
            pass

    @pl.when(i % 2 == 0)
    def _():
        step(z1_ref, z0_ref)

    @pl.when(i % 2 == 1)
    def _():
        step(z0_ref, z1_ref)


def _proj_post(z, cos_ref, sin_ref, wmisc_ref, bgate_ref,
               qng_ref, wuq_ref, kvng_ref, wukv_ref, qgn_ref, qgr_ref, kgn_ref, kgr_ref, kbias_ref,
               qd_ref, ki_ref, ke_ref, gv_ref, dec_ref, gg_ref, q_ref, k_ref, v_ref, mg_ref):
    tm = z.shape[0]

    misc = z[:, OFF_MISC:OFF_MISC + LANES]
    logit = jnp.dot(misc.astype(BF16), wmisc_ref[...], preferred_element_type=F32) + bgate_ref[...]
    cqn = _rms(z[:, OFF_CQ:OFF_CQ + MLA_Q_RANK], qng_ref[...]).astype(BF16)
    qraw = jnp.dot(cqn, wuq_ref[...], preferred_element_type=F32)
    ckvn = _rms(z[:, OFF_CKV:OFF_CKV + MLA_KV_RANK], kvng_ref[...]).astype(BF16)
    kvraw = jnp.dot(ckvn, wukv_ref[...], preferred_element_type=F32)
    yield

    log_a = (jnp.minimum(logit, 0.0) - jnp.log1p(jnp.exp(-jnp.abs(logit)))) * (1.0 / GLA_GATE_TEMP)

    cb = CUMSUM_BLOCK
    r = lax.broadcasted_iota(jnp.int32, (cb, cb), 0)
    c = lax.broadcasted_iota(jnp.int32, (cb, cb), 1)
    same = (r // GLA_CHUNK) == (c // GLA_CHUNK)
    lower = jnp.where(same & (c <= r), 1.0, 0.0).astype(BF16)
    upper = jnp.where(same & (c > r), 1.0, 0.0).astype(BF16)
    nck = cb // GLA_CHUNK
    sr = lax.broadcasted_iota(jnp.int32, (8, cb), 0)
    sc = lax.broadcasted_iota(jnp.int32, (8, cb), 1)
    chunk_sum = jnp.where((sc // GLA_CHUNK) == sr, 1.0, 0.0).astype(BF16)

    gq = z[:, OFF_GQ:OFF_GQ + GLA_KEY_WIDTH] * (GLA_DK ** -0.5)
    gk = z[:, OFF_GK:OFF_GK + GLA_KEY_WIDTH]
    for blk in range(tm // cb):
        rows = slice(blk * cb, (blk + 1) * cb)
        la = log_a[rows]
        b = _split_dot(lower, la)
        rest = _split_dot(upper, la)
        tot = _split_dot(chunk_sum, la)
        qd_ref[rows, :] = (gq[rows] * jnp.exp(b)).astype(BF16)
        ki_ref[rows, :] = (gk[rows] * jnp.exp(-b)).astype(BF16)
        ke_ref[rows, :] = (gk[rows] * jnp.exp(rest)).astype(BF16)
        dec_ref[blk * nck:(blk + 1) * nck, :] = jnp.exp(tot[:nck])
    yield
    gv_ref[...] = z[:, OFF_GV:OFF_GV + GLA_WIDTH].astype(BF16)
    gg_ref[...] = _silu(z[:, OFF_GGATE:OFF_GGATE + GLA_WIDTH]).astype(BF16)

    cos = cos_ref[...]
    sin = sin_ref[...]
    lane = lax.broadcasted_iota(jnp.int32, (tm, LANES), 1)
    low64 = lane < MLA_ROPE

    scale = MLA_QK ** -0.5 * LOG2E
    one_col = jnp.where(lane == MLA_ROPE, 1.0, 0.0)
    nope_w = MLA_HEADS * MLA_NOPE
    for pair in range(MLA_HEADS // 2):
        t = qraw[:, nope_w + pair * LANES:nope_w + (pair + 1) * LANES]
        t2 = t * t
        ssq_lo = jnp.sum(jnp.where(low64, t2, 0.0), axis=-1, keepdims=True)
        ssq_hi = jnp.sum(jnp.where(low64, 0.0, t2), axis=-1, keepdims=True)
        tg = t * qgr_ref[...]
        rot = tg * cos + _swap_halves(tg) * sin
        for sub in range(2):
            hd = 2 * pair + sub
            qn = qraw[:, hd * MLA_NOPE:(hd + 1) * MLA_NOPE]
            ssq = jnp.sum(qn * qn, axis=-1, keepdims=True) + (ssq_lo if sub == 0 else ssq_hi)
            rinv = lax.rsqrt(ssq * (1.0 / MLA_QK) + EPS) * scale
            q_ref[hd, :, 0:LANES] = (qn * rinv * qgn_ref[...]).astype(BF16)
            rsel = rot if sub == 0 else pltpu.roll(rot, MLA_ROPE, axis=1)
            q_ref[hd, :, LANES:QK_PAD] = jnp.where(low64, rsel * rinv, one_col).astype(BF16)

    kpe2 = misc * misc
    ssq_pe = jnp.sum(jnp.where(low64, kpe2, 0.0), axis=-1, keepdims=True)
    kg = misc * kgr_ref[...]
    krot = kg * cos + _swap_halves(kg) * sin
    for hd in range(MLA_HEADS):
        kn = kvraw[:, hd * MLA_NOPE:(hd + 1) * MLA_NOPE]
        ssq = jnp.sum(kn * kn, axis=-1, keepdims=True) + ssq_pe
        rinv = lax.rsqrt(ssq * (1.0 / MLA_QK) + EPS)
        k_ref[hd, :, 0:LANES] = (kn * rinv * kgn_ref[...]).astype(BF16)
        k_ref[hd, :, LANES:QK_PAD] = (krot * rinv + kbias_ref[...]).astype(BF16)
        v_ref[hd, :, :] = kvraw[:, nope_w + hd * MLA_DV:nope_w + (hd + 1) * MLA_DV].astype(BF16)
    mg_ref[...] = _silu(z[:, OFF_MGATE:OFF_MGATE + MLA_WIDTH]).astype(BF16)


def _gla_kernel(qd_ref, ki_ref, ke_ref, v_ref, dec_ref, gate_ref, g_ref, o_ref, state_ref, *, steps_per_seq):
    tm = qd_ref.shape[0]

    @pl.when(pl.program_id(0) % steps_per_seq == 0)
    def _():
        state_ref[...] = jnp.zeros_like(state_ref)

    C = GLA_CHUNK
    klane_head = lax.broadcasted_iota(jnp.int32, (C, GLA_KEY_WIDTH), 1) // GLA_DK
    vlane_head = lax.broadcasted_iota(jnp.int32, (C, GLA_WIDTH), 1) // GLA_DV
    row = lax.broadcasted_iota(jnp.int32, (C, GLA_KEY_WIDTH), 0)
    col_in_head = lax.broadcasted_iota(jnp.int32, (C, GLA_KEY_WIDTH), 1) % GLA_DK
    causal = row >= col_in_head
    sr = lax.broadcasted_iota(jnp.int32, (GLA_WIDTH, GLA_KEY_WIDTH), 0) // GLA_DV
    sc = lax.broadcasted_iota(jnp.int32, (GLA_WIDTH, GLA_KEY_WIDTH), 1) // GLA_DK
    diag = sr == sc
    nt = (((1,), (1,)), ((), ()))
    tn = (((0,), (0,)), ((), ()))
    g = g_ref[...]

    for ck in range(tm // C):
        rows = slice(ck * C, (ck + 1) * C)
        qd = qd_ref[rows, :]
        ki = ki_ref[rows, :]
        ke = ke_ref[rows, :]
        v = v_ref[rows, :]
        zk = jnp.zeros_like(ki)
        zv = jnp.zeros_like(v)
        kstack = jnp.concatenate([jnp.where(klane_head == hd, ki, zk) for hd in range(GLA_HEADS)], axis=0)
        vstack = jnp.concatenate([jnp.where(vlane_head == hd, v, zv) for hd in range(GLA_HEADS)], axis=0)
        a = lax.dot_general(qd, kstack, nt, preferred_element_type=F32)
        a = jnp.where(causal, a, 0.0).astype(BF16)
        state = state_ref[...]
        o = (jnp.dot(a, vstack, preferred_element_type=F32)
             + lax.dot_general(qd, state.astype(BF16), nt, preferred_element_type=F32))
        upd = lax.dot_general(v, ke, tn, preferred_element_type=F32)
        state_ref[...] = state * dec_ref[ck:ck + 1, :] + jnp.where(diag, upd, 0.0)
        for hd in range(GLA_HEADS):
            cols = slice(hd * GLA_DV, (hd + 1) * GLA_DV)
            oh = o[:, cols]
            on = oh * lax.rsqrt(jnp.mean(oh * oh, axis=-1, keepdims=True) + EPS) * g
            o_ref[rows, cols] = (on * gate_ref[rows, cols].astype(F32)).astype(BF16)


def _attn_kernel(q_ref, k_ref, v_ref, gate_ref, o_ref, m_ref, l_ref, acc_ref):
    tq = q_ref.shape[0]
    tk = ATTN_TK
    pair = pl.program_id(2)
    i = jnp.where(pl.program_id(3) == 0, pair, pl.num_programs(2) * 2 - 1 - pair)
    q = q_ref[...]
    nt = (((1,), (1,)), ((), ()))
    m_ref[...] = jnp.full_like(m_ref, -jnp.inf)
    l_ref[...] = jnp.zeros_like(l_ref)
    acc_ref[...] = jnp.zeros_like(acc_ref)

    def step(j, masked):
        start = pl.multiple_of(j * tk, tk)
        k = k_ref[pl.ds(start, tk), :]
        v = v_ref[pl.ds(start, tk), :]
        s = lax.dot_general(q, k, nt, preferred_element_type=F32)
        if masked:
            rr = lax.broadcasted_iota(jnp.int32, s.shape, 0)
            cc = lax.broadcasted_iota(jnp.int32, s.shape, 1)
            s = jnp.where(cc <= rr, s, -jnp.inf)
        m_prev = m_ref[...]
        m_new = jnp.maximum(m_prev, jnp.max(s, axis=-1, keepdims=True))
        alpha = jnp.exp2(m_prev - m_new)
        p = jnp.exp2(s - m_new)
        l_ref[...] = alpha * l_ref[...] + jnp.sum(p, axis=-1, keepdims=True)
        acc_ref[...] = alpha * acc_ref[...] + jnp.dot(p.astype(BF16), v, preferred_element_type=F32)
        m_ref[...] = m_new

    def body(j, carry):
        step(j, False)
        return carry

    lax.fori_loop(0, i * (tq // tk), body, 0)
    step(i * (tq // tk), True)
    o = acc_ref[...] / l_ref[...]
    o_ref[...] = (o * gate_ref[...].astype(F32)).astype(BF16)


def _attn_bounded_kernel(q_ref, k_ref, v_ref, gate_ref, o_ref):
    t = q_ref.shape[1]
    n_tiles = k_ref.shape[0] // t
    nt = (((1,), (1,)), ((), ()))
    ones_col = jnp.where(lax.broadcasted_iota(jnp.int32, (t, LANES), 1) == 0, 1.0, 0.0).astype(BF16)
    causal = (lax.broadcasted_iota(jnp.int32, (t, t), 1) <= lax.broadcasted_iota(jnp.int32, (t, t), 0))

    def key_tile(q, j, masked):
        k = k_ref[j * t:(j + 1) * t, :]
        v_aug = jnp.concatenate([v_ref[j * t:(j + 1) * t, :], ones_col], axis=1)
        p = jnp.exp2(lax.dot_general(q, k, nt, preferred_element_type=F32))
        if masked:
            p = jnp.where(causal, p, 0.0)
        return jnp.dot(p.astype(BF16), v_aug, preferred_element_type=F32)

    def query_tile(slot, idx):
        q = q_ref[slot]
        acc = key_tile(q, idx, True)
        for j in range(idx):
            acc = acc + key_tile(q, j, False)
        o = acc[:, :MLA_DV] * (1.0 / acc[:, MLA_DV:MLA_DV + 1])
        o_ref[slot] = (o * gate_ref[slot].astype(F32)).astype(BF16)

    for pair in range(n_tiles // 2):
        @pl.when(pl.program_id(2) == pair)
        def _(pair=pair):
            query_tile(0, pair)
            query_tile(1, n_tiles - 1 - pair)


def _out_kernel(x_ref, og_ref, om_ref, w_ref, o_ref):
    w = w_ref[...]
    o_ref[...] = (x_ref[...]
                  + jnp.dot(og_ref[...], w[:GLA_WIDTH], preferred_element_type=F32)
                  + jnp.dot(om_ref[...], w[GLA_WIDTH:], preferred_element_type=F32))


def _full(shape):
    return pl.BlockSpec(shape, lambda *idx: (0,) * len(shape), pipeline_mode=pl.Buffered(1))


def _pair_slot(n_tiles):
    half = n_tiles // 2
    pair = lambda t: jnp.where(t < half, t, n_tiles - 1 - t)
    slot = lambda t: jnp.where(t < half, 0, 1)
    return pair, slot


def _params(n_axes):
    return pltpu.CompilerParams(dimension_semantics=("arbitrary",) * n_axes,
                                vmem_limit_bytes=VMEM_LIMIT)


def _proj_call(x2, prev, cos_t, sin_t, lw, B, S):
    T = x2.shape[0]
    tm = PROJ_TM
    ns = S // tm
    n = T // tm
    fuse_out = prev is not None
    pair, slot = _pair_slot(ns)
    cur = lambda i: jnp.minimum(i, n - 1)
    lag = lambda i: jnp.maximum(i - 1, 0)
    rows = lambda w, at: pl.BlockSpec((tm, w), lambda i: (at(i), 0))
    heads = lambda w: pl.BlockSpec((None, MLA_HEADS, tm, w), lambda i: (lag(i) // ns, 0, lag(i) % ns, 0))
    tiles = lambda at: pl.BlockSpec((None, None, None, tm, MLA_WIDTH),
                                    lambda i: (at(i) // ns, pair(at(i) % ns), slot(at(i) % ns), 0, 0))
    q_spec = pl.BlockSpec((None, MLA_HEADS, None, None, tm, QK_PAD),
                          lambda i: (lag(i) // ns, 0, pair(lag(i) % ns), slot(lag(i) % ns), 0, 0))
    post_shapes = (
        jax.ShapeDtypeStruct((T, GLA_KEY_WIDTH), BF16),
        jax.ShapeDtypeStruct((T, GLA_KEY_WIDTH), BF16),
        jax.ShapeDtypeStruct((T, GLA_KEY_WIDTH), BF16),
        jax.ShapeDtypeStruct((T, GLA_WIDTH), BF16),
        jax.ShapeDtypeStruct((T // GLA_CHUNK, GLA_KEY_WIDTH), F32),
        jax.ShapeDtypeStruct((T, GLA_WIDTH), BF16),
        jax.ShapeDtypeStruct((B, MLA_HEADS, ns // 2, 2, tm, QK_PAD), BF16),
        jax.ShapeDtypeStruct((B, MLA_HEADS, S, QK_PAD), BF16),
        jax.ShapeDtypeStruct((B, MLA_HEADS, S, MLA_DV), BF16),
        jax.ShapeDtypeStruct((B, ns // 2, 2, tm, MLA_WIDTH), BF16),
    )
    post_specs = (rows(GLA_KEY_WIDTH, lag), rows(GLA_KEY_WIDTH, lag), rows(GLA_KEY_WIDTH, lag),
                  rows(GLA_WIDTH, lag),
                  pl.BlockSpec((tm // GLA_CHUNK, GLA_KEY_WIDTH), lambda i: (lag(i), 0)),
                  rows(GLA_WIDTH, lag), q_spec, heads(QK_PAD), heads(MLA_DV), tiles(lag))
    assert len(post_specs) == N_POST_OUT

    front_specs = [rows(D_MODEL, cur)]
    front_args = [x2]
    if fuse_out:
        o_gla, o_mla, w_out = prev
        front_specs += [rows(GLA_WIDTH, cur), tiles(cur), _full((D_MODEL, D_MODEL))]
        front_args += [o_gla, o_mla, w_out]
    front_specs += [_full((1, D_MODEL)), _full((D_MODEL, D_IN_PAD))]
    front_args += [lw["norm_g"], lw["w_in"]]
    post_in_specs = [rows(LANES, lag), rows(LANES, lag),
                     _full((LANES, GLA_KEY_WIDTH)), _full((1, GLA_KEY_WIDTH)),
                     _full((1, MLA_Q_RANK)), _full((MLA_Q_RANK, MLA_HEADS * MLA_QK)),
                     _full((1, MLA_KV_RANK)), _full((MLA_KV_RANK, MLA_HEADS * (MLA_NOPE + MLA_DV))),
                     _full((1, LANES)), _full((1, LANES)), _full((1, LANES)), _full((1, LANES)),
                     _full((1, LANES))]
    post_in_args = [cos_t, sin_t, lw["w_misc"], lw["b_gate"], lw["q_norm_g"], lw["w_uq"],
                    lw["kv_norm_g"], lw["w_ukv"], lw["qg_nope"], lw["qg_rope"], lw["kg_nope"],
                    lw["kg_rope"], lw["k_bias"]]
    assert len(post_in_specs) == N_POST_IN

    out_shape = post_shapes
    out_specs = post_specs
    if fuse_out:
        out_shape = (jax.ShapeDtypeStruct((T, D_MODEL), F32),) + out_shape
        out_specs = (rows(D_MODEL, cur),) + out_specs
    return pl.pallas_call(
        functools.partial(_proj_kernel, fuse_out=fuse_out),
        grid=(n + 1,), in_specs=front_specs + post_in_specs, out_specs=out_specs, out_shape=out_shape,
        scratch_shapes=[pltpu.VMEM((tm, D_IN_PAD), F32), pltpu.VMEM((tm, D_IN_PAD), F32)],
        compiler_params=_params(1), name="outproj_proj" if fuse_out else "proj",
    )(*front_args, *post_in_args)


def _gla_call(qd, ki, ke, gv, dec, gg, gla_g, S):
    T = qd.shape[0]
    tm = GLA_TM
    rows = lambda w: pl.BlockSpec((tm, w), lambda i: (i, 0))
    return pl.pallas_call(
        functools.partial(_gla_kernel, steps_per_seq=S // tm),
        grid=(T // tm,),
        in_specs=[rows(GLA_KEY_WIDTH), rows(GLA_KEY_WIDTH), rows(GLA_KEY_WIDTH), rows(GLA_WIDTH),
                  pl.BlockSpec((tm // GLA_CHUNK, GLA_KEY_WIDTH), lambda i: (i, 0)),
                  rows(GLA_WIDTH), _full((1, GLA_DV))],
        out_specs=rows(GLA_WIDTH),
        out_shape=jax.ShapeDtypeStruct((T, GLA_WIDTH), BF16),
        scratch_shapes=[pltpu.VMEM((GLA_WIDTH, GLA_KEY_WIDTH), F32)],
        compiler_params=_params(1), name="gla",
    )(qd, ki, ke, gv, dec, gg, gla_g)


def _attn_call(q, k, v, mg, bounded):
    B, H, n_pairs, _, tq, _ = q.shape
    S = k.shape[2]
    out_shape = jax.ShapeDtypeStruct(mg.shape, BF16)
    kv_specs = [pl.BlockSpec((None, None, S, QK_PAD), lambda b, h, *_: (b, h, 0, 0)),
                pl.BlockSpec((None, None, S, MLA_DV), lambda b, h, *_: (b, h, 0, 0))]
    if bounded:
        q_spec = pl.BlockSpec((None, None, None, 2, tq, QK_PAD), lambda b, h, p: (b, h, p, 0, 0, 0))
        io_spec = pl.BlockSpec((None, None, 2, tq, MLA_DV), lambda b, h, p: (b, p, 0, 0, h))
        return pl.pallas_call(
            _attn_bounded_kernel, grid=(B, H, n_pairs),
            in_specs=[q_spec] + kv_specs + [io_spec], out_specs=io_spec, out_shape=out_shape,
            compiler_params=_params(3), name="attn_bounded",
        )(q, k, v, mg)
    q_spec = pl.BlockSpec((None, None, None, None, tq, QK_PAD), lambda b, h, p, s: (b, h, p, s, 0, 0))
    io_spec = pl.BlockSpec((None, None, None, tq, MLA_DV), lambda b, h, p, s: (b, p, s, 0, h))
    return pl.pallas_call(
        _attn_kernel, grid=(B, H, n_pairs, 2),
        in_specs=[q_spec] + kv_specs + [io_spec], out_specs=io_spec, out_shape=out_shape,
        scratch_shapes=[pltpu.VMEM((tq, 1), F32), pltpu.VMEM((tq, 1), F32), pltpu.VMEM((tq, MLA_DV), F32)],
        compiler_params=_params(4), name="attn_online",
    )(q, k, v, mg)


def _out_call(x2, og, om, w_out):
    T = x2.shape[0]
    tm = OUT_TM
    ns = om.shape[1] * 2
    rows = lambda w: pl.BlockSpec((tm, w), lambda i: (i, 0))
    pair, slot = _pair_slot(ns)
    om_spec = pl.BlockSpec((None, None, None, tm, MLA_WIDTH),
                           lambda i: (i // ns, pair(i % ns), slot(i % ns), 0, 0))
    return pl.pallas_call(
        _out_kernel, grid=(T // tm,),
        in_specs=[rows(D_MODEL), rows(GLA_WIDTH), om_spec, _full((D_MODEL, D_MODEL))],
        out_specs=rows(D_MODEL),
        out_shape=jax.ShapeDtypeStruct((T, D_MODEL), F32),
        compiler_params=_params(1), name="outproj",
    )(x2, og, om, w_out)


def _prep_weights(norm_g, w_in, w_gla_gate_up, b_gla_gate, gla_norm_g, mla_q_norm_g, w_uq,
                  mla_kv_norm_g, w_ukv, q_head_g, k_head_g, w_out):
    depth = w_in.shape[0]
    o = [0]
    for wdt in (GLA_KEY_WIDTH, GLA_KEY_WIDTH, GLA_WIDTH, GLA_GATE_RANK, GLA_WIDTH,
                MLA_Q_RANK, MLA_KV_RANK, MLA_ROPE, MLA_WIDTH):
        o.append(o[-1] + wdt)
    gq, gk, gv, glr, ggate, cq, ckv, kpe, mgate = [w_in[:, :, o[n]:o[n + 1]] for n in range(9)]
    pad = jnp.zeros((depth, D_MODEL, LANES - MLA_ROPE - GLA_GATE_RANK), w_in.dtype)
    w_in_p = jnp.concatenate([gq, gk, gv, ggate, cq, ckv, kpe, glr, pad, mgate], axis=-1).astype(BF16)

    w_misc = jnp.zeros((depth, LANES, GLA_KEY_WIDTH), F32)
    w_misc = w_misc.at[:, MLA_ROPE:MLA_ROPE + GLA_GATE_RANK, :].set(w_gla_gate_up).astype(BF16)

    wq = w_uq.reshape(depth, MLA_Q_RANK, MLA_HEADS, MLA_QK)
    w_uq_p = jnp.concatenate([wq[..., :MLA_NOPE].reshape(depth, MLA_Q_RANK, -1),
                              wq[..., MLA_NOPE:].reshape(depth, MLA_Q_RANK, -1)], axis=-1).astype(BF16)
    wkv = w_ukv.reshape(depth, MLA_KV_RANK, MLA_HEADS, MLA_NOPE + MLA_DV)
    w_ukv_p = jnp.concatenate([wkv[..., :MLA_NOPE].reshape(depth, MLA_KV_RANK, -1),
                               wkv[..., MLA_NOPE:].reshape(depth, MLA_KV_RANK, -1)], axis=-1).astype(BF16)

    zeros64 = jnp.zeros((depth, 1, LANES - MLA_ROPE), F32)
    row = lambda a: a[:, None, :]
    return dict(
        norm_g=row(norm_g), w_in=w_in_p, w_misc=w_misc, b_gate=row(b_gla_gate),
        gla_g=row(gla_norm_g), q_norm_g=row(mla_q_norm_g), w_uq=w_uq_p,
        kv_norm_g=row(mla_kv_norm_g), w_ukv=w_ukv_p,
        qg_nope=row(q_head_g[:, :MLA_NOPE]),
        qg_rope=jnp.concatenate([row(q_head_g[:, MLA_NOPE:])] * 2, axis=-1),
        kg_nope=row(k_head_g[:, :MLA_NOPE]),
        kg_rope=jnp.concatenate([row(k_head_g[:, MLA_NOPE:]), zeros64], axis=-1),
        w_out=w_out.astype(BF16),
    )


def _score_bounds(q_head_g, k_head_g):
    bound = (MLA_QK ** 0.5 * LOG2E * SCORE_BOUND_SLACK
             * jnp.max(jnp.abs(q_head_g), axis=-1) * jnp.max(jnp.abs(k_head_g), axis=-1))
    use = bound <= SCORE_BOUND_MAX_LOG2
    depth = q_head_g.shape[0]
    bias = jnp.zeros((depth, 1, LANES), F32).at[:, 0, MLA_ROPE].set(jnp.where(use, -bound, 0.0))
    return use, bias


def _rope_tables(positions):
    inv_freq = ROPE_THETA ** (-jnp.arange(0, MLA_ROPE, 2, dtype=F32) / MLA_ROPE)
    ang = positions.astype(F32)[..., None] * inv_freq
    cos = jnp.cos(ang).reshape(-1, MLA_ROPE // 2)
    sin = jnp.sin(ang).reshape(-1, MLA_ROPE // 2)
    return (jnp.concatenate([cos, cos, cos, cos], axis=-1),
            jnp.concatenate([-sin, sin, -sin, sin], axis=-1))


def kernel(x, positions, norm_g, w_in, w_gla_gate_up, b_gla_gate, gla_norm_g, mla_q_norm_g, w_uq,
           mla_kv_norm_g, w_ukv, q_head_g, k_head_g, w_out):
    B, S, D = x.shape
    assert D == D_MODEL and S % GLA_TM == 0 and S % (2 * ATTN_TQ) == 0
    assert PROJ_TM == ATTN_TQ == OUT_TM and ATTN_TQ % ATTN_TK == 0
    depth = w_in.shape[0]
    weights = _prep_weights(norm_g, w_in, w_gla_gate_up, b_gla_gate, gla_norm_g, mla_q_norm_g, w_uq,
                            mla_kv_norm_g, w_ukv, q_head_g, k_head_g, w_out)
    use_bounded, weights["k_bias"] = _score_bounds(q_head_g, k_head_g)
    cos_t, sin_t = _rope_tables(positions)
    x2 = x.reshape(B * S, D)
    prev = None
    for l in range(depth):
        lw = {name: w[l] for name, w in weights.items()}
        outs = _proj_call(x2, prev, cos_t, sin_t, lw, B, S)
        if prev is not None:
            x2, outs = outs[0], outs[1:]
        qd, ki, ke, gv, dec, gg, q, k, v, mg = outs
        o_gla = _gla_call(qd, ki, ke, gv, dec, gg, lw["gla_g"], S)
        o_mla = lax.cond(use_bounded[l],
                         functools.partial(_attn_call, bounded=True),
                         functools.partial(_attn_call, bounded=False), q, k, v, mg)
        prev = (o_gla, o_mla, lw["w_out"])
    x2 = _out_call(x2, *prev)
    return x2.reshape(B, S, D)
```

```python
import functools

import jax
import jax.numpy as jnp
from jax import lax
from jax.experimental import pallas as pl
from jax.experimental.pallas import tpu as pltpu

F32 = jnp.float32
BF16 = jnp.bfloat16

D_MODEL = 1024
GLA_HEADS = 4
GLA_DK = 64
GLA_DV = 128
GLA_KEY_WIDTH = GLA_HEADS * GLA_DK
GLA_WIDTH = GLA_HEADS * GLA_DV
GLA_GATE_RANK = 16
GLA_GATE_TEMP = 16.0
GLA_CHUNK = 64
MLA_HEADS = 4
MLA_NOPE = 128
MLA_ROPE = 64
MLA_QK = MLA_NOPE + MLA_ROPE
MLA_DV = 128
MLA_WIDTH = MLA_HEADS * MLA_DV
MLA_Q_RANK = 256
MLA_KV_RANK = 128
ROPE_THETA = 10000.0
EPS = 1e-6
LOG2E = 1.4426950408889634
SCORE_BOUND_MAX_LOG2 = 40.0 * LOG2E
SCORE_BOUND_SLACK = 1.02

LANES = 128
QK_PAD = 2 * LANES

OFF_GQ = 0
OFF_GK = 256
OFF_GV = 512
OFF_GGATE = 1024
OFF_CQ = 1536
OFF_CKV = 1792
OFF_MISC = 1920
OFF_MGATE = 2048
D_IN_PAD = 2560

PROJ_TM = 512
GLA_TM = 512
ATTN_TQ = 512
ATTN_TK = 512
OUT_TM = 512
CUMSUM_BLOCK = 256
PROJ_CHUNK = 256
VMEM_LIMIT = 56 * 1024 * 1024


def _rms(t, g):
    return t * lax.rsqrt(jnp.mean(t * t, axis=-1, keepdims=True) + EPS) * g


def _silu(t):
    return t * jax.nn.sigmoid(t)


def _split_dot(mat_bf16, t):
    hi = t.astype(BF16)
    lo = (t - hi.astype(F32)).astype(BF16)
    return (jnp.dot(mat_bf16, hi, preferred_element_type=F32)
            + jnp.dot(mat_bf16, lo, preferred_element_type=F32))


def _swap_halves(t):
    lane = lax.broadcasted_iota(jnp.int32, t.shape, 1)
    return jnp.where(lane % MLA_ROPE < MLA_ROPE // 2,
                     pltpu.roll(t, LANES - MLA_ROPE // 2, axis=1),
                     pltpu.roll(t, MLA_ROPE // 2, axis=1))


N_POST_IN = 13
N_POST_OUT = 10


def _proj_kernel(*refs, fuse_out):
    refs = list(refs)
    x_ref = refs.pop(0)
    if fuse_out:
        og_ref, om_ref, wout_ref = refs[:3]
        del refs[:3]
    ng_ref, win_ref = refs[:2]
    post_in = refs[2:2 + N_POST_IN]
    del refs[:2 + N_POST_IN]
    if fuse_out:
        xo_ref = refs.pop(0)
    post_out = refs[:N_POST_OUT]
    z0_ref, z1_ref = refs[N_POST_OUT:]
    i = pl.program_id(0)

    @pl.when(i == 0)
    def _():
        z1_ref[...] = jnp.zeros_like(z1_ref)

    def step(z_read, z_write):
        post = _proj_post(z_read, *post_in, *post_out)
        x = x_ref[...]
        if fuse_out:
            w = wout_ref[...]
            x = (x + jnp.dot(og_ref[...], w[:GLA_WIDTH], preferred_element_type=F32)
                 + jnp.dot(om_ref[...], w[GLA_WIDTH:], preferred_element_type=F32))
            xo_ref[...] = x
        h = _rms(x, ng_ref[...]).astype(BF16)
        n_chunks = D_IN_PAD // PROJ_CHUNK
        for c in range(n_chunks):
            if c in (0, n_chunks // 2):
                next(post)
            cols = slice(c * PROJ_CHUNK, (c + 1) * PROJ_CHUNK)
            z_write[:, cols] = jnp.dot(h, win_ref[:, cols], preferred_element_type=F32)
        for _ in post:
            pass

    @pl.when(i % 2 == 0)
    def _():
        step(z1_ref, z0_ref)

    @pl.when(i % 2 == 1)
    def _():
        step(z0_ref, z1_ref)


def _proj_post(z, cos_ref, sin_ref, wmisc_ref, bgate_ref,
               qng_ref, wuq_ref, kvng_ref, wukv_ref, qgn_ref, qgr_ref, kgn_ref, kgr_ref, kbias_ref,
               qd_ref, ki_ref, ke_ref, gv_ref, dec_ref, gg_ref, q_ref, k_ref, v_ref, mg_ref):
    tm = z.shape[0]

    misc = z[:, OFF_MISC:OFF_MISC + LANES]
    logit = jnp.dot(misc.astype(BF16), wmisc_ref[...], preferred_element_type=F32) + bgate_ref[...]
    cqn = _rms(z[:, OFF_CQ:OFF_CQ + MLA_Q_RANK], qng_ref[...]).astype(BF16)
    qraw = jnp.dot(cqn, wuq_ref[...], preferred_element_type=F32)
    ckvn = _rms(z[:, OFF_CKV:OFF_CKV + MLA_KV_RANK], kvng_ref[...]).astype(BF16)
    kvraw = jnp.dot(ckvn, wukv_ref[...], preferred_element_type=F32)
    yield

    log_a = (jnp.minimum(logit, 0.0) - jnp.log1p(jnp.exp(-jnp.abs(logit)))) * (1.0 / GLA_GATE_TEMP)

    cb = CUMSUM_BLOCK
    r = lax.broadcasted_iota(jnp.int32, (cb, cb), 0)
    c = lax.broadcasted_iota(jnp.int32, (cb, cb), 1)
    same = (r // GLA_CHUNK) == (c // GLA_CHUNK)
    lower = jnp.where(same & (c <= r), 1.0, 0.0).astype(BF16)
    upper = jnp.where(same & (c > r), 1.0, 0.0).astype(BF16)
    nck = cb // GLA_CHUNK
    sr = lax.broadcasted_iota(jnp.int32, (8, cb), 0)
    sc = lax.broadcasted_iota(jnp.int32, (8, cb), 1)
    chunk_sum = jnp.where((sc // GLA_CHUNK) == sr, 1.0, 0.0).astype(BF16)

    gq = z[:, OFF_GQ:OFF_GQ + GLA_KEY_WIDTH] * (GLA_DK ** -0.5)
    gk = z[:, OFF_GK:OFF_GK + GLA_KEY_WIDTH]
    for blk in range(tm // cb):
        rows = slice(blk * cb, (blk + 1) * cb)
        la = log_a[rows]
        b = _split_dot(lower, la)
        rest = _split_dot(upper, la)
        tot = _split_dot(chunk_sum, la)
        qd_ref[rows, :] = (gq[rows] * jnp.exp(b)).astype(BF16)
        ki_ref[rows, :] = (gk[rows] * jnp.exp(-b)).astype(BF16)
        ke_ref[rows, :] = (gk[rows] * jnp.exp(rest)).astype(BF16)
        dec_ref[blk * nck:(blk + 1) * nck, :] = jnp.exp(tot[:nck])
    yield
    gv_ref[...] = z[:, OFF_GV:OFF_GV + GLA_WIDTH].astype(BF16)
    gg_ref[...] = _silu(z[:, OFF_GGATE:OFF_GGATE + GLA_WIDTH]).astype(BF16)

    cos = cos_ref[...]
    sin = sin_ref[...]
    lane = lax.broadcasted_iota(jnp.int32, (tm, LANES), 1)
    low64 = lane < MLA_ROPE

    scale = MLA_QK ** -0.5 * LOG2E
    one_col = jnp.where(lane == MLA_ROPE, 1.0, 0.0)
    nope_w = MLA_HEADS * MLA_NOPE
    for pair in range(MLA_HEADS // 2):
        t = qraw[:, nope_w + pair * LANES:nope_w + (pair + 1) * LANES]
        t2 = t * t
        ssq_lo = jnp.sum(jnp.where(low64, t2, 0.0), axis=-1, keepdims=True)
        ssq_hi = jnp.sum(jnp.where(low64, 0.0, t2), axis=-1, keepdims=True)
        tg = t * qgr_ref[...]
        rot = tg * cos + _swap_halves(tg) * sin
        for sub in range(2):
            hd = 2 * pair + sub
            qn = qraw[:, hd * MLA_NOPE:(hd + 1) * MLA_NOPE]
            ssq = jnp.sum(qn * qn, axis=-1, keepdims=True) + (ssq_lo if sub == 0 else ssq_hi)
            rinv = lax.rsqrt(ssq * (1.0 / MLA_QK) + EPS) * scale
            q_ref[hd, :, 0:LANES] = (qn * rinv * qgn_ref[...]).astype(BF16)
            rsel = rot if sub == 0 else pltpu.roll(rot, MLA_ROPE, axis=1)
            q_ref[hd, :, LANES:QK_PAD] = jnp.where(low64, rsel * rinv, one_col).astype(BF16)

    kpe2 = misc * misc
    ssq_pe = jnp.sum(jnp.where(low64, kpe2, 0.0), axis=-1, keepdims=True)
    kg = misc * kgr_ref[...]
    krot = kg * cos + _swap_halves(kg) * sin
    for hd in range(MLA_HEADS):
        kn = kvraw[:, hd * MLA_NOPE:(hd + 1) * MLA_NOPE]
        ssq = jnp.sum(kn * kn, axis=-1, keepdims=True) + ssq_pe
        rinv = lax.rsqrt(ssq * (1.0 / MLA_QK) + EPS)
        k_ref[hd, :, 0:LANES] = (kn * rinv * kgn_ref[...]).astype(BF16)
        k_ref[hd, :, LANES:QK_PAD] = (krot * rinv + kbias_ref[...]).astype(BF16)
        v_ref[hd, :, :] = kvraw[:, nope_w + hd * MLA_DV:nope_w + (hd + 1) * MLA_DV].astype(BF16)
    mg_ref[...] = _silu(z[:, OFF_MGATE:OFF_MGATE + MLA_WIDTH]).astype(BF16)


def _gla_kernel(qd_ref, ki_ref, ke_ref, v_ref, dec_ref, gate_ref, g_ref, o_ref, state_ref, *, steps_per_seq):
    tm = qd_ref.shape[0]

    @pl.when(pl.program_id(0) % steps_per_seq == 0)
    def _():
        state_ref[...] = jnp.zeros_like(state_ref)

    C = GLA_CHUNK
    klane_head = lax.broadcasted_iota(jnp.int32, (C, GLA_KEY_WIDTH), 1) // GLA_DK
    vlane_head = lax.broadcasted_iota(jnp.int32, (C, GLA_WIDTH), 1) // GLA_DV
    row = lax.broadcasted_iota(jnp.int32, (C, GLA_KEY_WIDTH), 0)
    col_in_head = lax.broadcasted_iota(jnp.int32, (C, GLA_KEY_WIDTH), 1) % GLA_DK
    causal = row >= col_in_head
    pair_rows = 2 * GLA_DV
    own = ((lax.broadcasted_iota(jnp.int32, (pair_rows, LANES), 0) // GLA_DV)
           == (lax.broadcasted_iota(jnp.int32, (pair_rows, LANES), 1) // GLA_DK))
    nt = (((1,), (1,)), ((), ()))
    tn = (((0,), (0,)), ((), ()))
    g = g_ref[...]
    n_pairs = GLA_HEADS // 2

    for ck in range(tm // C):
        rows = slice(ck * C, (ck + 1) * C)
        qd = qd_ref[rows, :]
        ki = ki_ref[rows, :]
        ke = ke_ref[rows, :]
        v = v_ref[rows, :]
        upd = [lax.dot_general(v[:, p * pair_rows:(p + 1) * pair_rows], ke[:, p * LANES:(p + 1) * LANES],
                               tn, preferred_element_type=F32) for p in range(n_pairs)]
        zk = jnp.zeros_like(ki)
        zv = jnp.zeros_like(v)
        kstack = jnp.concatenate([jnp.where(klane_head == hd, ki, zk) for hd in range(GLA_HEADS)], axis=0)
        vstack = jnp.concatenate([jnp.where(vlane_head == hd, v, zv) for hd in range(GLA_HEADS)], axis=0)
        a = lax.dot_general(qd, kstack, nt, preferred_element_type=F32)
        a = jnp.where(causal, a, 0.0).astype(BF16)
        o = jnp.dot(a, vstack, preferred_element_type=F32)
        state = state_ref[...]
        o = o + jnp.concatenate(
            [lax.dot_general(qd[:, p * LANES:(p + 1) * LANES],
                             state[p * pair_rows:(p + 1) * pair_rows].astype(BF16),
                             nt, preferred_element_type=F32) for p in range(n_pairs)], axis=1)
        for p in range(n_pairs):
            srows = slice(p * pair_rows, (p + 1) * pair_rows)
            state_ref[srows, :] = (state[srows] * dec_ref[ck:ck + 1, p * LANES:(p + 1) * LANES]
                                   + jnp.where(own, upd[p], 0.0))
        for hd in range(GLA_HEADS):
            cols = slice(hd * GLA_DV, (hd + 1) * GLA_DV)
            oh = o[:, cols]
            on = oh * lax.rsqrt(jnp.mean(oh * oh, axis=-1, keepdims=True) + EPS) * g
            o_ref[rows, cols] = (on * gate_ref[rows, cols].astype(F32)).astype(BF16)


def _attn_kernel(q_ref, k_ref, v_ref, gate_ref, o_ref, m_ref, l_ref, acc_ref):
    tq = q_ref.shape[0]
    tk = ATTN_TK
    pair = pl.program_id(2)
    i = jnp.where(pl.program_id(3) == 0, pair, pl.num_programs(2) * 2 - 1 - pair)
    q = q_ref[...]
    nt = (((1,), (1,)), ((), ()))
    m_ref[...] = jnp.full_like(m_ref, -jnp.inf)
    l_ref[...] = jnp.zeros_like(l_ref)
    acc_ref[...] = jnp.zeros_like(acc_ref)

    def step(j, masked):
        start = pl.multiple_of(j * tk, tk)
        k = k_ref[pl.ds(start, tk), :]
        v = v_ref[pl.ds(start, tk), :]
        s = lax.dot_general(q, k, nt, preferred_element_type=F32)
        if masked:
            rr = lax.broadcasted_iota(jnp.int32, s.shape, 0)
            cc = lax.broadcasted_iota(jnp.int32, s.shape, 1)
            s = jnp.where(cc <= rr, s, -jnp.inf)
        m_prev = m_ref[...]
        m_new = jnp.maximum(m_prev, jnp.max(s, axis=-1, keepdims=True))
        alpha = jnp.exp2(m_prev - m_new)
        p = jnp.exp2(s - m_new)
        l_ref[...] = alpha * l_ref[...] + jnp.sum(p, axis=-1, keepdims=True)
        acc_ref[...] = alpha * acc_ref[...] + jnp.dot(p.astype(BF16), v, preferred_element_type=F32)
        m_ref[...] = m_new

    def body(j, carry):
        step(j, False)
        return carry

    lax.fori_loop(0, i * (tq // tk), body, 0)
    step(i * (tq // tk), True)
    o = acc_ref[...] / l_ref[...]
    o_ref[...] = (o * gate_ref[...].astype(F32)).astype(BF16)


def _attn_bounded_kernel(q_ref, k_ref, v_ref, gate_ref, o_ref):
    t = q_ref.shape[1]
    n_tiles = k_ref.shape[0] // t
    nt = (((1,), (1,)), ((), ()))
    ones_col = jnp.where(lax.broadcasted_iota(jnp.int32, (t, LANES), 1) == 0, 1.0, 0.0).astype(BF16)
    causal = (lax.broadcasted_iota(jnp.int32, (t, t), 1) <= lax.broadcasted_iota(jnp.int32, (t, t), 0))

    def key_tile(q, j, masked):
        k = k_ref[j * t:(j + 1) * t, :]
        v_aug = jnp.concatenate([v_ref[j * t:(j + 1) * t, :], ones_col], axis=1)
        p = jnp.exp2(lax.dot_general(q, k, nt, preferred_element_type=F32))
        if masked:
            p = jnp.where(causal, p, 0.0)
        return jnp.dot(p.astype(BF16), v_aug, preferred_element_type=F32)

    def query_tile(slot, idx):
        q = q_ref[slot]
        acc = key_tile(q, idx, True)
        for j in range(idx):
            acc = acc + key_tile(q, j, False)
        o = acc[:, :MLA_DV] * (1.0 / acc[:, MLA_DV:MLA_DV + 1])
        o_ref[slot] = (o * gate_ref[slot].astype(F32)).astype(BF16)

    for pair in range(n_tiles // 2):
        @pl.when(pl.program_id(2) == pair)
        def _(pair=pair):
            query_tile(0, pair)
            query_tile(1, n_tiles - 1 - pair)


def _out_kernel(x_ref, og_ref, om_ref, w_ref, o_ref):
    w = w_ref[...]
    o_ref[...] = (x_ref[...]
                  + jnp.dot(og_ref[...], w[:GLA_WIDTH], preferred_element_type=F32)
                  + jnp.dot(om_ref[...], w[GLA_WIDTH:], preferred_element_type=F32))


def _full(shape):
    return pl.BlockSpec(shape, lambda *idx: (0,) * len(shape), pipeline_mode=pl.Buffered(1))


def _pair_slot(n_tiles):
    half = n_tiles // 2
    pair = lambda t: jnp.where(t < half, t, n_tiles - 1 - t)
    slot = lambda t: jnp.where(t < half, 0, 1)
    return pair, slot


def _params(n_axes):
    return pltpu.CompilerParams(dimension_semantics=("arbitrary",) * n_axes,
                                vmem_limit_bytes=VMEM_LIMIT)


def _proj_call(x2, prev, cos_t, sin_t, lw, B, S):
    T = x2.shape[0]
    tm = PROJ_TM
    ns = S // tm
    n = T // tm
    fuse_out = prev is not None
    pair, slot = _pair_slot(ns)
    cur = lambda i: jnp.minimum(i, n - 1)
    lag = lambda i: jnp.maximum(i - 1, 0)
    rows = lambda w, at: pl.BlockSpec((tm, w), lambda i: (at(i), 0))
    heads = lambda w: pl.BlockSpec((None, MLA_HEADS, tm, w), lambda i: (lag(i) // ns, 0, lag(i) % ns, 0))
    tiles = lambda at: pl.BlockSpec((None, None, None, tm, MLA_WIDTH),
                                    lambda i: (at(i) // ns, pair(at(i) % ns), slot(at(i) % ns), 0, 0))
    q_spec = pl.BlockSpec((None, MLA_HEADS, None, None, tm, QK_PAD),
                          lambda i: (lag(i) // ns, 0, pair(lag(i) % ns), slot(lag(i) % ns), 0, 0))
    post_shapes = (
        jax.ShapeDtypeStruct((T, GLA_KEY_WIDTH), BF16),
        jax.ShapeDtypeStruct((T, GLA_KEY_WIDTH), BF16),
        jax.ShapeDtypeStruct((T, GLA_KEY_WIDTH), BF16),
        jax.ShapeDtypeStruct((T, GLA_WIDTH), BF16),
        jax.ShapeDtypeStruct((T // GLA_CHUNK, GLA_KEY_WIDTH), F32),
        jax.ShapeDtypeStruct((T, GLA_WIDTH), BF16),
        jax.ShapeDtypeStruct((B, MLA_HEADS, ns // 2, 2, tm, QK_PAD), BF16),
        jax.ShapeDtypeStruct((B, MLA_HEADS, S, QK_PAD), BF16),
        jax.ShapeDtypeStruct((B, MLA_HEADS, S, MLA_DV), BF16),
        jax.ShapeDtypeStruct((B, ns // 2, 2, tm, MLA_WIDTH), BF16),
    )
    post_specs = (rows(GLA_KEY_WIDTH, lag), rows(GLA_KEY_WIDTH, lag), rows(GLA_KEY_WIDTH, lag),
                  rows(GLA_WIDTH, lag),
                  pl.BlockSpec((tm // GLA_CHUNK, GLA_KEY_WIDTH), lambda i: (lag(i), 0)),
                  rows(GLA_WIDTH, lag), q_spec, heads(QK_PAD), heads(MLA_DV), tiles(lag))
    assert len(post_specs) == N_POST_OUT

    front_specs = [rows(D_MODEL, cur)]
    front_args = [x2]
    if fuse_out:
        o_gla, o_mla, w_out = prev
        front_specs += [rows(GLA_WIDTH, cur), tiles(cur), _full((D_MODEL, D_MODEL))]
        front_args += [o_gla, o_mla, w_out]
    front_specs += [_full((1, D_MODEL)), _full((D_MODEL, D_IN_PAD))]
    front_args += [lw["norm_g"], lw["w_in"]]
    post_in_specs = [rows(LANES, lag), rows(LANES, lag),
                     _full((LANES, GLA_KEY_WIDTH)), _full((1, GLA_KEY_WIDTH)),
                     _full((1, MLA_Q_RANK)), _full((MLA_Q_RANK, MLA_HEADS * MLA_QK)),
                     _full((1, MLA_KV_RANK)), _full((MLA_KV_RANK, MLA_HEADS * (MLA_NOPE + MLA_DV))),
                     _full((1, LANES)), _full((1, LANES)), _full((1, LANES)), _full((1, LANES)),
                     _full((1, LANES))]
    post_in_args = [cos_t, sin_t, lw["w_misc"], lw["b_gate"], lw["q_norm_g"], lw["w_uq"],
                    lw["kv_norm_g"], lw["w_ukv"], lw["qg_nope"], lw["qg_rope"], lw["kg_nope"],
                    lw["kg_rope"], lw["k_bias"]]
    assert len(post_in_specs) == N_POST_IN

    out_shape = post_shapes
    out_specs = post_specs
    if fuse_out:
        out_shape = (jax.ShapeDtypeStruct((T, D_MODEL), F32),) + out_shape
        out_specs = (rows(D_MODEL, cur),) + out_specs
    return pl.pallas_call(
        functools.partial(_proj_kernel, fuse_out=fuse_out),
        grid=(n + 1,), in_specs=front_specs + post_in_specs, out_specs=out_specs, out_shape=out_shape,
        scratch_shapes=[pltpu.VMEM((tm, D_IN_PAD), F32), pltpu.VMEM((tm, D_IN_PAD), F32)],
        compiler_params=_params(1), name="outproj_proj" if fuse_out else "proj",
    )(*front_args, *post_in_args)


def _gla_call(qd, ki, ke, gv, dec, gg, gla_g, S):
    T = qd.shape[0]
    tm = GLA_TM
    rows = lambda w: pl.BlockSpec((tm, w), lambda i: (i, 0))
    return pl.pallas_call(
        functools.partial(_gla_kernel, steps_per_seq=S // tm),
        grid=(T // tm,),
        in_specs=[rows(GLA_KEY_WIDTH), rows(GLA_KEY_WIDTH), rows(GLA_KEY_WIDTH), rows(GLA_WIDTH),
                  pl.BlockSpec((tm // GLA_CHUNK, GLA_KEY_WIDTH), lambda i: (i, 0)),
                  rows(GLA_WIDTH), _full((1, GLA_DV))],
        out_specs=rows(GLA_WIDTH),
        out_shape=jax.ShapeDtypeStruct((T, GLA_WIDTH), BF16),
        scratch_shapes=[pltpu.VMEM((GLA_WIDTH, LANES), F32)],
        compiler_params=_params(1), name="gla",
    )(qd, ki, ke, gv, dec, gg, gla_g)


def _attn_call(q, k, v, mg, bounded):
    B, H, n_pairs, _, tq, _ = q.shape
    S = k.shape[2]
    out_shape = jax.ShapeDtypeStruct(mg.shape, BF16)
    kv_specs = [pl.BlockSpec((None, None, S, QK_PAD), lambda b, h, *_: (b, h, 0, 0)),
                pl.BlockSpec((None, None, S, MLA_DV), lambda b, h, *_: (b, h, 0, 0))]
    if bounded:
        q_spec = pl.BlockSpec((None, None, None, 2, tq, QK_PAD), lambda b, h, p: (b, h, p, 0, 0, 0))
        io_spec = pl.BlockSpec((None, None, 2, tq, MLA_DV), lambda b, h, p: (b, p, 0, 0, h))
        return pl.pallas_call(
            _attn_bounded_kernel, grid=(B, H, n_pairs),
            in_specs=[q_spec] + kv_specs + [io_spec], out_specs=io_spec, out_shape=out_shape,
            compiler_params=_params(3), name="attn_bounded",
        )(q, k, v, mg)
    q_spec = pl.BlockSpec((None, None, None, None, tq, QK_PAD), lambda b, h, p, s: (b, h, p, s, 0, 0))
    io_spec = pl.BlockSpec((None, None, None, tq, MLA_DV), lambda b, h, p, s: (b, p, s, 0, h))
    return pl.pallas_call(
        _attn_kernel, grid=(B, H, n_pairs, 2),
        in_specs=[q_spec] + kv_specs + [io_spec], out_specs=io_spec, out_shape=out_shape,
        scratch_shapes=[pltpu.VMEM((tq, 1), F32), pltpu.VMEM((tq, 1), F32), pltpu.VMEM((tq, MLA_DV), F32)],
        compiler_params=_params(4), name="attn_online",
    )(q, k, v, mg)


def _out_call(x2, og, om, w_out):
    T = x2.shape[0]
    tm = OUT_TM
    ns = om.shape[1] * 2
    rows = lambda w: pl.BlockSpec((tm, w), lambda i: (i, 0))
    pair, slot = _pair_slot(ns)
    om_spec = pl.BlockSpec((None, None, None, tm, MLA_WIDTH),
                           lambda i: (i // ns, pair(i % ns), slot(i % ns), 0, 0))
    return pl.pallas_call(
        _out_kernel, grid=(T // tm,),
        in_specs=[rows(D_MODEL), rows(GLA_WIDTH), om_spec, _full((D_MODEL, D_MODEL))],
        out_specs=rows(D_MODEL),
        out_shape=jax.ShapeDtypeStruct((T, D_MODEL), F32),
        compiler_params=_params(1), name="outproj",
    )(x2, og, om, w_out)


def _prep_weights(norm_g, w_in, w_gla_gate_up, b_gla_gate, gla_norm_g, mla_q_norm_g, w_uq,
                  mla_kv_norm_g, w_ukv, q_head_g, k_head_g, w_out):
    depth = w_in.shape[0]
    o = [0]
    for wdt in (GLA_KEY_WIDTH, GLA_KEY_WIDTH, GLA_WIDTH, GLA_GATE_RANK, GLA_WIDTH,
                MLA_Q_RANK, MLA_KV_RANK, MLA_ROPE, MLA_WIDTH):
        o.append(o[-1] + wdt)
    w16 = w_in.astype(BF16)
    pad = jnp.zeros((depth, D_MODEL, LANES - MLA_ROPE - GLA_GATE_RANK), BF16)
    w_in_p = jnp.concatenate([w16[:, :, :o[3]], w16[:, :, o[4]:o[8]], w16[:, :, o[3]:o[4]], pad,
                              w16[:, :, o[8]:]], axis=-1)
    assert w_in_p.shape[-1] == D_IN_PAD and o[3] == OFF_GGATE and o[8] - o[4] + o[3] == OFF_MISC + MLA_ROPE

    w_misc = jnp.zeros((depth, LANES, GLA_KEY_WIDTH), F32)
    w_misc = w_misc.at[:, MLA_ROPE:MLA_ROPE + GLA_GATE_RANK, :].set(w_gla_gate_up).astype(BF16)

    wq = w_uq.reshape(depth, MLA_Q_RANK, MLA_HEADS, MLA_QK)
    w_uq_p = jnp.concatenate([wq[..., :MLA_NOPE].reshape(depth, MLA_Q_RANK, -1),
                              wq[..., MLA_NOPE:].reshape(depth, MLA_Q_RANK, -1)], axis=-1).astype(BF16)
    wkv = w_ukv.reshape(depth, MLA_KV_RANK, MLA_HEADS, MLA_NOPE + MLA_DV)
    w_ukv_p = jnp.concatenate([wkv[..., :MLA_NOPE].reshape(depth, MLA_KV_RANK, -1),
                               wkv[..., MLA_NOPE:].reshape(depth, MLA_KV_RANK, -1)], axis=-1).astype(BF16)

    zeros64 = jnp.zeros((depth, 1, LANES - MLA_ROPE), F32)
    row = lambda a: a[:, None, :]
    return dict(
        norm_g=row(norm_g), w_in=w_in_p, w_misc=w_misc, b_gate=row(b_gla_gate),
        gla_g=row(gla_norm_g), q_norm_g=row(mla_q_norm_g), w_uq=w_uq_p,
        kv_norm_g=row(mla_kv_norm_g), w_ukv=w_ukv_p,
        qg_nope=row(q_head_g[:, :MLA_NOPE]),
        qg_rope=jnp.concatenate([row(q_head_g[:, MLA_NOPE:])] * 2, axis=-1),
        kg_nope=row(k_head_g[:, :MLA_NOPE]),
        kg_rope=jnp.concatenate([row(k_head_g[:, MLA_NOPE:]), zeros64], axis=-1),
        w_out=w_out.astype(BF16),
    )


def _score_bounds(q_head_g, k_head_g):
    bound = (MLA_QK ** 0.5 * LOG2E * SCORE_BOUND_SLACK
             * jnp.max(jnp.abs(q_head_g), axis=-1) * jnp.max(jnp.abs(k_head_g), axis=-1))
    use = bound <= SCORE_BOUND_MAX_LOG2
    depth = q_head_g.shape[0]
    bias = jnp.zeros((depth, 1, LANES), F32).at[:, 0, MLA_ROPE].set(jnp.where(use, -bound, 0.0))
    return use, bias


def _rope_tables(positions):
    inv_freq = ROPE_THETA ** (-jnp.arange(0, MLA_ROPE, 2, dtype=F32) / MLA_ROPE)
    reps = LANES // inv_freq.shape[0]
    inv_freq = jnp.tile(inv_freq, reps)
    sign = jnp.tile(jnp.repeat(jnp.array([-1.0, 1.0], F32), MLA_ROPE // 2), LANES // MLA_ROPE)
    ang = positions.astype(F32).reshape(-1, 1) * inv_freq
    return jnp.cos(ang), jnp.sin(ang) * sign


def kernel(x, positions, norm_g, w_in, w_gla_gate_up, b_gla_gate, gla_norm_g, mla_q_norm_g, w_uq,
           mla_kv_norm_g, w_ukv, q_head_g, k_head_g, w_out):
    B, S, D = x.shape
    assert D == D_MODEL and S % GLA_TM == 0 and S % (2 * ATTN_TQ) == 0
    assert PROJ_TM == ATTN_TQ == OUT_TM and ATTN_TQ % ATTN_TK == 0
    depth = w_in.shape[0]
    weights = _prep_weights(norm_g, w_in, w_gla_gate_up, b_gla_gate, gla_norm_g, mla_q_norm_g, w_uq,
                            mla_kv_norm_g, w_ukv, q_head_g, k_head_g, w_out)
    use_bounded, weights["k_bias"] = _score_bounds(q_head_g, k_head_g)
    cos_t, sin_t = _rope_tables(positions)
    x2 = x.reshape(B * S, D)
    prev = None
    for l in range(depth):
        lw = {name: w[l] for name, w in weights.items()}
        outs = _proj_call(x2, prev, cos_t, sin_t, lw, B, S)
        if prev is not None:
            x2, outs = outs[0], outs[1:]
        qd, ki, ke, gv, dec, gg, q, k, v, mg = outs
        o_gla = _gla_call(qd, ki, ke, gv, dec, gg, lw["gla_g"], S)
        o_mla = lax.cond(use_bounded[l],
                         functools.partial(_attn_call, bounded=True),
                         functools.partial(_attn_call, bounded=False), q, k, v, mg)
        prev = (o_gla, o_mla, lw["w_out"])
    x2 = _out_call(x2, *prev)
    return x2.reshape(B, S, D)
```

```python
import functools

import jax
import jax.numpy as jnp
from jax import lax
from jax.experimental import pallas as pl
from jax.experimental.pallas import tpu as pltpu

F32 = jnp.float32
BF16 = jnp.bfloat16

D_MODEL = 1024
GLA_HEADS = 4
GLA_DK = 64
GLA_DV = 128
GLA_KEY_WIDTH = GLA_HEADS * GLA_DK
GLA_WIDTH = GLA_HEADS * GLA_DV
GLA_GATE_RANK = 16
GLA_GATE_TEMP = 16.0
GLA_CHUNK = 64
MLA_HEADS = 4
MLA_NOPE = 128
MLA_ROPE = 64
MLA_QK = MLA_NOPE + MLA_ROPE
MLA_DV = 128
MLA_WIDTH = MLA_HEADS * MLA_DV
MLA_Q_RANK = 256
MLA_KV_RANK = 128
ROPE_THETA = 10000.0
EPS = 1e-6
LOG2E = 1.4426950408889634
SCORE_BOUND_MAX_LOG2 = 40.0 * LOG2E
SCORE_BOUND_SLACK = 1.02

LANES = 128
QK_PAD = 2 * LANES

OFF_GQ = 0
OFF_GK = 256
OFF_GV = 512
OFF_GGATE = 1024
OFF_CQ = 1536
OFF_CKV = 1792
OFF_MISC = 1920
OFF_MGATE = 2048
D_IN_PAD = 2560

PROJ_TM = 512
GLA_TM = 512
ATTN_TQ = 512
ATTN_TK = 512
OUT_TM = 512
CUMSUM_BLOCK = 256
PROJ_CHUNK = 256
VMEM_LIMIT = 56 * 1024 * 1024


def _rms(t, g):
    return t * lax.rsqrt(jnp.mean(t * t, axis=-1, keepdims=True) + EPS) * g


def _silu(t):
    return t * jax.nn.sigmoid(t)


def _split_dot(mat_bf16, t):
    hi = t.astype(BF16)
    lo = (t - hi.astype(F32)).astype(BF16)
    return (jnp.dot(mat_bf16, hi, preferred_element_type=F32)
            + jnp.dot(mat_bf16, lo, preferred_element_type=F32))


def _swap_halves(t):
    lane = lax.broadcasted_iota(jnp.int32, t.shape, 1)
    return jnp.where(lane % MLA_ROPE < MLA_ROPE // 2,
                     pltpu.roll(t, LANES - MLA_ROPE // 2, axis=1),
                     pltpu.roll(t, MLA_ROPE // 2, axis=1))


N_POST_IN = 13
N_POST_OUT = 7


def _proj_kernel(*refs, fuse_out):
    refs = list(refs)
    x_ref = refs.pop(0)
    if fuse_out:
        og_ref, om_ref, wout_ref = refs[:3]
        del refs[:3]
    ng_ref, win_ref = refs[:2]
    post_in = refs[2:2 + N_POST_IN]
    del refs[:2 + N_POST_IN]
    if fuse_out:
        xo_ref = refs.pop(0)
    gv_ref, gg_ref, mg_ref = refs[:3]
    post_out = refs[3:3 + N_POST_OUT]
    z0_ref, z1_ref = refs[3 + N_POST_OUT:]
    i = pl.program_id(0)
    nt = (((1,), (1,)), ((), ()))
    direct = {}
    for ref, off, fn in ((gv_ref, OFF_GV, None), (gg_ref, OFF_GGATE, _silu), (mg_ref, OFF_MGATE, _silu)):
        for k in range(ref.shape[1] // PROJ_CHUNK):
            direct[off // PROJ_CHUNK + k] = (ref, k, fn)

    @pl.when(i == 0)
    def _():
        z1_ref[...] = jnp.zeros_like(z1_ref)

    def step(z_read, z_write):
        post = _proj_post(z_read, *post_in, *post_out)
        x = x_ref[...]
        if fuse_out:
            w = wout_ref[...]
            x = (x + jnp.dot(og_ref[...], w[:GLA_WIDTH], preferred_element_type=F32)
                 + jnp.dot(om_ref[...], w[GLA_WIDTH:], preferred_element_type=F32))
            xo_ref[...] = x
        h = _rms(x, ng_ref[...]).astype(BF16)
        n_chunks = D_IN_PAD // PROJ_CHUNK
        for c in range(n_chunks):
            if c in (0, n_chunks // 2):
                next(post)
            cols = slice(c * PROJ_CHUNK, (c + 1) * PROJ_CHUNK)
            zc = lax.dot_general(h, win_ref[cols, :], nt, preferred_element_type=F32)
            if c in direct:
                ref, k, fn = direct[c]
                ref[:, k * PROJ_CHUNK:(k + 1) * PROJ_CHUNK] = (zc if fn is None else fn(zc)).astype(BF16)
            else:
                z_write[:, cols] = zc
        for _ in post:
            pass

    @pl.when(i % 2 == 0)
    def _():
        step(z1_ref, z0_ref)

    @pl.when(i % 2 == 1)
    def _():
        step(z0_ref, z1_ref)


def _proj_post(z, cos_ref, sin_ref, wmisc_ref, bgate_ref,
               qng_ref, wuq_ref, kvng_ref, wukv_ref, qgn_ref, qgr_ref, kgn_ref, kgr_ref, kbias_ref,
               qd_ref, ki_ref, ke_ref, dec_ref, q_ref, k_ref, v_ref):
    tm = z.shape[0]

    misc = z[:, OFF_MISC:OFF_MISC + LANES]
    logit = jnp.dot(misc.astype(BF16), wmisc_ref[...], preferred_element_type=F32) + bgate_ref[...]
    cqn = _rms(z[:, OFF_CQ:OFF_CQ + MLA_Q_RANK], qng_ref[...]).astype(BF16)
    qraw = jnp.dot(cqn, wuq_ref[...], preferred_element_type=F32)
    ckvn = _rms(z[:, OFF_CKV:OFF_CKV + MLA_KV_RANK], kvng_ref[...]).astype(BF16)
    kvraw = jnp.dot(ckvn, wukv_ref[...], preferred_element_type=F32)
    yield

    log_a = (jnp.minimum(logit, 0.0) - jnp.log1p(jnp.exp(-jnp.abs(logit)))) * (1.0 / GLA_GATE_TEMP)

    cb = CUMSUM_BLOCK
    r = lax.broadcasted_iota(jnp.int32, (cb, cb), 0)
    c = lax.broadcasted_iota(jnp.int32, (cb, cb), 1)
    same = (r // GLA_CHUNK) == (c // GLA_CHUNK)
    lower = jnp.where(same & (c <= r), 1.0, 0.0).astype(BF16)
    upper = jnp.where(same & (c > r), 1.0, 0.0).astype(BF16)
    nck = cb // GLA_CHUNK
    sr = lax.broadcasted_iota(jnp.int32, (8, cb), 0)
    sc = lax.broadcasted_iota(jnp.int32, (8, cb), 1)
    chunk_sum = jnp.where((sc // GLA_CHUNK) == sr, 1.0, 0.0).astype(BF16)

    gq = z[:, OFF_GQ:OFF_GQ + GLA_KEY_WIDTH] * (GLA_DK ** -0.5)
    gk = z[:, OFF_GK:OFF_GK + GLA_KEY_WIDTH]
    for blk in range(tm // cb):
        rows = slice(blk * cb, (blk + 1) * cb)
        la = log_a[rows]
        b = _split_dot(lower, la)
        rest = _split_dot(upper, la)
        tot = _split_dot(chunk_sum, la)
        qd_ref[rows, :] = (gq[rows] * jnp.exp(b)).astype(BF16)
        ki_ref[rows, :] = (gk[rows] * jnp.exp(-b)).astype(BF16)
        ke_ref[rows, :] = (gk[rows] * jnp.exp(rest)).astype(BF16)
        dec_ref[blk * nck:(blk + 1) * nck, :] = jnp.exp(tot[:nck])
    yield

    cos = cos_ref[...]
    sin = sin_ref[...]
    lane = lax.broadcasted_iota(jnp.int32, (tm, LANES), 1)
    low64 = lane < MLA_ROPE

    scale = MLA_QK ** -0.5 * LOG2E
    one_col = jnp.where(lane == MLA_ROPE, 1.0, 0.0)
    nope_w = MLA_HEADS * MLA_NOPE
    for pair in range(MLA_HEADS // 2):
        t = qraw[:, nope_w + pair * LANES:nope_w + (pair + 1) * LANES]
        t2 = t * t
        ssq_lo = jnp.sum(jnp.where(low64, t2, 0.0), axis=-1, keepdims=True)
        ssq_hi = jnp.sum(jnp.where(low64, 0.0, t2), axis=-1, keepdims=True)
        tg = t * qgr_ref[...]
        rot = tg * cos + _swap_halves(tg) * sin
        for sub in range(2):
            hd = 2 * pair + sub
            qn = qraw[:, hd * MLA_NOPE:(hd + 1) * MLA_NOPE]
            ssq = jnp.sum(qn * qn, axis=-1, keepdims=True) + (ssq_lo if sub == 0 else ssq_hi)
            rinv = lax.rsqrt(ssq * (1.0 / MLA_QK) + EPS) * scale
            q_ref[hd, :, 0:LANES] = (qn * rinv * qgn_ref[...]).astype(BF16)
            rsel = rot if sub == 0 else pltpu.roll(rot, MLA_ROPE, axis=1)
            q_ref[hd, :, LANES:QK_PAD] = jnp.where(low64, rsel * rinv, one_col).astype(BF16)

    kpe2 = misc * misc
    ssq_pe = jnp.sum(jnp.where(low64, kpe2, 0.0), axis=-1, keepdims=True)
    kg = misc * kgr_ref[...]
    krot = kg * cos + _swap_halves(kg) * sin
    for hd in range(MLA_HEADS):
        kn = kvraw[:, hd * MLA_NOPE:(hd + 1) * MLA_NOPE]
        ssq = jnp.sum(kn * kn, axis=-1, keepdims=True) + ssq_pe
        rinv = lax.rsqrt(ssq * (1.0 / MLA_QK) + EPS)
        k_ref[hd, :, 0:LANES] = (kn * rinv * kgn_ref[...]).astype(BF16)
        k_ref[hd, :, LANES:QK_PAD] = (krot * rinv + kbias_ref[...]).astype(BF16)
        v_ref[hd, :, :] = kvraw[:, nope_w + hd * MLA_DV:nope_w + (hd + 1) * MLA_DV].astype(BF16)


def _gla_kernel(qd_ref, ki_ref, ke_ref, v_ref, dec_ref, gate_ref, g_ref, o_ref, state_ref, *, steps_per_seq):
    tm = qd_ref.shape[0]

    @pl.when(pl.program_id(0) % steps_per_seq == 0)
    def _():
        state_ref[...] = jnp.zeros_like(state_ref)

    C = GLA_CHUNK
    klane_head = lax.broadcasted_iota(jnp.int32, (C, GLA_KEY_WIDTH), 1) // GLA_DK
    vlane_head = lax.broadcasted_iota(jnp.int32, (C, GLA_WIDTH), 1) // GLA_DV
    row = lax.broadcasted_iota(jnp.int32, (C, GLA_KEY_WIDTH), 0)
    col_in_head = lax.broadcasted_iota(jnp.int32, (C, GLA_KEY_WIDTH), 1) % GLA_DK
    causal = row >= col_in_head
    pair_rows = 2 * GLA_DV
    own = ((lax.broadcasted_iota(jnp.int32, (pair_rows, LANES), 0) // GLA_DV)
           == (lax.broadcasted_iota(jnp.int32, (pair_rows, LANES), 1) // GLA_DK))
    nt = (((1,), (1,)), ((), ()))
    tn = (((0,), (0,)), ((), ()))
    g = g_ref[...]
    n_pairs = GLA_HEADS // 2

    for ck in range(tm // C):
        rows = slice(ck * C, (ck + 1) * C)
        qd = qd_ref[rows, :]
        ki = ki_ref[rows, :]
        ke = ke_ref[rows, :]
        v = v_ref[rows, :]
        upd = [lax.dot_general(v[:, p * pair_rows:(p + 1) * pair_rows], ke[:, p * LANES:(p + 1) * LANES],
                               tn, preferred_element_type=F32) for p in range(n_pairs)]
        zk = jnp.zeros_like(ki)
        zv = jnp.zeros_like(v)
        kstack = jnp.concatenate([jnp.where(klane_head == hd, ki, zk) for hd in range(GLA_HEADS)], axis=0)
        vstack = jnp.concatenate([jnp.where(vlane_head == hd, v, zv) for hd in range(GLA_HEADS)], axis=0)
        a = lax.dot_general(qd, kstack, nt, preferred_element_type=F32)
        a = jnp.where(causal, a, 0.0).astype(BF16)
        o = jnp.dot(a, vstack, preferred_element_type=F32)
        state = state_ref[...]
        o = o + jnp.concatenate(
            [lax.dot_general(qd[:, p * LANES:(p + 1) * LANES],
                             state[p * pair_rows:(p + 1) * pair_rows].astype(BF16),
                             nt, preferred_element_type=F32) for p in range(n_pairs)], axis=1)
        for p in range(n_pairs):
            srows = slice(p * pair_rows, (p + 1) * pair_rows)
            state_ref[srows, :] = (state[srows] * dec_ref[ck:ck + 1, p * LANES:(p + 1) * LANES]
                                   + jnp.where(own, upd[p], 0.0))
        for hd in range(GLA_HEADS):
            cols = slice(hd * GLA_DV, (hd + 1) * GLA_DV)
            oh = o[:, cols]
            on = oh * lax.rsqrt(jnp.mean(oh * oh, axis=-1, keepdims=True) + EPS) * g
            o_ref[rows, cols] = (on * gate_ref[rows, cols].astype(F32)).astype(BF16)


def _attn_kernel(q_ref, k_ref, v_ref, gate_ref, o_ref, m_ref, l_ref, acc_ref):
    tq = q_ref.shape[0]
    tk = ATTN_TK
    pair = pl.program_id(2)
    i = jnp.where(pl.program_id(3) == 0, pair, pl.num_programs(2) * 2 - 1 - pair)
    q = q_ref[...]
    nt = (((1,), (1,)), ((), ()))
    m_ref[...] = jnp.full_like(m_ref, -jnp.inf)
    l_ref[...] = jnp.zeros_like(l_ref)
    acc_ref[...] = jnp.zeros_like(acc_ref)

    def step(j, masked):
        start = pl.multiple_of(j * tk, tk)
        k = k_ref[pl.ds(start, tk), :]
        v = v_ref[pl.ds(start, tk), :]
        s = lax.dot_general(q, k, nt, preferred_element_type=F32)
        if masked:
            rr = lax.broadcasted_iota(jnp.int32, s.shape, 0)
            cc = lax.broadcasted_iota(jnp.int32, s.shape, 1)
            s = jnp.where(cc <= rr, s, -jnp.inf)
        m_prev = m_ref[...]
        m_new = jnp.maximum(m_prev, jnp.max(s, axis=-1, keepdims=True))
        alpha = jnp.exp2(m_prev - m_new)
        p = jnp.exp2(s - m_new)
        l_ref[...] = alpha * l_ref[...] + jnp.sum(p, axis=-1, keepdims=True)
        acc_ref[...] = alpha * acc_ref[...] + jnp.dot(p.astype(BF16), v, preferred_element_type=F32)
        m_ref[...] = m_new

    def body(j, carry):
        step(j, False)
        return carry

    lax.fori_loop(0, i * (tq // tk), body, 0)
    step(i * (tq // tk), True)
    o = acc_ref[...] / l_ref[...]
    o_ref[...] = (o * gate_ref[...].astype(F32)).astype(BF16)


def _attn_bounded_kernel(q_ref, k_ref, v_ref, gate_ref, o_ref):
    t = q_ref.shape[1]
    n_tiles = k_ref.shape[0] // t
    nt = (((1,), (1,)), ((), ()))
    ones_col = jnp.where(lax.broadcasted_iota(jnp.int32, (t, LANES), 1) == 0, 1.0, 0.0).astype(BF16)
    causal = (lax.broadcasted_iota(jnp.int32, (t, t), 1) <= lax.broadcasted_iota(jnp.int32, (t, t), 0))

    def key_tile(q, j, masked):
        k = k_ref[j * t:(j + 1) * t, :]
        v_aug = jnp.concatenate([v_ref[j * t:(j + 1) * t, :], ones_col], axis=1)
        p = jnp.exp2(lax.dot_general(q, k, nt, preferred_element_type=F32))
        if masked:
            p = jnp.where(causal, p, 0.0)
        return jnp.dot(p.astype(BF16), v_aug, preferred_element_type=F32)

    def query_tile(slot, idx):
        q = q_ref[slot]
        acc = key_tile(q, idx, True)
        for j in range(idx):
            acc = acc + key_tile(q, j, False)
        o = acc[:, :MLA_DV] * (1.0 / acc[:, MLA_DV:MLA_DV + 1])
        o_ref[slot] = (o * gate_ref[slot].astype(F32)).astype(BF16)

    for pair in range(n_tiles // 2):
        @pl.when(pl.program_id(2) == pair)
        def _(pair=pair):
            query_tile(0, pair)
            query_tile(1, n_tiles - 1 - pair)


def _out_kernel(x_ref, og_ref, om_ref, w_ref, o_ref):
    w = w_ref[...]
    o_ref[...] = (x_ref[...]
                  + jnp.dot(og_ref[...], w[:GLA_WIDTH], preferred_element_type=F32)
                  + jnp.dot(om_ref[...], w[GLA_WIDTH:], preferred_element_type=F32))


def _full(shape):
    return pl.BlockSpec(shape, lambda *idx: (0,) * len(shape), pipeline_mode=pl.Buffered(1))


def _pair_slot(n_tiles):
    half = n_tiles // 2
    pair = lambda t: jnp.where(t < half, t, n_tiles - 1 - t)
    slot = lambda t: jnp.where(t < half, 0, 1)
    return pair, slot


def _params(n_axes):
    return pltpu.CompilerParams(dimension_semantics=("arbitrary",) * n_axes,
                                vmem_limit_bytes=VMEM_LIMIT)


def _proj_call(x2, prev, cos_t, sin_t, lw, B, S):
    T = x2.shape[0]
    tm = PROJ_TM
    ns = S // tm
    n = T // tm
    fuse_out = prev is not None
    pair, slot = _pair_slot(ns)
    cur = lambda i: jnp.minimum(i, n - 1)
    lag = lambda i: jnp.maximum(i - 1, 0)
    rows = lambda w, at: pl.BlockSpec((tm, w), lambda i: (at(i), 0))
    heads = lambda w: pl.BlockSpec((None, MLA_HEADS, tm, w), lambda i: (lag(i) // ns, 0, lag(i) % ns, 0))
    tiles = lambda at: pl.BlockSpec((None, None, None, tm, MLA_WIDTH),
                                    lambda i: (at(i) // ns, pair(at(i) % ns), slot(at(i) % ns), 0, 0))
    q_spec = pl.BlockSpec((None, MLA_HEADS, None, None, tm, QK_PAD),
                          lambda i: (lag(i) // ns, 0, pair(lag(i) % ns), slot(lag(i) % ns), 0, 0))
    front_shapes = (
        jax.ShapeDtypeStruct((T, GLA_WIDTH), BF16),
        jax.ShapeDtypeStruct((T, GLA_WIDTH), BF16),
        jax.ShapeDtypeStruct((B, ns // 2, 2, tm, MLA_WIDTH), BF16),
    )
    front_out_specs = (rows(GLA_WIDTH, cur), rows(GLA_WIDTH, cur), tiles(cur))
    post_shapes = (
        jax.ShapeDtypeStruct((T, GLA_KEY_WIDTH), BF16),
        jax.ShapeDtypeStruct((T, GLA_KEY_WIDTH), BF16),
        jax.ShapeDtypeStruct((T, GLA_KEY_WIDTH), BF16),
        jax.ShapeDtypeStruct((T // GLA_CHUNK, GLA_KEY_WIDTH), F32),
        jax.ShapeDtypeStruct((B, MLA_HEADS, ns // 2, 2, tm, QK_PAD), BF16),
        jax.ShapeDtypeStruct((B, MLA_HEADS, S, QK_PAD), BF16),
        jax.ShapeDtypeStruct((B, MLA_HEADS, S, MLA_DV), BF16),
    )
    post_specs = (rows(GLA_KEY_WIDTH, lag), rows(GLA_KEY_WIDTH, lag), rows(GLA_KEY_WIDTH, lag),
                  pl.BlockSpec((tm // GLA_CHUNK, GLA_KEY_WIDTH), lambda i: (lag(i), 0)),
                  q_spec, heads(QK_PAD), heads(MLA_DV))
    assert len(post_specs) == N_POST_OUT

    front_specs = [rows(D_MODEL, cur)]
    front_args = [x2]
    if fuse_out:
        o_gla, o_mla, w_out = prev
        front_specs += [rows(GLA_WIDTH, cur), tiles(cur), _full((D_MODEL, D_MODEL))]
        front_args += [o_gla, o_mla, w_out]
    front_specs += [_full((1, D_MODEL)), _full((D_IN_PAD, D_MODEL))]
    front_args += [lw["norm_g"], lw["w_in"]]
    post_in_specs = [rows(LANES, lag), rows(LANES, lag),
                     _full((LANES, GLA_KEY_WIDTH)), _full((1, GLA_KEY_WIDTH)),
                     _full((1, MLA_Q_RANK)), _full((MLA_Q_RANK, MLA_HEADS * MLA_QK)),
                     _full((1, MLA_KV_RANK)), _full((MLA_KV_RANK, MLA_HEADS * (MLA_NOPE + MLA_DV))),
                     _full((1, LANES)), _full((1, LANES)), _full((1, LANES)), _full((1, LANES)),
                     _full((1, LANES))]
    post_in_args = [cos_t, sin_t, lw["w_misc"], lw["b_gate"], lw["q_norm_g"], lw["w_uq"],
                    lw["kv_norm_g"], lw["w_ukv"], lw["qg_nope"], lw["qg_rope"], lw["kg_nope"],
                    lw["kg_rope"], lw["k_bias"]]
    assert len(post_in_specs) == N_POST_IN

    out_shape = front_shapes + post_shapes
    out_specs = front_out_specs + post_specs
    if fuse_out:
        out_shape = (jax.ShapeDtypeStruct((T, D_MODEL), F32),) + out_shape
        out_specs = (rows(D_MODEL, cur),) + out_specs
    return pl.pallas_call(
        functools.partial(_proj_kernel, fuse_out=fuse_out),
        grid=(n + 1,), in_specs=front_specs + post_in_specs, out_specs=out_specs, out_shape=out_shape,
        scratch_shapes=[pltpu.VMEM((tm, D_IN_PAD), F32), pltpu.VMEM((tm, D_IN_PAD), F32)],
        compiler_params=_params(1), name="outproj_proj" if fuse_out else "proj",
    )(*front_args, *post_in_args)


def _gla_call(qd, ki, ke, gv, dec, gg, gla_g, S):
    T = qd.shape[0]
    tm = GLA_TM
    rows = lambda w: pl.BlockSpec((tm, w), lambda i: (i, 0))
    return pl.pallas_call(
        functools.partial(_gla_kernel, steps_per_seq=S // tm),
        grid=(T // tm,),
        in_specs=[rows(GLA_KEY_WIDTH), rows(GLA_KEY_WIDTH), rows(GLA_KEY_WIDTH), rows(GLA_WIDTH),
                  pl.BlockSpec((tm // GLA_CHUNK, GLA_KEY_WIDTH), lambda i: (i, 0)),
                  rows(GLA_WIDTH), _full((1, GLA_DV))],
        out_specs=rows(GLA_WIDTH),
        out_shape=jax.ShapeDtypeStruct((T, GLA_WIDTH), BF16),
        scratch_shapes=[pltpu.VMEM((GLA_WIDTH, LANES), F32)],
        compiler_params=_params(1), name="gla",
    )(qd, ki, ke, gv, dec, gg, gla_g)


def _attn_call(q, k, v, mg, bounded):
    B, H, n_pairs, _, tq, _ = q.shape
    S = k.shape[2]
    out_shape = jax.ShapeDtypeStruct(mg.shape, BF16)
    kv_specs = [pl.BlockSpec((None, None, S, QK_PAD), lambda b, h, *_: (b, h, 0, 0)),
                pl.BlockSpec((None, None, S, MLA_DV), lambda b, h, *_: (b, h, 0, 0))]
    if bounded:
        q_spec = pl.BlockSpec((None, None, None, 2, tq, QK_PAD), lambda b, h, p: (b, h, p, 0, 0, 0))
        io_spec = pl.BlockSpec((None, None, 2, tq, MLA_DV), lambda b, h, p: (b, p, 0, 0, h))
        return pl.pallas_call(
            _attn_bounded_kernel, grid=(B, H, n_pairs),
            in_specs=[q_spec] + kv_specs + [io_spec], out_specs=io_spec, out_shape=out_shape,
            compiler_params=_params(3), name="attn_bounded",
        )(q, k, v, mg)
    q_spec = pl.BlockSpec((None, None, None, None, tq, QK_PAD), lambda b, h, p, s: (b, h, p, s, 0, 0))
    io_spec = pl.BlockSpec((None, None, None, tq, MLA_DV), lambda b, h, p, s: (b, p, s, 0, h))
    return pl.pallas_call(
        _attn_kernel, grid=(B, H, n_pairs, 2),
        in_specs=[q_spec] + kv_specs + [io_spec], out_specs=io_spec, out_shape=out_shape,
        scratch_shapes=[pltpu.VMEM((tq, 1), F32), pltpu.VMEM((tq, 1), F32), pltpu.VMEM((tq, MLA_DV), F32)],
        compiler_params=_params(4), name="attn_online",
    )(q, k, v, mg)


def _out_call(x2, og, om, w_out):
    T = x2.shape[0]
    tm = OUT_TM
    ns = om.shape[1] * 2
    rows = lambda w: pl.BlockSpec((tm, w), lambda i: (i, 0))
    pair, slot = _pair_slot(ns)
    om_spec = pl.BlockSpec((None, None, None, tm, MLA_WIDTH),
                           lambda i: (i // ns, pair(i % ns), slot(i % ns), 0, 0))
    return pl.pallas_call(
        _out_kernel, grid=(T // tm,),
        in_specs=[rows(D_MODEL), rows(GLA_WIDTH), om_spec, _full((D_MODEL, D_MODEL))],
        out_specs=rows(D_MODEL),
        out_shape=jax.ShapeDtypeStruct((T, D_MODEL), F32),
        compiler_params=_params(1), name="outproj",
    )(x2, og, om, w_out)


def _prep_weights(norm_g, w_in, w_gla_gate_up, b_gla_gate, gla_norm_g, mla_q_norm_g, w_uq,
                  mla_kv_norm_g, w_ukv, q_head_g, k_head_g, w_out):
    depth = w_in.shape[0]
    o = [0]
    for wdt in (GLA_KEY_WIDTH, GLA_KEY_WIDTH, GLA_WIDTH, GLA_GATE_RANK, GLA_WIDTH,
                MLA_Q_RANK, MLA_KV_RANK, MLA_ROPE, MLA_WIDTH):
        o.append(o[-1] + wdt)
    wt = jnp.swapaxes(w_in, 1, 2).astype(BF16)
    pad = jnp.zeros((depth, LANES - MLA_ROPE - GLA_GATE_RANK, D_MODEL), BF16)
    w_in_p = jnp.concatenate([wt[:, :o[3]], wt[:, o[4]:o[8]], wt[:, o[3]:o[4]], pad, wt[:, o[8]:]], axis=1)
    assert w_in_p.shape[1] == D_IN_PAD and o[3] == OFF_GGATE and o[8] - o[4] + o[3] == OFF_MISC + MLA_ROPE

    w_misc = jnp.zeros((depth, LANES, GLA_KEY_WIDTH), F32)
    w_misc = w_misc.at[:, MLA_ROPE:MLA_ROPE + GLA_GATE_RANK, :].set(w_gla_gate_up).astype(BF16)

    wq = w_uq.reshape(depth, MLA_Q_RANK, MLA_HEADS, MLA_QK)
    w_uq_p = jnp.concatenate([wq[..., :MLA_NOPE].reshape(depth, MLA_Q_RANK, -1),
                              wq[..., MLA_NOPE:].reshape(depth, MLA_Q_RANK, -1)], axis=-1).astype(BF16)
    wkv = w_ukv.reshape(depth, MLA_KV_RANK, MLA_HEADS, MLA_NOPE + MLA_DV)
    w_ukv_p = jnp.concatenate([wkv[..., :MLA_NOPE].reshape(depth, MLA_KV_RANK, -1),
                               wkv[..., MLA_NOPE:].reshape(depth, MLA_KV_RANK, -1)], axis=-1).astype(BF16)

    zeros64 = jnp.zeros((depth, 1, LANES - MLA_ROPE), F32)
    row = lambda a: a[:, None, :]
    return dict(
        norm_g=row(norm_g), w_in=w_in_p, w_misc=w_misc, b_gate=row(b_gla_gate),
        gla_g=row(gla_norm_g), q_norm_g=row(mla_q_norm_g), w_uq=w_uq_p,
        kv_norm_g=row(mla_kv_norm_g), w_ukv=w_ukv_p,
        qg_nope=row(q_head_g[:, :MLA_NOPE]),
        qg_rope=jnp.concatenate([row(q_head_g[:, MLA_NOPE:])] * 2, axis=-1),
        kg_nope=row(k_head_g[:, :MLA_NOPE]),
        kg_rope=jnp.concatenate([row(k_head_g[:, MLA_NOPE:]), zeros64], axis=-1),
        w_out=w_out.astype(BF16),
    )


def _score_bounds(q_head_g, k_head_g):
    bound = (MLA_QK ** 0.5 * LOG2E * SCORE_BOUND_SLACK
             * jnp.max(jnp.abs(q_head_g), axis=-1) * jnp.max(jnp.abs(k_head_g), axis=-1))
    use = bound <= SCORE_BOUND_MAX_LOG2
    depth = q_head_g.shape[0]
    bias = jnp.zeros((depth, 1, LANES), F32).at[:, 0, MLA_ROPE].set(jnp.where(use, -bound, 0.0))
    return use, bias


def _rope_tables(positions):
    inv_freq = ROPE_THETA ** (-jnp.arange(0, MLA_ROPE, 2, dtype=F32) / MLA_ROPE)
    reps = LANES // inv_freq.shape[0]
    inv_freq = jnp.tile(inv_freq, reps)
    sign = jnp.tile(jnp.repeat(jnp.array([-1.0, 1.0], F32), MLA_ROPE // 2), LANES // MLA_ROPE)
    ang = positions.astype(F32).reshape(-1, 1) * inv_freq
    return jnp.cos(ang), jnp.sin(ang) * sign


def kernel(x, positions, norm_g, w_in, w_gla_gate_up, b_gla_gate, gla_norm_g, mla_q_norm_g, w_uq,
           mla_kv_norm_g, w_ukv, q_head_g, k_head_g, w_out):
    B, S, D = x.shape
    assert D == D_MODEL and S % GLA_TM == 0 and S % (2 * ATTN_TQ) == 0
    assert PROJ_TM == ATTN_TQ == OUT_TM and ATTN_TQ % ATTN_TK == 0
    depth = w_in.shape[0]
    weights = _prep_weights(norm_g, w_in, w_gla_gate_up, b_gla_gate, gla_norm_g, mla_q_norm_g, w_uq,
                            mla_kv_norm_g, w_ukv, q_head_g, k_head_g, w_out)
    use_bounded, weights["k_bias"] = _score_bounds(q_head_g, k_head_g)
    cos_t, sin_t = _rope_tables(positions)
    x2 = x.reshape(B * S, D)
    prev = None
    for l in range(depth):
        lw = {name: w[l] for name, w in weights.items()}
        outs = _proj_call(x2, prev, cos_t, sin_t, lw, B, S)
        if prev is not None:
            x2, outs = outs[0], outs[1:]
        gv, gg, mg, qd, ki, ke, dec, q, k, v = outs
        o_gla = _gla_call(qd, ki, ke, gv, dec, gg, lw["gla_g"], S)
        o_mla = lax.cond(use_bounded[l],
                         functools.partial(_attn_call, bounded=True),
                         functools.partial(_attn_call, bounded=False), q, k, v, mg)
        prev = (o_gla, o_mla, lw["w_out"])
    x2 = _out_call(x2, *prev)
    return x2.reshape(B, S, D)
```

```python
import functools

import jax
import jax.numpy as jnp
from jax import lax
from jax.experimental import pallas as pl
from jax.experimental.pallas import tpu as pltpu

F32 = jnp.float32
BF16 = jnp.bfloat16

D_MODEL = 1024
GLA_HEADS = 4
GLA_DK = 64
GLA_DV = 128
GLA_KEY_WIDTH = GLA_HEADS * GLA_DK
GLA_WIDTH = GLA_HEADS * GLA_DV
GLA_GATE_RANK = 16
GLA_GATE_TEMP = 16.0
GLA_CHUNK = 64
MLA_HEADS = 4
MLA_NOPE = 128
MLA_ROPE = 64
MLA_QK = MLA_NOPE + MLA_ROPE
MLA_DV = 128
MLA_WIDTH = MLA_HEADS * MLA_DV
MLA_Q_RANK = 256
MLA_KV_RANK = 128
ROPE_THETA = 10000.0
EPS = 1e-6
LOG2E = 1.4426950408889634
SCORE_BOUND_MAX_LOG2 = 40.0 * LOG2E
SCORE_BOUND_SLACK = 1.02

LANES = 128
QK_PAD = 2 * LANES

OFF_GQ = 0
OFF_GK = 256
OFF_GV = 512
OFF_GGATE = 1024
OFF_CQ = 1536
OFF_CKV = 1792
OFF_MISC = 1920
OFF_MGATE = 2048
D_IN_PAD = 2560

PROJ_TM = 512
GLA_TM = 512
ATTN_TQ = 512
ATTN_TK = 512
OUT_TM = 512
CUMSUM_BLOCK = 256
PROJ_CHUNK = 256
VMEM_LIMIT = 56 * 1024 * 1024


def _rms(t, g):
    return t * lax.rsqrt(jnp.mean(t * t, axis=-1, keepdims=True) + EPS) * g


def _silu(t):
    return t * jax.nn.sigmoid(t)


def _split_dot(mat_bf16, t):
    hi = t.astype(BF16)
    lo = (t - hi.astype(F32)).astype(BF16)
    return (jnp.dot(mat_bf16, hi, preferred_element_type=F32)
            + jnp.dot(mat_bf16, lo, preferred_element_type=F32))


def _swap_halves(t):
    lane = lax.broadcasted_iota(jnp.int32, t.shape, 1)
    return jnp.where(lane % MLA_ROPE < MLA_ROPE // 2,
                     pltpu.roll(t, LANES - MLA_ROPE // 2, axis=1),
                     pltpu.roll(t, MLA_ROPE // 2, axis=1))


N_POST_IN = 13
N_POST_OUT = 7


def _proj_kernel(*refs, fuse_out):
    refs = list(refs)
    x_ref = refs.pop(0)
    if fuse_out:
        og_ref, om_ref, wout_ref = refs[:3]
        del refs[:3]
    ng_ref, win_ref = refs[:2]
    post_in = refs[2:2 + N_POST_IN]
    del refs[:2 + N_POST_IN]
    if fuse_out:
        xo_ref = refs.pop(0)
    gv_ref, gg_ref, mg_ref = refs[:3]
    post_out = refs[3:3 + N_POST_OUT]
    z0_ref, z1_ref = refs[3 + N_POST_OUT:]
    i = pl.program_id(0)
    nt = (((1,), (1,)), ((), ()))
    direct = {}
    for ref, off, fn in ((gv_ref, OFF_GV, None), (gg_ref, OFF_GGATE, _silu), (mg_ref, OFF_MGATE, _silu)):
        for k in range(ref.shape[1] // PROJ_CHUNK):
            direct[off // PROJ_CHUNK + k] = (ref, k, fn)

    @pl.when(i == 0)
    def _():
        z1_ref[...] = jnp.zeros_like(z1_ref)

    def step(z_read, z_write):
        post = _proj_post(z_read, *post_in, *post_out)
        x = x_ref[...]
        if fuse_out:
            w = wout_ref[...]
            x = (x + jnp.dot(og_ref[...], w[:GLA_WIDTH], preferred_element_type=F32)
                 + jnp.dot(om_ref[...], w[GLA_WIDTH:], preferred_element_type=F32))
            xo_ref[...] = x
        h = _rms(x, ng_ref[...]).astype(BF16)
        n_chunks = D_IN_PAD // PROJ_CHUNK
        for c in range(n_chunks):
            if c in (0, n_chunks // 2):
                next(post)
            cols = slice(c * PROJ_CHUNK, (c + 1) * PROJ_CHUNK)
            zc = lax.dot_general(h, win_ref[cols, :], nt, preferred_element_type=F32)
            if c in direct:
                ref, k, fn = direct[c]
                ref[:, k * PROJ_CHUNK:(k + 1) * PROJ_CHUNK] = (zc if fn is None else fn(zc)).astype(BF16)
            else:
                z_write[:, cols] = zc
        for _ in post:
            pass

    @pl.when(i % 2 == 0)
    def _():
        step(z1_ref, z0_ref)

    @pl.when(i % 2 == 1)
    def _():
        step(z0_ref, z1_ref)


def _proj_post(z, cos_ref, sin_ref, wmisc_ref, bgate_ref,
               qng_ref, wuq_ref, kvng_ref, wukv_ref, qgn_ref, qgr_ref, kgn_ref, kgr_ref, kbias_ref,
               qd_ref, ki_ref, ke_ref, dec_ref, q_ref, k_ref, v_ref):
    tm = z.shape[0]

    misc = z[:, OFF_MISC:OFF_MISC + LANES]
    logit = jnp.dot(misc.astype(BF16), wmisc_ref[...], preferred_element_type=F32) + bgate_ref[...]
    cqn = _rms(z[:, OFF_CQ:OFF_CQ + MLA_Q_RANK], qng_ref[...]).astype(BF16)
    qraw = jnp.dot(cqn, wuq_ref[...], preferred_element_type=F32)
    ckvn = _rms(z[:, OFF_CKV:OFF_CKV + MLA_KV_RANK], kvng_ref[...]).astype(BF16)
    kvraw = jnp.dot(ckvn, wukv_ref[...], preferred_element_type=F32)
    yield

    log_a = (jnp.minimum(logit, 0.0) - jnp.log(1.0 + jnp.exp(-jnp.abs(logit)))) * (1.0 / GLA_GATE_TEMP)

    cb = CUMSUM_BLOCK
    r = lax.broadcasted_iota(jnp.int32, (cb, cb), 0)
    c = lax.broadcasted_iota(jnp.int32, (cb, cb), 1)
    same = (r // GLA_CHUNK) == (c // GLA_CHUNK)
    lower = jnp.where(same & (c <= r), 1.0, 0.0).astype(BF16)
    upper = jnp.where(same & (c > r), 1.0, 0.0).astype(BF16)
    nck = cb // GLA_CHUNK
    sr = lax.broadcasted_iota(jnp.int32, (8, cb), 0)
    sc = lax.broadcasted_iota(jnp.int32, (8, cb), 1)
    chunk_sum = jnp.where((sc // GLA_CHUNK) == sr, 1.0, 0.0).astype(BF16)

    gq = z[:, OFF_GQ:OFF_GQ + GLA_KEY_WIDTH] * (GLA_DK ** -0.5)
    gk = z[:, OFF_GK:OFF_GK + GLA_KEY_WIDTH]
    for blk in range(tm // cb):
        rows = slice(blk * cb, (blk + 1) * cb)
        la = log_a[rows]
        b = _split_dot(lower, la)
        rest = _split_dot(upper, la)
        tot = _split_dot(chunk_sum, la)
        qd_ref[rows, :] = (gq[rows] * jnp.exp(b)).astype(BF16)
        ki_ref[rows, :] = (gk[rows] * jnp.exp(-b)).astype(BF16)
        ke_ref[rows, :] = (gk[rows] * jnp.exp(rest)).astype(BF16)
        dec_ref[blk * nck:(blk + 1) * nck, :] = jnp.exp(tot[:nck])
    yield

    cos = cos_ref[...]
    sin = sin_ref[...]
    lane = lax.broadcasted_iota(jnp.int32, (tm, LANES), 1)
    low64 = lane < MLA_ROPE

    scale = MLA_QK ** -0.5 * LOG2E
    one_col = jnp.where(lane == MLA_ROPE, 1.0, 0.0)
    nope_w = MLA_HEADS * MLA_NOPE
    for pair in range(MLA_HEADS // 2):
        t = qraw[:, nope_w + pair * LANES:nope_w + (pair + 1) * LANES]
        t2 = t * t
        ssq_lo = jnp.sum(jnp.where(low64, t2, 0.0), axis=-1, keepdims=True)
        ssq_hi = jnp.sum(jnp.where(low64, 0.0, t2), axis=-1, keepdims=True)
        tg = t * qgr_ref[...]
        rot = tg * cos + _swap_halves(tg) * sin
        for sub in range(2):
            hd = 2 * pair + sub
            qn = qraw[:, hd * MLA_NOPE:(hd + 1) * MLA_NOPE]
            ssq = jnp.sum(qn * qn, axis=-1, keepdims=True) + (ssq_lo if sub == 0 else ssq_hi)
            rinv = lax.rsqrt(ssq * (1.0 / MLA_QK) + EPS) * scale
            q_ref[hd, :, 0:LANES] = (qn * rinv * qgn_ref[...]).astype(BF16)
            rsel = rot if sub == 0 else pltpu.roll(rot, MLA_ROPE, axis=1)
            q_ref[hd, :, LANES:QK_PAD] = jnp.where(low64, rsel * rinv, one_col).astype(BF16)

    kpe2 = misc * misc
    ssq_pe = jnp.sum(jnp.where(low64, kpe2, 0.0), axis=-1, keepdims=True)
    kg = misc * kgr_ref[...]
    krot = kg * cos + _swap_halves(kg) * sin
    for hd in range(MLA_HEADS):
        kn = kvraw[:, hd * MLA_NOPE:(hd + 1) * MLA_NOPE]
        ssq = jnp.sum(kn * kn, axis=-1, keepdims=True) + ssq_pe
        rinv = lax.rsqrt(ssq * (1.0 / MLA_QK) + EPS)
        k_ref[hd, :, 0:LANES] = (kn * rinv * kgn_ref[...]).astype(BF16)
        k_ref[hd, :, LANES:QK_PAD] = (krot * rinv + kbias_ref[...]).astype(BF16)
        v_ref[hd, :, :] = kvraw[:, nope_w + hd * MLA_DV:nope_w + (hd + 1) * MLA_DV].astype(BF16)


def _gla_kernel(qd_ref, ki_ref, ke_ref, v_ref, dec_ref, gate_ref, g_ref, o_ref, state_ref, *, steps_per_seq):
    tm = qd_ref.shape[0]

    @pl.when(pl.program_id(0) % steps_per_seq == 0)
    def _():
        state_ref[...] = jnp.zeros_like(state_ref)

    C = GLA_CHUNK
    klane_head = lax.broadcasted_iota(jnp.int32, (C, GLA_KEY_WIDTH), 1) // GLA_DK
    vlane_head = lax.broadcasted_iota(jnp.int32, (C, GLA_WIDTH), 1) // GLA_DV
    row = lax.broadcasted_iota(jnp.int32, (C, GLA_KEY_WIDTH), 0)
    col_in_head = lax.broadcasted_iota(jnp.int32, (C, GLA_KEY_WIDTH), 1) % GLA_DK
    causal = row >= col_in_head
    pair_rows = 2 * GLA_DV
    own = ((lax.broadcasted_iota(jnp.int32, (pair_rows, LANES), 0) // GLA_DV)
           == (lax.broadcasted_iota(jnp.int32, (pair_rows, LANES), 1) // GLA_DK))
    nt = (((1,), (1,)), ((), ()))
    tn = (((0,), (0,)), ((), ()))
    g = g_ref[...]
    n_pairs = GLA_HEADS // 2

    for ck in range(tm // C):
        rows = slice(ck * C, (ck + 1) * C)
        qd = qd_ref[rows, :]
        ki = ki_ref[rows, :]
        ke = ke_ref[rows, :]
        v = v_ref[rows, :]
        upd = [lax.dot_general(v[:, p * pair_rows:(p + 1) * pair_rows], ke[:, p * LANES:(p + 1) * LANES],
                               tn, preferred_element_type=F32) for p in range(n_pairs)]
        zk = jnp.zeros_like(ki)
        zv = jnp.zeros_like(v)
        kstack = jnp.concatenate([jnp.where(klane_head == hd, ki, zk) for hd in range(GLA_HEADS)], axis=0)
        vstack = jnp.concatenate([jnp.where(vlane_head == hd, v, zv) for hd in range(GLA_HEADS)], axis=0)
        a = lax.dot_general(qd, kstack, nt, preferred_element_type=F32)
        a = jnp.where(causal, a, 0.0).astype(BF16)
        o = jnp.dot(a, vstack, preferred_element_type=F32)
        state = state_ref[...]
        o = o + jnp.concatenate(
            [lax.dot_general(qd[:, p * LANES:(p + 1) * LANES],
                             state[p * pair_rows:(p + 1) * pair_rows].astype(BF16),
                             nt, preferred_element_type=F32) for p in range(n_pairs)], axis=1)
        for p in range(n_pairs):
            srows = slice(p * pair_rows, (p + 1) * pair_rows)
            state_ref[srows, :] = (state[srows] * dec_ref[ck:ck + 1, p * LANES:(p + 1) * LANES]
                                   + jnp.where(own, upd[p], 0.0))
        for hd in range(GLA_HEADS):
            cols = slice(hd * GLA_DV, (hd + 1) * GLA_DV)
            oh = o[:, cols]
            on = oh * lax.rsqrt(jnp.mean(oh * oh, axis=-1, keepdims=True) + EPS) * g
            o_ref[rows, cols] = (on * gate_ref[rows, cols].astype(F32)).astype(BF16)


def _attn_kernel(q_ref, k_ref, v_ref, gate_ref, o_ref, m_ref, l_ref, acc_ref):
    tq = q_ref.shape[0]
    tk = ATTN_TK
    i = pl.program_id(2)
    q = q_ref[...]
    nt = (((1,), (1,)), ((), ()))
    m_ref[...] = jnp.full_like(m_ref, -jnp.inf)
    l_ref[...] = jnp.zeros_like(l_ref)
    acc_ref[...] = jnp.zeros_like(acc_ref)

    def step(j, masked):
        start = pl.multiple_of(j * tk, tk)
        k = k_ref[pl.ds(start, tk), :]
        v = v_ref[pl.ds(start, tk), :]
        s = lax.dot_general(q, k, nt, preferred_element_type=F32)
        if masked:
            rr = lax.broadcasted_iota(jnp.int32, s.shape, 0)
            cc = lax.broadcasted_iota(jnp.int32, s.shape, 1)
            s = jnp.where(cc <= rr, s, -jnp.inf)
        m_prev = m_ref[...]
        m_new = jnp.maximum(m_prev, jnp.max(s, axis=-1, keepdims=True))
        alpha = jnp.exp2(m_prev - m_new)
        p = jnp.exp2(s - m_new)
        l_ref[...] = alpha * l_ref[...] + jnp.sum(p, axis=-1, keepdims=True)
        acc_ref[...] = alpha * acc_ref[...] + jnp.dot(p.astype(BF16), v, preferred_element_type=F32)
        m_ref[...] = m_new

    def body(j, carry):
        step(j, False)
        return carry

    lax.fori_loop(0, i * (tq // tk), body, 0)
    step(i * (tq // tk), True)
    o = acc_ref[...] / l_ref[...]
    o_ref[...] = (o * gate_ref[...].astype(F32)).astype(BF16)


def _attn_bounded_kernel(q_ref, k_ref, v_ref, gate_ref, o_ref):
    t = ATTN_TQ
    n_tiles = k_ref.shape[0] // t
    nt = (((1,), (1,)), ((), ()))
    ones_col = jnp.where(lax.broadcasted_iota(jnp.int32, (t, LANES), 1) == 0, 1.0, 0.0).astype(BF16)
    causal = (lax.broadcasted_iota(jnp.int32, (t, t), 1) <= lax.broadcasted_iota(jnp.int32, (t, t), 0))

    def key_tile(q, j, masked):
        k = k_ref[j * t:(j + 1) * t, :]
        v_aug = jnp.concatenate([v_ref[j * t:(j + 1) * t, :], ones_col], axis=1)
        p = jnp.exp2(lax.dot_general(q, k, nt, preferred_element_type=F32))
        if masked:
            p = jnp.where(causal, p, 0.0)
        return jnp.dot(p.astype(BF16), v_aug, preferred_element_type=F32)

    for idx in range(n_tiles):
        rows = slice(idx * t, (idx + 1) * t)
        q = q_ref[rows, :]
        acc = key_tile(q, idx, True)
        for j in range(idx):
            acc = acc + key_tile(q, j, False)
        o = acc[:, :MLA_DV] * (1.0 / acc[:, MLA_DV:MLA_DV + 1])
        o_ref[rows, :] = (o * gate_ref[rows, :].astype(F32)).astype(BF16)


def _out_kernel(x_ref, og_ref, om_ref, w_ref, o_ref):
    w = w_ref[...]
    o_ref[...] = (x_ref[...]
                  + jnp.dot(og_ref[...], w[:GLA_WIDTH], preferred_element_type=F32)
                  + jnp.dot(om_ref[...], w[GLA_WIDTH:], preferred_element_type=F32))


def _full(shape):
    return pl.BlockSpec(shape, lambda *idx: (0,) * len(shape), pipeline_mode=pl.Buffered(1))


def _params(n_axes):
    return pltpu.CompilerParams(dimension_semantics=("arbitrary",) * n_axes,
                                vmem_limit_bytes=VMEM_LIMIT)


def _proj_call(x2, prev, cos_t, sin_t, lw, B, S):
    T = x2.shape[0]
    tm = PROJ_TM
    ns = S // tm
    n = T // tm
    fuse_out = prev is not None
    cur = lambda i: jnp.minimum(i, n - 1)
    lag = lambda i: jnp.maximum(i - 1, 0)
    rows = lambda w, at: pl.BlockSpec((tm, w), lambda i: (at(i), 0))
    heads = lambda w: pl.BlockSpec((None, MLA_HEADS, tm, w), lambda i: (lag(i) // ns, 0, lag(i) % ns, 0))
    front_shapes = (
        jax.ShapeDtypeStruct((T, GLA_WIDTH), BF16),
        jax.ShapeDtypeStruct((T, GLA_WIDTH), BF16),
        jax.ShapeDtypeStruct((T, MLA_WIDTH), BF16),
    )
    front_out_specs = (rows(GLA_WIDTH, cur), rows(GLA_WIDTH, cur), rows(MLA_WIDTH, cur))
    post_shapes = (
        jax.ShapeDtypeStruct((T, GLA_KEY_WIDTH), BF16),
        jax.ShapeDtypeStruct((T, GLA_KEY_WIDTH), BF16),
        jax.ShapeDtypeStruct((T, GLA_KEY_WIDTH), BF16),
        jax.ShapeDtypeStruct((T // GLA_CHUNK, GLA_KEY_WIDTH), F32),
        jax.ShapeDtypeStruct((B, MLA_HEADS, S, QK_PAD), BF16),
        jax.ShapeDtypeStruct((B, MLA_HEADS, S, QK_PAD), BF16),
        jax.ShapeDtypeStruct((B, MLA_HEADS, S, MLA_DV), BF16),
    )
    post_specs = (rows(GLA_KEY_WIDTH, lag), rows(GLA_KEY_WIDTH, lag), rows(GLA_KEY_WIDTH, lag),
                  pl.BlockSpec((tm // GLA_CHUNK, GLA_KEY_WIDTH), lambda i: (lag(i), 0)),
                  heads(QK_PAD), heads(QK_PAD), heads(MLA_DV))
    assert len(post_specs) == N_POST_OUT

    front_specs = [rows(D_MODEL, cur)]
    front_args = [x2]
    if fuse_out:
        o_gla, o_mla, w_out = prev
        front_specs += [rows(GLA_WIDTH, cur), rows(MLA_WIDTH, cur), _full((D_MODEL, D_MODEL))]
        front_args += [o_gla, o_mla, w_out]
    front_specs += [_full((1, D_MODEL)), _full((D_IN_PAD, D_MODEL))]
    front_args += [lw["norm_g"], lw["w_in"]]
    post_in_specs = [rows(LANES, lag), rows(LANES, lag),
                     _full((LANES, GLA_KEY_WIDTH)), _full((1, GLA_KEY_WIDTH)),
                     _full((1, MLA_Q_RANK)), _full((MLA_Q_RANK, MLA_HEADS * MLA_QK)),
                     _full((1, MLA_KV_RANK)), _full((MLA_KV_RANK, MLA_HEADS * (MLA_NOPE + MLA_DV))),
                     _full((1, LANES)), _full((1, LANES)), _full((1, LANES)), _full((1, LANES)),
                     _full((1, LANES))]
    post_in_args = [cos_t, sin_t, lw["w_misc"], lw["b_gate"], lw["q_norm_g"], lw["w_uq"],
                    lw["kv_norm_g"], lw["w_ukv"], lw["qg_nope"], lw["qg_rope"], lw["kg_nope"],
                    lw["kg_rope"], lw["k_bias"]]
    assert len(post_in_specs) == N_POST_IN

    out_shape = front_shapes + post_shapes
    out_specs = front_out_specs + post_specs
    if fuse_out:
        out_shape = (jax.ShapeDtypeStruct((T, D_MODEL), F32),) + out_shape
        out_specs = (rows(D_MODEL, cur),) + out_specs
    return pl.pallas_call(
        functools.partial(_proj_kernel, fuse_out=fuse_out),
        grid=(n + 1,), in_specs=front_specs + post_in_specs, out_specs=out_specs, out_shape=out_shape,
        scratch_shapes=[pltpu.VMEM((tm, D_IN_PAD), F32), pltpu.VMEM((tm, D_IN_PAD), F32)],
        compiler_params=_params(1), name="outproj_proj" if fuse_out else "proj",
    )(*front_args, *post_in_args)


def _gla_call(qd, ki, ke, gv, dec, gg, gla_g, S):
    T = qd.shape[0]
    tm = GLA_TM
    rows = lambda w: pl.BlockSpec((tm, w), lambda i: (i, 0))
    return pl.pallas_call(
        functools.partial(_gla_kernel, steps_per_seq=S // tm),
        grid=(T // tm,),
        in_specs=[rows(GLA_KEY_WIDTH), rows(GLA_KEY_WIDTH), rows(GLA_KEY_WIDTH), rows(GLA_WIDTH),
                  pl.BlockSpec((tm // GLA_CHUNK, GLA_KEY_WIDTH), lambda i: (i, 0)),
                  rows(GLA_WIDTH), _full((1, GLA_DV))],
        out_specs=rows(GLA_WIDTH),
        out_shape=jax.ShapeDtypeStruct((T, GLA_WIDTH), BF16),
        scratch_shapes=[pltpu.VMEM((GLA_WIDTH, LANES), F32)],
        compiler_params=_params(1), name="gla",
    )(qd, ki, ke, gv, dec, gg, gla_g)


def _attn_call(q, k, v, mg, bounded):
    B, H, S, _ = q.shape
    tq = ATTN_TQ
    nq = S // tq
    out_shape = jax.ShapeDtypeStruct(mg.shape, BF16)
    kv_specs = [pl.BlockSpec((None, None, S, QK_PAD), lambda b, h, *_: (b, h, 0, 0)),
                pl.BlockSpec((None, None, S, MLA_DV), lambda b, h, *_: (b, h, 0, 0))]
    if bounded:
        io_spec = pl.BlockSpec((S, MLA_DV), lambda b, h: (b, h))
        return pl.pallas_call(
            _attn_bounded_kernel, grid=(B, H),
            in_specs=[kv_specs[0]] + kv_specs + [io_spec], out_specs=io_spec, out_shape=out_shape,
            compiler_params=_params(2), name="attn_bounded",
        )(q, k, v, mg)
    q_spec = pl.BlockSpec((None, None, tq, QK_PAD), lambda b, h, i: (b, h, i, 0))
    io_spec = pl.BlockSpec((tq, MLA_DV), lambda b, h, i: (b * nq + i, h))
    return pl.pallas_call(
        _attn_kernel, grid=(B, H, nq),
        in_specs=[q_spec] + kv_specs + [io_spec], out_specs=io_spec, out_shape=out_shape,
        scratch_shapes=[pltpu.VMEM((tq, 1), F32), pltpu.VMEM((tq, 1), F32), pltpu.VMEM((tq, MLA_DV), F32)],
        compiler_params=_params(3), name="attn_online",
    )(q, k, v, mg)


def _out_call(x2, og, om, w_out):
    T = x2.shape[0]
    tm = OUT_TM
    rows = lambda w: pl.BlockSpec((tm, w), lambda i: (i, 0))
    return pl.pallas_call(
        _out_kernel, grid=(T // tm,),
        in_specs=[rows(D_MODEL), rows(GLA_WIDTH), rows(MLA_WIDTH), _full((D_MODEL, D_MODEL))],
        out_specs=rows(D_MODEL),
        out_shape=jax.ShapeDtypeStruct((T, D_MODEL), F32),
        compiler_params=_params(1), name="outproj",
    )(x2, og, om, w_out)


def _prep_weights(norm_g, w_in, w_gla_gate_up, b_gla_gate, gla_norm_g, mla_q_norm_g, w_uq,
                  mla_kv_norm_g, w_ukv, q_head_g, k_head_g, w_out):
    depth = w_in.shape[0]
    o = [0]
    for wdt in (GLA_KEY_WIDTH, GLA_KEY_WIDTH, GLA_WIDTH, GLA_GATE_RANK, GLA_WIDTH,
                MLA_Q_RANK, MLA_KV_RANK, MLA_ROPE, MLA_WIDTH):
        o.append(o[-1] + wdt)
    wt = jnp.swapaxes(w_in, 1, 2).astype(BF16)
    pad = jnp.zeros((depth, LANES - MLA_ROPE - GLA_GATE_RANK, D_MODEL), BF16)
    w_in_p = jnp.concatenate([wt[:, :o[3]], wt[:, o[4]:o[8]], wt[:, o[3]:o[4]], pad, wt[:, o[8]:]], axis=1)
    assert w_in_p.shape[1] == D_IN_PAD and o[3] == OFF_GGATE and o[8] - o[4] + o[3] == OFF_MISC + MLA_ROPE

    w_misc = jnp.zeros((depth, LANES, GLA_KEY_WIDTH), F32)
    w_misc = w_misc.at[:, MLA_ROPE:MLA_ROPE + GLA_GATE_RANK, :].set(w_gla_gate_up).astype(BF16)

    wq = w_uq.reshape(depth, MLA_Q_RANK, MLA_HEADS, MLA_QK)
    w_uq_p = jnp.concatenate([wq[..., :MLA_NOPE].reshape(depth, MLA_Q_RANK, -1),
                              wq[..., MLA_NOPE:].reshape(depth, MLA_Q_RANK, -1)], axis=-1).astype(BF16)
    wkv = w_ukv.reshape(depth, MLA_KV_RANK, MLA_HEADS, MLA_NOPE + MLA_DV)
    w_ukv_p = jnp.concatenate([wkv[..., :MLA_NOPE].reshape(depth, MLA_KV_RANK, -1),
                               wkv[..., MLA_NOPE:].reshape(depth, MLA_KV_RANK, -1)], axis=-1).astype(BF16)

    zeros64 = jnp.zeros((depth, 1, LANES - MLA_ROPE), F32)
    row = lambda a: a[:, None, :]
    return dict(
        norm_g=row(norm_g), w_in=w_in_p, w_misc=w_misc, b_gate=row(b_gla_gate),
        gla_g=row(gla_norm_g), q_norm_g=row(mla_q_norm_g), w_uq=w_uq_p,
        kv_norm_g=row(mla_kv_norm_g), w_ukv=w_ukv_p,
        qg_nope=row(q_head_g[:, :MLA_NOPE]),
        qg_rope=jnp.concatenate([row(q_head_g[:, MLA_NOPE:])] * 2, axis=-1),
        kg_nope=row(k_head_g[:, :MLA_NOPE]),
        kg_rope=jnp.concatenate([row(k_head_g[:, MLA_NOPE:]), zeros64], axis=-1),
        w_out=w_out.astype(BF16),
    )


def _score_bounds(q_head_g, k_head_g):
    bound = (MLA_QK ** 0.5 * LOG2E * SCORE_BOUND_SLACK
             * jnp.max(jnp.abs(q_head_g), axis=-1) * jnp.max(jnp.abs(k_head_g), axis=-1))
    use = bound <= SCORE_BOUND_MAX_LOG2
    depth = q_head_g.shape[0]
    bias = jnp.zeros((depth, 1, LANES), F32).at[:, 0, MLA_ROPE].set(jnp.where(use, -bound, 0.0))
    return use, bias


def _rope_tables(positions):
    inv_freq = ROPE_THETA ** (-jnp.arange(0, MLA_ROPE, 2, dtype=F32) / MLA_ROPE)
    reps = LANES // inv_freq.shape[0]
    inv_freq = jnp.tile(inv_freq, reps)
    sign = jnp.tile(jnp.repeat(jnp.array([-1.0, 1.0], F32), MLA_ROPE // 2), LANES // MLA_ROPE)
    ang = positions.astype(F32).reshape(-1, 1) * inv_freq
    return jnp.cos(ang), jnp.sin(ang) * sign


def kernel(x, positions, norm_g, w_in, w_gla_gate_up, b_gla_gate, gla_norm_g, mla_q_norm_g, w_uq,
           mla_kv_norm_g, w_ukv, q_head_g, k_head_g, w_out):
    B, S, D = x.shape
    assert D == D_MODEL and S % max(PROJ_TM, GLA_TM, ATTN_TQ, OUT_TM) == 0 and ATTN_TQ % ATTN_TK == 0
    depth = w_in.shape[0]
    weights = _prep_weights(norm_g, w_in, w_gla_gate_up, b_gla_gate, gla_norm_g, mla_q_norm_g, w_uq,
                            mla_kv_norm_g, w_ukv, q_head_g, k_head_g, w_out)
    use_bounded, weights["k_bias"] = _score_bounds(q_head_g, k_head_g)
    cos_t, sin_t = _rope_tables(positions)
    x2 = x.reshape(B * S, D)
    prev = None
    for l in range(depth):
        lw = {name: w[l] for name, w in weights.items()}
        outs = _proj_call(x2, prev, cos_t, sin_t, lw, B, S)
        if prev is not None:
            x2, outs = outs[0], outs[1:]
        gv, gg, mg, qd, ki, ke, dec, q, k, v = outs
        o_gla = _gla_call(qd, ki, ke, gv, dec, gg, lw["gla_g"], S)
        o_mla = lax.cond(use_bounded[l],
                         functools.partial(_attn_call, bounded=True),
                         functools.partial(_attn_call, bounded=False), q, k, v, mg)
        prev = (o_gla, o_mla, lw["w_out"])
    x2 = _out_call(x2, *prev)
    return x2.reshape(B, S, D)
```

```python
import functools

import jax
import jax.numpy as jnp
from jax import lax
from jax.experimental import pallas as pl
from jax.experimental.pallas import tpu as pltpu

F32 = jnp.float32
BF16 = jnp.bfloat16

D_MODEL = 1024
GLA_HEADS = 4
GLA_DK = 64
GLA_DV = 128
GLA_KEY_WIDTH = GLA_HEADS * GLA_DK
GLA_WIDTH = GLA_HEADS * GLA_DV
GLA_GATE_RANK = 16
GLA_GATE_TEMP = 16.0
GLA_CHUNK = 64
MLA_HEADS = 4
MLA_NOPE = 128
MLA_ROPE = 64
MLA_QK = MLA_NOPE + MLA_ROPE
MLA_DV = 128
MLA_WIDTH = MLA_HEADS * MLA_DV
MLA_Q_RANK = 256
MLA_KV_RANK = 128
ROPE_THETA = 10000.0
EPS = 1e-6
LOG2E = 1.4426950408889634
SCORE_BOUND_MAX_LOG2 = 40.0 * LOG2E
SCORE_BOUND_SLACK = 1.02

LANES = 128
QK_PAD = 2 * LANES

OFF_GQ = 0
OFF_GK = 256
OFF_GV = 512
OFF_GGATE = 1024
OFF_CQ = 1536
OFF_CKV = 1792
OFF_MISC = 1920
OFF_MGATE = 2048
D_IN_PAD = 2560

PROJ_TM = 512
GLA_TM = 512
ATTN_TQ = 512
ATTN_TK = 512
OUT_TM = 512
CUMSUM_BLOCK = 256
PROJ_CHUNK = 256
VMEM_LIMIT = 56 * 1024 * 1024


def _rms(t, g):
    return t * lax.rsqrt(jnp.mean(t * t, axis=-1, keepdims=True) + EPS) * g


def _silu(t):
    return t * jax.nn.sigmoid(t)


def _split_dot(mat_bf16, t):
    hi = t.astype(BF16)
    lo = (t - hi.astype(F32)).astype(BF16)
    return (jnp.dot(mat_bf16, hi, preferred_element_type=F32)
            + jnp.dot(mat_bf16, lo, preferred_element_type=F32))


def _swap_halves(t):
    lane = lax.broadcasted_iota(jnp.int32, t.shape, 1)
    return jnp.where(lane % MLA_ROPE < MLA_ROPE // 2,
                     pltpu.roll(t, LANES - MLA_ROPE // 2, axis=1),
                     pltpu.roll(t, MLA_ROPE // 2, axis=1))


N_POST_IN = 13
N_POST_OUT = 7


def _proj_kernel(*refs, fuse_out):
    refs = list(refs)
    x_ref = refs.pop(0)
    if fuse_out:
        og_ref, om_ref, wout_ref = refs[:3]
        del refs[:3]
    ng_ref, win_ref = refs[:2]
    post_in = refs[2:2 + N_POST_IN]
    del refs[:2 + N_POST_IN]
    if fuse_out:
        xo_ref = refs.pop(0)
    gv_ref, gg_ref, mg_ref = refs[:3]
    post_out = refs[3:3 + N_POST_OUT]
    z0_ref, z1_ref = refs[3 + N_POST_OUT:]
    i = pl.program_id(0)
    nt = (((1,), (1,)), ((), ()))
    direct = {}
    for ref, off, fn in ((gv_ref, OFF_GV, None), (gg_ref, OFF_GGATE, _silu), (mg_ref, OFF_MGATE, _silu)):
        for k in range(ref.shape[1] // PROJ_CHUNK):
            direct[off // PROJ_CHUNK + k] = (ref, k, fn)

    @pl.when(i == 0)
    def _():
        z1_ref[...] = jnp.zeros_like(z1_ref)

    def step(z_read, z_write):
        post = _proj_post(z_read, *post_in, *post_out)
        x = x_ref[...]
        if fuse_out:
            w = wout_ref[...]
            x = (x + jnp.dot(og_ref[...], w[:GLA_WIDTH], preferred_element_type=F32)
                 + jnp.dot(om_ref[...], w[GLA_WIDTH:], preferred_element_type=F32))
            xo_ref[...] = x
        h = _rms(x, ng_ref[...]).astype(BF16)
        n_chunks = D_IN_PAD // PROJ_CHUNK
        for c in range(n_chunks):
            if c in (0, n_chunks // 2):
                next(post)
            cols = slice(c * PROJ_CHUNK, (c + 1) * PROJ_CHUNK)
            zc = lax.dot_general(h, win_ref[cols, :], nt, preferred_element_type=F32)
            if c in direct:
                ref, k, fn = direct[c]
                ref[:, k * PROJ_CHUNK:(k + 1) * PROJ_CHUNK] = (zc if fn is None else fn(zc)).astype(BF16)
            else:
                z_write[:, cols] = zc
        for _ in post:
            pass

    @pl.when(i % 2 == 0)
    def _():
        step(z1_ref, z0_ref)

    @pl.when(i % 2 == 1)
    def _():
        step(z0_ref, z1_ref)


def _proj_post(z, cos_ref, sin_ref, wmisc_ref, bgate_ref,
               qng_ref, wuq_ref, kvng_ref, wukv_ref, qgn_ref, qgr_ref, kgn_ref, kgr_ref, kbias_ref,
               qd_ref, ki_ref, ke_ref, dec_ref, q_ref, k_ref, v_ref):
    tm = z.shape[0]

    misc = z[:, OFF_MISC:OFF_MISC + LANES]
    logit = jnp.dot(misc.astype(BF16), wmisc_ref[...], preferred_element_type=F32) + bgate_ref[...]
    cqn = _rms(z[:, OFF_CQ:OFF_CQ + MLA_Q_RANK], qng_ref[...]).astype(BF16)
    qraw = jnp.dot(cqn, wuq_ref[...], preferred_element_type=F32)
    ckvn = _rms(z[:, OFF_CKV:OFF_CKV + MLA_KV_RANK], kvng_ref[...]).astype(BF16)
    kvraw = jnp.dot(ckvn, wukv_ref[...], preferred_element_type=F32)
    yield

    log_a = (jnp.minimum(logit, 0.0) - jnp.log(1.0 + jnp.exp(-jnp.abs(logit)))) * (1.0 / GLA_GATE_TEMP)

    cb = CUMSUM_BLOCK
    r = lax.broadcasted_iota(jnp.int32, (cb, cb), 0)
    c = lax.broadcasted_iota(jnp.int32, (cb, cb), 1)
    same = (r // GLA_CHUNK) == (c // GLA_CHUNK)
    lower = jnp.where(same & (c <= r), 1.0, 0.0).astype(BF16)
    upper = jnp.where(same & (c > r), 1.0, 0.0).astype(BF16)
    nck = cb // GLA_CHUNK
    sr = lax.broadcasted_iota(jnp.int32, (8, cb), 0)
    sc = lax.broadcasted_iota(jnp.int32, (8, cb), 1)
    chunk_sum = jnp.where((sc // GLA_CHUNK) == sr, 1.0, 0.0).astype(BF16)

    gq = z[:, OFF_GQ:OFF_GQ + GLA_KEY_WIDTH] * (GLA_DK ** -0.5)
    gk = z[:, OFF_GK:OFF_GK + GLA_KEY_WIDTH]
    for blk in range(tm // cb):
        rows = slice(blk * cb, (blk + 1) * cb)
        la = log_a[rows]
        b = _split_dot(lower, la)
        rest = _split_dot(upper, la)
        tot = _split_dot(chunk_sum, la)
        qd_ref[rows, :] = (gq[rows] * jnp.exp(b)).astype(BF16)
        ki_ref[rows, :] = (gk[rows] * jnp.exp(-b)).astype(BF16)
        ke_ref[rows, :] = (gk[rows] * jnp.exp(rest)).astype(BF16)
        dec_ref[blk * nck:(blk + 1) * nck, :] = jnp.exp(tot[:nck])
    yield

    cos = cos_ref[...]
    sin = sin_ref[...]
    lane = lax.broadcasted_iota(jnp.int32, (tm, LANES), 1)
    low64 = lane < MLA_ROPE

    scale = MLA_QK ** -0.5 * LOG2E
    one_col = jnp.where(lane == MLA_ROPE, 1.0, 0.0)
    nope_w = MLA_HEADS * MLA_NOPE
    for pair in range(MLA_HEADS // 2):
        t = qraw[:, nope_w + pair * LANES:nope_w + (pair + 1) * LANES]
        t2 = t * t
        ssq_lo = jnp.sum(jnp.where(low64, t2, 0.0), axis=-1, keepdims=True)
        ssq_hi = jnp.sum(jnp.where(low64, 0.0, t2), axis=-1, keepdims=True)
        tg = t * qgr_ref[...]
        rot = tg * cos + _swap_halves(tg) * sin
        for sub in range(2):
            hd = 2 * pair + sub
            qn = qraw[:, hd * MLA_NOPE:(hd + 1) * MLA_NOPE]
            ssq = jnp.sum(qn * qn, axis=-1, keepdims=True) + (ssq_lo if sub == 0 else ssq_hi)
            rinv = lax.rsqrt(ssq * (1.0 / MLA_QK) + EPS) * scale
            q_ref[hd, :, 0:LANES] = (qn * rinv * qgn_ref[...]).astype(BF16)
            rsel = rot if sub == 0 else pltpu.roll(rot, MLA_ROPE, axis=1)
            q_ref[hd, :, LANES:QK_PAD] = jnp.where(low64, rsel * rinv, one_col).astype(BF16)

    kpe2 = misc * misc
    ssq_pe = jnp.sum(jnp.where(low64, kpe2, 0.0), axis=-1, keepdims=True)
    kg = misc * kgr_ref[...]
    krot = kg * cos + _swap_halves(kg) * sin
    for hd in range(MLA_HEADS):
        kn = kvraw[:, hd * MLA_NOPE:(hd + 1) * MLA_NOPE]
        ssq = jnp.sum(kn * kn, axis=-1, keepdims=True) + ssq_pe
        rinv = lax.rsqrt(ssq * (1.0 / MLA_QK) + EPS)
        k_ref[hd, :, 0:LANES] = (kn * rinv * kgn_ref[...]).astype(BF16)
        k_ref[hd, :, LANES:QK_PAD] = (krot * rinv + kbias_ref[...]).astype(BF16)
        v_ref[hd, :, :] = kvraw[:, nope_w + hd * MLA_DV:nope_w + (hd + 1) * MLA_DV].astype(BF16)


def _gla_kernel(qd_ref, ki_ref, ke_ref, v_ref, dec_ref, gate_ref, g_ref, o_ref, state_ref, *, steps_per_seq):
    @pl.when(pl.program_id(0) % steps_per_seq == 0)
    def _():
        state_ref[...] = jnp.zeros_like(state_ref)

    for _ in _gla_chunks(qd_ref, ki_ref, ke_ref, v_ref, dec_ref, gate_ref, g_ref, o_ref, state_ref):
        pass


def _gla_chunks(qd_ref, ki_ref, ke_ref, v_ref, dec_ref, gate_ref, g_ref, o_ref, state_ref):
    tm = qd_ref.shape[0]

    C = GLA_CHUNK
    klane_head = lax.broadcasted_iota(jnp.int32, (C, GLA_KEY_WIDTH), 1) // GLA_DK
    vlane_head = lax.broadcasted_iota(jnp.int32, (C, GLA_WIDTH), 1) // GLA_DV
    row = lax.broadcasted_iota(jnp.int32, (C, GLA_KEY_WIDTH), 0)
    col_in_head = lax.broadcasted_iota(jnp.int32, (C, GLA_KEY_WIDTH), 1) % GLA_DK
    causal = row >= col_in_head
    pair_rows = 2 * GLA_DV
    own = ((lax.broadcasted_iota(jnp.int32, (pair_rows, LANES), 0) // GLA_DV)
           == (lax.broadcasted_iota(jnp.int32, (pair_rows, LANES), 1) // GLA_DK))
    nt = (((1,), (1,)), ((), ()))
    tn = (((0,), (0,)), ((), ()))
    g = g_ref[...]
    n_pairs = GLA_HEADS // 2

    for ck in range(tm // C):
        rows = slice(ck * C, (ck + 1) * C)
        qd = qd_ref[rows, :]
        ki = ki_ref[rows, :]
        ke = ke_ref[rows, :]
        v = v_ref[rows, :]
        upd = [lax.dot_general(v[:, p * pair_rows:(p + 1) * pair_rows], ke[:, p * LANES:(p + 1) * LANES],
                               tn, preferred_element_type=F32) for p in range(n_pairs)]
        zk = jnp.zeros_like(ki)
        zv = jnp.zeros_like(v)
        kstack = jnp.concatenate([jnp.where(klane_head == hd, ki, zk) for hd in range(GLA_HEADS)], axis=0)
        vstack = jnp.concatenate([jnp.where(vlane_head == hd, v, zv) for hd in range(GLA_HEADS)], axis=0)
        a = lax.dot_general(qd, kstack, nt, preferred_element_type=F32)
        a = jnp.where(causal, a, 0.0).astype(BF16)
        o = jnp.dot(a, vstack, preferred_element_type=F32)
        state = state_ref[...]
        o = o + jnp.concatenate(
            [lax.dot_general(qd[:, p * LANES:(p + 1) * LANES],
                             state[p * pair_rows:(p + 1) * pair_rows].astype(BF16),
                             nt, preferred_element_type=F32) for p in range(n_pairs)], axis=1)
        for p in range(n_pairs):
            srows = slice(p * pair_rows, (p + 1) * pair_rows)
            state_ref[srows, :] = (state[srows] * dec_ref[ck:ck + 1, p * LANES:(p + 1) * LANES]
                                   + jnp.where(own, upd[p], 0.0))
        for hd in range(GLA_HEADS):
            cols = slice(hd * GLA_DV, (hd + 1) * GLA_DV)
            oh = o[:, cols]
            on = oh * lax.rsqrt(jnp.mean(oh * oh, axis=-1, keepdims=True) + EPS) * g
            o_ref[rows, cols] = (on * gate_ref[rows, cols].astype(F32)).astype(BF16)
        yield


def _attn_kernel(q_ref, k_ref, v_ref, gate_ref, o_ref, m_ref, l_ref, acc_ref):
    tq = q_ref.shape[0]
    tk = ATTN_TK
    i = pl.program_id(2)
    q = q_ref[...]
    nt = (((1,), (1,)), ((), ()))
    m_ref[...] = jnp.full_like(m_ref, -jnp.inf)
    l_ref[...] = jnp.zeros_like(l_ref)
    acc_ref[...] = jnp.zeros_like(acc_ref)

    def step(j, masked):
        start = pl.multiple_of(j * tk, tk)
        k = k_ref[pl.ds(start, tk), :]
        v = v_ref[pl.ds(start, tk), :]
        s = lax.dot_general(q, k, nt, preferred_element_type=F32)
        if masked:
            rr = lax.broadcasted_iota(jnp.int32, s.shape, 0)
            cc = lax.broadcasted_iota(jnp.int32, s.shape, 1)
            s = jnp.where(cc <= rr, s, -jnp.inf)
        m_prev = m_ref[...]
        m_new = jnp.maximum(m_prev, jnp.max(s, axis=-1, keepdims=True))
        alpha = jnp.exp2(m_prev - m_new)
        p = jnp.exp2(s - m_new)
        l_ref[...] = alpha * l_ref[...] + jnp.sum(p, axis=-1, keepdims=True)
        acc_ref[...] = alpha * acc_ref[...] + jnp.dot(p.astype(BF16), v, preferred_element_type=F32)
        m_ref[...] = m_new

    def body(j, carry):
        step(j, False)
        return carry

    lax.fori_loop(0, i * (tq // tk), body, 0)
    step(i * (tq // tk), True)
    o = acc_ref[...] / l_ref[...]
    o_ref[...] = (o * gate_ref[...].astype(F32)).astype(BF16)


def _attn_gla_kernel(q_ref, k_ref, v_ref, gate_ref, qd_ref, ki_ref, ke_ref, gv_ref, dec_ref, gg_ref, glag_ref,
                     o_ref, og_ref, state_ref):
    @pl.when(pl.program_id(1) == 0)
    def _():
        state_ref[...] = jnp.zeros_like(state_ref)

    t = ATTN_TQ
    n_tiles = k_ref.shape[0] // t
    nt = (((1,), (1,)), ((), ()))
    ones_col = jnp.where(lax.broadcasted_iota(jnp.int32, (t, LANES), 1) == 0, 1.0, 0.0).astype(BF16)
    causal = (lax.broadcasted_iota(jnp.int32, (t, t), 1) <= lax.broadcasted_iota(jnp.int32, (t, t), 0))
    gla = _gla_chunks(qd_ref, ki_ref, ke_ref, gv_ref, dec_ref, gg_ref, glag_ref, og_ref, state_ref)
    n_chunks = qd_ref.shape[0] // GLA_CHUNK
    n_products = n_tiles * (n_tiles + 1) // 2
    done = {"products": 0, "chunks": 0}

    def key_tile(q, j, masked):
        k = k_ref[j * t:(j + 1) * t, :]
        v_aug = jnp.concatenate([v_ref[j * t:(j + 1) * t, :], ones_col], axis=1)
        p = jnp.exp2(lax.dot_general(q, k, nt, preferred_element_type=F32))
        if masked:
            p = jnp.where(causal, p, 0.0)
        out = jnp.dot(p.astype(BF16), v_aug, preferred_element_type=F32)
        done["products"] += 1
        while done["chunks"] < done["products"] * n_chunks // n_products:
            next(gla)
            done["chunks"] += 1
        return out

    for idx in range(n_tiles):
        rows = slice(idx * t, (idx + 1) * t)
        q = q_ref[rows, :]
        acc = key_tile(q, idx, True)
        for j in range(idx):
            acc = acc + key_tile(q, j, False)
        o = acc[:, :MLA_DV] * (1.0 / acc[:, MLA_DV:MLA_DV + 1])
        o_ref[rows, :] = (o * gate_ref[rows, :].astype(F32)).astype(BF16)
    assert done["chunks"] == n_chunks


def _out_kernel(x_ref, og_ref, om_ref, w_ref, o_ref):
    w = w_ref[...]
    o_ref[...] = (x_ref[...]
                  + jnp.dot(og_ref[...], w[:GLA_WIDTH], preferred_element_type=F32)
                  + jnp.dot(om_ref[...], w[GLA_WIDTH:], preferred_element_type=F32))


def _full(shape):
    return pl.BlockSpec(shape, lambda *idx: (0,) * len(shape), pipeline_mode=pl.Buffered(1))


def _params(n_axes):
    return pltpu.CompilerParams(dimension_semantics=("arbitrary",) * n_axes,
                                vmem_limit_bytes=VMEM_LIMIT)


def _proj_call(x2, prev, cos_t, sin_t, lw, B, S):
    T = x2.shape[0]
    tm = PROJ_TM
    ns = S // tm
    n = T // tm
    fuse_out = prev is not None
    cur = lambda i: jnp.minimum(i, n - 1)
    lag = lambda i: jnp.maximum(i - 1, 0)
    rows = lambda w, at: pl.BlockSpec((tm, w), lambda i: (at(i), 0))
    heads = lambda w: pl.BlockSpec((None, MLA_HEADS, tm, w), lambda i: (lag(i) // ns, 0, lag(i) % ns, 0))
    front_shapes = (
        jax.ShapeDtypeStruct((T, GLA_WIDTH), BF16),
        jax.ShapeDtypeStruct((T, GLA_WIDTH), BF16),
        jax.ShapeDtypeStruct((T, MLA_WIDTH), BF16),
    )
    front_out_specs = (rows(GLA_WIDTH, cur), rows(GLA_WIDTH, cur), rows(MLA_WIDTH, cur))
    post_shapes = (
        jax.ShapeDtypeStruct((T, GLA_KEY_WIDTH), BF16),
        jax.ShapeDtypeStruct((T, GLA_KEY_WIDTH), BF16),
        jax.ShapeDtypeStruct((T, GLA_KEY_WIDTH), BF16),
        jax.ShapeDtypeStruct((T // GLA_CHUNK, GLA_KEY_WIDTH), F32),
        jax.ShapeDtypeStruct((B, MLA_HEADS, S, QK_PAD), BF16),
        jax.ShapeDtypeStruct((B, MLA_HEADS, S, QK_PAD), BF16),
        jax.ShapeDtypeStruct((B, MLA_HEADS, S, MLA_DV), BF16),
    )
    post_specs = (rows(GLA_KEY_WIDTH, lag), rows(GLA_KEY_WIDTH, lag), rows(GLA_KEY_WIDTH, lag),
                  pl.BlockSpec((tm // GLA_CHUNK, GLA_KEY_WIDTH), lambda i: (lag(i), 0)),
                  heads(QK_PAD), heads(QK_PAD), heads(MLA_DV))
    assert len(post_specs) == N_POST_OUT

    front_specs = [rows(D_MODEL, cur)]
    front_args = [x2]
    if fuse_out:
        o_gla, o_mla, w_out = prev
        front_specs += [rows(GLA_WIDTH, cur), rows(MLA_WIDTH, cur), _full((D_MODEL, D_MODEL))]
        front_args += [o_gla, o_mla, w_out]
    front_specs += [_full((1, D_MODEL)), _full((D_IN_PAD, D_MODEL))]
    front_args += [lw["norm_g"], lw["w_in"]]
    post_in_specs = [rows(LANES, lag), rows(LANES, lag),
                     _full((LANES, GLA_KEY_WIDTH)), _full((1, GLA_KEY_WIDTH)),
                     _full((1, MLA_Q_RANK)), _full((MLA_Q_RANK, MLA_HEADS * MLA_QK)),
                     _full((1, MLA_KV_RANK)), _full((MLA_KV_RANK, MLA_HEADS * (MLA_NOPE + MLA_DV))),
                     _full((1, LANES)), _full((1, LANES)), _full((1, LANES)), _full((1, LANES)),
                     _full((1, LANES))]
    post_in_args = [cos_t, sin_t, lw["w_misc"], lw["b_gate"], lw["q_norm_g"], lw["w_uq"],
                    lw["kv_norm_g"], lw["w_ukv"], lw["qg_nope"], lw["qg_rope"], lw["kg_nope"],
                    lw["kg_rope"], lw["k_bias"]]
    assert len(post_in_specs) == N_POST_IN

    out_shape = front_shapes + post_shapes
    out_specs = front_out_specs + post_specs
    if fuse_out:
        out_shape = (jax.ShapeDtypeStruct((T, D_MODEL), F32),) + out_shape
        out_specs = (rows(D_MODEL, cur),) + out_specs
    return pl.pallas_call(
        functools.partial(_proj_kernel, fuse_out=fuse_out),
        grid=(n + 1,), in_specs=front_specs + post_in_specs, out_specs=out_specs, out_shape=out_shape,
        scratch_shapes=[pltpu.VMEM((tm, D_IN_PAD), F32), pltpu.VMEM((tm, D_IN_PAD), F32)],
        compiler_params=_params(1), name="outproj_proj" if fuse_out else "proj",
    )(*front_args, *post_in_args)


def _gla_call(qd, ki, ke, gv, dec, gg, gla_g, S):
    T = qd.shape[0]
    tm = GLA_TM
    rows = lambda w: pl.BlockSpec((tm, w), lambda i: (i, 0))
    return pl.pallas_call(
        functools.partial(_gla_kernel, steps_per_seq=S // tm),
        grid=(T // tm,),
        in_specs=[rows(GLA_KEY_WIDTH), rows(GLA_KEY_WIDTH), rows(GLA_KEY_WIDTH), rows(GLA_WIDTH),
                  pl.BlockSpec((tm // GLA_CHUNK, GLA_KEY_WIDTH), lambda i: (i, 0)),
                  rows(GLA_WIDTH), _full((1, GLA_DV))],
        out_specs=rows(GLA_WIDTH),
        out_shape=jax.ShapeDtypeStruct((T, GLA_WIDTH), BF16),
        scratch_shapes=[pltpu.VMEM((GLA_WIDTH, LANES), F32)],
        compiler_params=_params(1), name="gla",
    )(qd, ki, ke, gv, dec, gg, gla_g)


def _attn_call(q, k, v, mg):
    B, H, S, _ = q.shape
    tq = ATTN_TQ
    nq = S // tq
    q_spec = pl.BlockSpec((None, None, tq, QK_PAD), lambda b, h, i: (b, h, i, 0))
    kv_specs = [pl.BlockSpec((None, None, S, QK_PAD), lambda b, h, i: (b, h, 0, 0)),
                pl.BlockSpec((None, None, S, MLA_DV), lambda b, h, i: (b, h, 0, 0))]
    io_spec = pl.BlockSpec((tq, MLA_DV), lambda b, h, i: (b * nq + i, h))
    return pl.pallas_call(
        _attn_kernel, grid=(B, H, nq),
        in_specs=[q_spec] + kv_specs + [io_spec], out_specs=io_spec,
        out_shape=jax.ShapeDtypeStruct(mg.shape, BF16),
        scratch_shapes=[pltpu.VMEM((tq, 1), F32), pltpu.VMEM((tq, 1), F32), pltpu.VMEM((tq, MLA_DV), F32)],
        compiler_params=_params(3), name="attn_online",
    )(q, k, v, mg)


def _attn_gla_call(q, k, v, mg, qd, ki, ke, gv, dec, gg, gla_g):
    B, H, S, _ = q.shape
    tg = S // H
    assert S % (H * GLA_CHUNK) == 0 and tg // GLA_CHUNK % 8 == 0
    seq = lambda w: pl.BlockSpec((None, None, S, w), lambda b, h: (b, h, 0, 0))
    io_spec = pl.BlockSpec((S, MLA_DV), lambda b, h: (b, h))
    grows = lambda w: pl.BlockSpec((tg, w), lambda b, h: (b * H + h, 0))
    o_mla, o_gla = pl.pallas_call(
        _attn_gla_kernel, grid=(B, H),
        in_specs=[seq(QK_PAD), seq(QK_PAD), seq(MLA_DV), io_spec,
                  grows(GLA_KEY_WIDTH), grows(GLA_KEY_WIDTH), grows(GLA_KEY_WIDTH), grows(GLA_WIDTH),
                  pl.BlockSpec((tg // GLA_CHUNK, GLA_KEY_WIDTH), lambda b, h: (b * H + h, 0)),
                  grows(GLA_WIDTH), _full((1, GLA_DV))],
        out_specs=(io_spec, grows(GLA_WIDTH)),
        out_shape=(jax.ShapeDtypeStruct(mg.shape, BF16), jax.ShapeDtypeStruct(gv.shape, BF16)),
        scratch_shapes=[pltpu.VMEM((GLA_WIDTH, LANES), F32)],
        compiler_params=_params(2), name="attn_gla",
    )(q, k, v, mg, qd, ki, ke, gv, dec, gg, gla_g)
    return o_gla, o_mla


def _out_call(x2, og, om, w_out):
    T = x2.shape[0]
    tm = OUT_TM
    rows = lambda w: pl.BlockSpec((tm, w), lambda i: (i, 0))
    return pl.pallas_call(
        _out_kernel, grid=(T // tm,),
        in_specs=[rows(D_MODEL), rows(GLA_WIDTH), rows(MLA_WIDTH), _full((D_MODEL, D_MODEL))],
        out_specs=rows(D_MODEL),
        out_shape=jax.ShapeDtypeStruct((T, D_MODEL), F32),
        compiler_params=_params(1), name="outproj",
    )(x2, og, om, w_out)


def _prep_weights(norm_g, w_in, w_gla_gate_up, b_gla_gate, gla_norm_g, mla_q_norm_g, w_uq,
                  mla_kv_norm_g, w_ukv, q_head_g, k_head_g, w_out):
    depth = w_in.shape[0]
    o = [0]
    for wdt in (GLA_KEY_WIDTH, GLA_KEY_WIDTH, GLA_WIDTH, GLA_GATE_RANK, GLA_WIDTH,
                MLA_Q_RANK, MLA_KV_RANK, MLA_ROPE, MLA_WIDTH):
        o.append(o[-1] + wdt)
    wt = jnp.swapaxes(w_in, 1, 2).astype(BF16)
    pad = jnp.zeros((depth, LANES - MLA_ROPE - GLA_GATE_RANK, D_MODEL), BF16)
    w_in_p = jnp.concatenate([wt[:, :o[3]], wt[:, o[4]:o[8]], wt[:, o[3]:o[4]], pad, wt[:, o[8]:]], axis=1)
    assert w_in_p.shape[1] == D_IN_PAD and o[3] == OFF_GGATE and o[8] - o[4] + o[3] == OFF_MISC + MLA_ROPE

    w_misc = jnp.zeros((depth, LANES, GLA_KEY_WIDTH), F32)
    w_misc = w_misc.at[:, MLA_ROPE:MLA_ROPE + GLA_GATE_RANK, :].set(w_gla_gate_up).astype(BF16)

    wq = w_uq.reshape(depth, MLA_Q_RANK, MLA_HEADS, MLA_QK)
    w_uq_p = jnp.concatenate([wq[..., :MLA_NOPE].reshape(depth, MLA_Q_RANK, -1),
                              wq[..., MLA_NOPE:].reshape(depth, MLA_Q_RANK, -1)], axis=-1).astype(BF16)
    wkv = w_ukv.reshape(depth, MLA_KV_RANK, MLA_HEADS, MLA_NOPE + MLA_DV)
    w_ukv_p = jnp.concatenate([wkv[..., :MLA_NOPE].reshape(depth, MLA_KV_RANK, -1),
                               wkv[..., MLA_NOPE:].reshape(depth, MLA_KV_RANK, -1)], axis=-1).astype(BF16)

    zeros64 = jnp.zeros((depth, 1, LANES - MLA_ROPE), F32)
    row = lambda a: a[:, None, :]
    return dict(
        norm_g=row(norm_g), w_in=w_in_p, w_misc=w_misc, b_gate=row(b_gla_gate),
        gla_g=row(gla_norm_g), q_norm_g=row(mla_q_norm_g), w_uq=w_uq_p,
        kv_norm_g=row(mla_kv_norm_g), w_ukv=w_ukv_p,
        qg_nope=row(q_head_g[:, :MLA_NOPE]),
        qg_rope=jnp.concatenate([row(q_head_g[:, MLA_NOPE:])] * 2, axis=-1),
        kg_nope=row(k_head_g[:, :MLA_NOPE]),
        kg_rope=jnp.concatenate([row(k_head_g[:, MLA_NOPE:]), zeros64], axis=-1),
        w_out=w_out.astype(BF16),
    )


def _score_bounds(q_head_g, k_head_g):
    bound = (MLA_QK ** 0.5 * LOG2E * SCORE_BOUND_SLACK
             * jnp.max(jnp.abs(q_head_g), axis=-1) * jnp.max(jnp.abs(k_head_g), axis=-1))
    use = bound <= SCORE_BOUND_MAX_LOG2
    depth = q_head_g.shape[0]
    bias = jnp.zeros((depth, 1, LANES), F32).at[:, 0, MLA_ROPE].set(jnp.where(use, -bound, 0.0))
    return use, bias


def _rope_tables(positions):
    inv_freq = ROPE_THETA ** (-jnp.arange(0, MLA_ROPE, 2, dtype=F32) / MLA_ROPE)
    reps = LANES // inv_freq.shape[0]
    inv_freq = jnp.tile(inv_freq, reps)
    sign = jnp.tile(jnp.repeat(jnp.array([-1.0, 1.0], F32), MLA_ROPE // 2), LANES // MLA_ROPE)
    ang = positions.astype(F32).reshape(-1, 1) * inv_freq
    return jnp.cos(ang), jnp.sin(ang) * sign


def kernel(x, positions, norm_g, w_in, w_gla_gate_up, b_gla_gate, gla_norm_g, mla_q_norm_g, w_uq,
           mla_kv_norm_g, w_ukv, q_head_g, k_head_g, w_out):
    B, S, D = x.shape
    assert D == D_MODEL and S % max(PROJ_TM, GLA_TM, ATTN_TQ, OUT_TM) == 0 and ATTN_TQ % ATTN_TK == 0
    depth = w_in.shape[0]
    weights = _prep_weights(norm_g, w_in, w_gla_gate_up, b_gla_gate, gla_norm_g, mla_q_norm_g, w_uq,
                            mla_kv_norm_g, w_ukv, q_head_g, k_head_g, w_out)
    use_bounded, weights["k_bias"] = _score_bounds(q_head_g, k_head_g)
    cos_t, sin_t = _rope_tables(positions)
    x2 = x.reshape(B * S, D)
    prev = None
    for l in range(depth):
        lw = {name: w[l] for name, w in weights.items()}
        outs = _proj_call(x2, prev, cos_t, sin_t, lw, B, S)
        if prev is not None:
            x2, outs = outs[0], outs[1:]
        gv, gg, mg, qd, ki, ke, dec, q, k, v = outs
        o_gla, o_mla = lax.cond(
            use_bounded[l],
            lambda: _attn_gla_call(q, k, v, mg, qd, ki, ke, gv, dec, gg, lw["gla_g"]),
            lambda: (_gla_call(qd, ki, ke, gv, dec, gg, lw["gla_g"], S), _attn_call(q, k, v, mg)))
        prev = (o_gla, o_mla, lw["w_out"])
    x2 = _out_call(x2, *prev)
    return x2.reshape(B, S, D)
```

```python
import functools

import jax
import jax.numpy as jnp
from jax import lax
from jax.experimental import pallas as pl
from jax.experimental.pallas import tpu as pltpu

F32 = jnp.float32
BF16 = jnp.bfloat16

D_MODEL = 1024
GLA_HEADS = 4
GLA_DK = 64
GLA_DV = 128
GLA_KEY_WIDTH = GLA_HEADS * GLA_DK
GLA_WIDTH = GLA_HEADS * GLA_DV
GLA_GATE_RANK = 16
GLA_GATE_TEMP = 16.0
GLA_CHUNK = 64
MLA_HEADS = 4
MLA_NOPE = 128
MLA_ROPE = 64
MLA_QK = MLA_NOPE + MLA_ROPE
MLA_DV = 128
MLA_WIDTH = MLA_HEADS * MLA_DV
MLA_Q_RANK = 256
MLA_KV_RANK = 128
ROPE_THETA = 10000.0
EPS = 1e-6
LOG2E = 1.4426950408889634
SCORE_BOUND_MAX_LOG2 = 40.0 * LOG2E
SCORE_BOUND_SLACK = 1.02

LANES = 128
QK_PAD = 2 * LANES

OFF_GQ = 0
OFF_GK = 256
OFF_GV = 512
OFF_GGATE = 1024
OFF_CQ = 1536
OFF_CKV = 1792
OFF_MISC = 1920
OFF_MGATE = 2048
D_IN_PAD = 2560

PROJ_TM = 512
GLA_TM = 512
ATTN_TQ = 512
ATTN_TK = 512
OUT_TM = 512
CUMSUM_BLOCK = 256
PROJ_CHUNK = 256
VMEM_LIMIT = 56 * 1024 * 1024


def _rms(t, g):
    return t * lax.rsqrt(jnp.mean(t * t, axis=-1, keepdims=True) + EPS) * g


def _silu(t):
    return t * jax.nn.sigmoid(t)


def _split_dot(mat_bf16, t):
    hi = t.astype(BF16)
    lo = (t - hi.astype(F32)).astype(BF16)
    return (jnp.dot(mat_bf16, hi, preferred_element_type=F32)
            + jnp.dot(mat_bf16, lo, preferred_element_type=F32))


def _swap_halves(t):
    lane = lax.broadcasted_iota(jnp.int32, t.shape, 1)
    return jnp.where(lane % MLA_ROPE < MLA_ROPE // 2,
                     pltpu.roll(t, LANES - MLA_ROPE // 2, axis=1),
                     pltpu.roll(t, MLA_ROPE // 2, axis=1))


N_POST_IN = 13
N_POST_OUT = 7


def _proj_kernel(*refs, fuse_out):
    refs = list(refs)
    x_ref = refs.pop(0)
    if fuse_out:
        og_ref, om_ref, wout_ref = refs[:3]
        del refs[:3]
    ng_ref, win_ref = refs[:2]
    post_in = refs[2:2 + N_POST_IN]
    del refs[:2 + N_POST_IN]
    if fuse_out:
        xo_ref = refs.pop(0)
    gv_ref, gg_ref, mg_ref = refs[:3]
    post_out = refs[3:3 + N_POST_OUT]
    z0_ref, z1_ref = refs[3 + N_POST_OUT:]
    i = pl.program_id(0)
    nt = (((1,), (1,)), ((), ()))
    direct = {}
    for ref, off, fn in ((gv_ref, OFF_GV, None), (gg_ref, OFF_GGATE, _silu), (mg_ref, OFF_MGATE, _silu)):
        for k in range(ref.shape[1] // PROJ_CHUNK):
            direct[off // PROJ_CHUNK + k] = (ref, k, fn)

    @pl.when(i == 0)
    def _():
        z1_ref[...] = jnp.zeros_like(z1_ref)

    def step(z_read, z_write):
        post = _proj_post(z_read, *post_in, *post_out)
        x = x_ref[...]
        if fuse_out:
            w = wout_ref[...]
            x = (x + jnp.dot(og_ref[...], w[:GLA_WIDTH], preferred_element_type=F32)
                 + jnp.dot(om_ref[...], w[GLA_WIDTH:], preferred_element_type=F32))
            xo_ref[...] = x
        h = _rms(x, ng_ref[...]).astype(BF16)
        order = sorted(range(D_IN_PAD // PROJ_CHUNK), key=lambda c: (c not in direct, c))
        for pos, c in enumerate(order):
            if pos in (0, len(order) // 2):
                next(post)
            cols = slice(c * PROJ_CHUNK, (c + 1) * PROJ_CHUNK)
            zc = lax.dot_general(h, win_ref[cols, :], nt, preferred_element_type=F32)
            if c in direct:
                ref, k, fn = direct[c]
                ref[:, k * PROJ_CHUNK:(k + 1) * PROJ_CHUNK] = (zc if fn is None else fn(zc)).astype(BF16)
            else:
                z_write[:, cols] = zc
        for _ in post:
            pass

    @pl.when(i % 2 == 0)
    def _():
        step(z1_ref, z0_ref)

    @pl.when(i % 2 == 1)
    def _():
        step(z0_ref, z1_ref)


def _proj_post(z, cos_ref, sin_ref, wmisc_ref, bgate_ref,
               qng_ref, wuq_ref, kvng_ref, wukv_ref, qgn_ref, qgr_ref, kgn_ref, kgr_ref, kbias_ref,
               qd_ref, ki_ref, ke_ref, dec_ref, q_ref, k_ref, v_ref):
    tm = z.shape[0]

    misc = z[:, OFF_MISC:OFF_MISC + LANES]
    logit = jnp.dot(misc.astype(BF16), wmisc_ref[...], preferred_element_type=F32) + bgate_ref[...]
    cqn = _rms(z[:, OFF_CQ:OFF_CQ + MLA_Q_RANK], qng_ref[...]).astype(BF16)
    qraw = jnp.dot(cqn, wuq_ref[...], preferred_element_type=F32)
    ckvn = _rms(z[:, OFF_CKV:OFF_CKV + MLA_KV_RANK], kvng_ref[...]).astype(BF16)
    kvraw = jnp.dot(ckvn, wukv_ref[...], preferred_element_type=F32)
    yield

    log_a = (jnp.minimum(logit, 0.0) - jnp.log(1.0 + jnp.exp(-jnp.abs(logit)))) * (1.0 / GLA_GATE_TEMP)

    cb = CUMSUM_BLOCK
    r = lax.broadcasted_iota(jnp.int32, (cb, cb), 0)
    c = lax.broadcasted_iota(jnp.int32, (cb, cb), 1)
    same = (r // GLA_CHUNK) == (c // GLA_CHUNK)
    lower = jnp.where(same & (c <= r), 1.0, 0.0).astype(BF16)
    upper = jnp.where(same & (c > r), 1.0, 0.0).astype(BF16)
    nck = cb // GLA_CHUNK
    sr = lax.broadcasted_iota(jnp.int32, (8, cb), 0)
    sc = lax.broadcasted_iota(jnp.int32, (8, cb), 1)
    chunk_sum = jnp.where((sc // GLA_CHUNK) == sr, 1.0, 0.0).astype(BF16)

    gq = z[:, OFF_GQ:OFF_GQ + GLA_KEY_WIDTH] * (GLA_DK ** -0.5)
    gk = z[:, OFF_GK:OFF_GK + GLA_KEY_WIDTH]
    for blk in range(tm // cb):
        rows = slice(blk * cb, (blk + 1) * cb)
        la = log_a[rows]
        b = _split_dot(lower, la)
        rest = _split_dot(upper, la)
        tot = _split_dot(chunk_sum, la)
        qd_ref[rows, :] = (gq[rows] * jnp.exp(b)).astype(BF16)
        ki_ref[rows, :] = (gk[rows] * jnp.exp(-b)).astype(BF16)
        ke_ref[rows, :] = (gk[rows] * jnp.exp(rest)).astype(BF16)
        dec_ref[blk * nck:(blk + 1) * nck, :] = jnp.exp(tot[:nck])
    yield

    cos = cos_ref[...]
    sin = sin_ref[...]
    lane = lax.broadcasted_iota(jnp.int32, (tm, LANES), 1)
    low64 = lane < MLA_ROPE

    scale = MLA_QK ** -0.5 * LOG2E
    one_col = jnp.where(lane == MLA_ROPE, 1.0, 0.0)
    nope_w = MLA_HEADS * MLA_NOPE
    for pair in range(MLA_HEADS // 2):
        t = qraw[:, nope_w + pair * LANES:nope_w + (pair + 1) * LANES]
        t2 = t * t
        ssq_lo = jnp.sum(jnp.where(low64, t2, 0.0), axis=-1, keepdims=True)
        ssq_hi = jnp.sum(jnp.where(low64, 0.0, t2), axis=-1, keepdims=True)
        tg = t * qgr_ref[...]
        rot = tg * cos + _swap_halves(tg) * sin
        for sub in range(2):
            hd = 2 * pair + sub
            qn = qraw[:, hd * MLA_NOPE:(hd + 1) * MLA_NOPE]
            ssq = jnp.sum(qn * qn, axis=-1, keepdims=True) + (ssq_lo if sub == 0 else ssq_hi)
            rinv = lax.rsqrt(ssq * (1.0 / MLA_QK) + EPS) * scale
            q_ref[hd, :, 0:LANES] = (qn * rinv * qgn_ref[...]).astype(BF16)
            rsel = rot if sub == 0 else pltpu.roll(rot, MLA_ROPE, axis=1)
            q_ref[hd, :, LANES:QK_PAD] = jnp.where(low64, rsel * rinv, one_col).astype(BF16)

    kpe2 = misc * misc
    ssq_pe = jnp.sum(jnp.where(low64, kpe2, 0.0), axis=-1, keepdims=True)
    kg = misc * kgr_ref[...]
    krot = kg * cos + _swap_halves(kg) * sin
    for hd in range(MLA_HEADS):
        kn = kvraw[:, hd * MLA_NOPE:(hd + 1) * MLA_NOPE]
        ssq = jnp.sum(kn * kn, axis=-1, keepdims=True) + ssq_pe
        rinv = lax.rsqrt(ssq * (1.0 / MLA_QK) + EPS)
        k_ref[hd, :, 0:LANES] = (kn * rinv * kgn_ref[...]).astype(BF16)
        k_ref[hd, :, LANES:QK_PAD] = (krot * rinv + kbias_ref[...]).astype(BF16)
        v_ref[hd, :, :] = kvraw[:, nope_w + hd * MLA_DV:nope_w + (hd + 1) * MLA_DV].astype(BF16)


def _gla_kernel(qd_ref, ki_ref, ke_ref, v_ref, dec_ref, gate_ref, g_ref, o_ref, state_ref, *, steps_per_seq):
    @pl.when(pl.program_id(0) % steps_per_seq == 0)
    def _():
        state_ref[...] = jnp.zeros_like(state_ref)

    for _ in _gla_chunks(qd_ref, ki_ref, ke_ref, v_ref, dec_ref, gate_ref, g_ref, o_ref, state_ref):
        pass


def _gla_chunks(qd_ref, ki_ref, ke_ref, v_ref, dec_ref, gate_ref, g_ref, o_ref, state_ref):
    tm = qd_ref.shape[0]

    C = GLA_CHUNK
    klane_head = lax.broadcasted_iota(jnp.int32, (C, GLA_KEY_WIDTH), 1) // GLA_DK
    vlane_head = lax.broadcasted_iota(jnp.int32, (C, GLA_WIDTH), 1) // GLA_DV
    row = lax.broadcasted_iota(jnp.int32, (C, GLA_KEY_WIDTH), 0)
    col_in_head = lax.broadcasted_iota(jnp.int32, (C, GLA_KEY_WIDTH), 1) % GLA_DK
    causal = row >= col_in_head
    pair_rows = 2 * GLA_DV
    own = ((lax.broadcasted_iota(jnp.int32, (pair_rows, LANES), 0) // GLA_DV)
           == (lax.broadcasted_iota(jnp.int32, (pair_rows, LANES), 1) // GLA_DK))
    nt = (((1,), (1,)), ((), ()))
    tn = (((0,), (0,)), ((), ()))
    g = g_ref[...]
    n_pairs = GLA_HEADS // 2

    for ck in range(tm // C):
        rows = slice(ck * C, (ck + 1) * C)
        qd = qd_ref[rows, :]
        ki = ki_ref[rows, :]
        ke = ke_ref[rows, :]
        v = v_ref[rows, :]
        upd = [lax.dot_general(v[:, p * pair_rows:(p + 1) * pair_rows], ke[:, p * LANES:(p + 1) * LANES],
                               tn, preferred_element_type=F32) for p in range(n_pairs)]
        zk = jnp.zeros_like(ki)
        zv = jnp.zeros_like(v)
        kstack = jnp.concatenate([jnp.where(klane_head == hd, ki, zk) for hd in range(GLA_HEADS)], axis=0)
        vstack = jnp.concatenate([jnp.where(vlane_head == hd, v, zv) for hd in range(GLA_HEADS)], axis=0)
        a = lax.dot_general(qd, kstack, nt, preferred_element_type=F32)
        a = jnp.where(causal, a, 0.0).astype(BF16)
        o = jnp.dot(a, vstack, preferred_element_type=F32)
        state = state_ref[...]
        o = o + jnp.concatenate(
            [lax.dot_general(qd[:, p * LANES:(p + 1) * LANES],
                             state[p * pair_rows:(p + 1) * pair_rows].astype(BF16),
                             nt, preferred_element_type=F32) for p in range(n_pairs)], axis=1)
        for p in range(n_pairs):
            srows = slice(p * pair_rows, (p + 1) * pair_rows)
            state_ref[srows, :] = (state[srows] * dec_ref[ck:ck + 1, p * LANES:(p + 1) * LANES]
                                   + jnp.where(own, upd[p], 0.0))
        for hd in range(GLA_HEADS):
            cols = slice(hd * GLA_DV, (hd + 1) * GLA_DV)
            oh = o[:, cols]
            on = oh * lax.rsqrt(jnp.mean(oh * oh, axis=-1, keepdims=True) + EPS) * g
            o_ref[rows, cols] = (on * gate_ref[rows, cols].astype(F32)).astype(BF16)
        yield


def _attn_kernel(q_ref, k_ref, v_ref, gate_ref, o_ref, m_ref, l_ref, acc_ref):
    tq = q_ref.shape[0]
    tk = ATTN_TK
    i = pl.program_id(2)
    q = q_ref[...]
    nt = (((1,), (1,)), ((), ()))
    m_ref[...] = jnp.full_like(m_ref, -jnp.inf)
    l_ref[...] = jnp.zeros_like(l_ref)
    acc_ref[...] = jnp.zeros_like(acc_ref)

    def step(j, masked):
        start = pl.multiple_of(j * tk, tk)
        k = k_ref[pl.ds(start, tk), :]
        v = v_ref[pl.ds(start, tk), :]
        s = lax.dot_general(q, k, nt, preferred_element_type=F32)
        if masked:
            rr = lax.broadcasted_iota(jnp.int32, s.shape, 0)
            cc = lax.broadcasted_iota(jnp.int32, s.shape, 1)
            s = jnp.where(cc <= rr, s, -jnp.inf)
        m_prev = m_ref[...]
        m_new = jnp.maximum(m_prev, jnp.max(s, axis=-1, keepdims=True))
        alpha = jnp.exp2(m_prev - m_new)
        p = jnp.exp2(s - m_new)
        l_ref[...] = alpha * l_ref[...] + jnp.sum(p, axis=-1, keepdims=True)
        acc_ref[...] = alpha * acc_ref[...] + jnp.dot(p.astype(BF16), v, preferred_element_type=F32)
        m_ref[...] = m_new

    def body(j, carry):
        step(j, False)
        return carry

    lax.fori_loop(0, i * (tq // tk), body, 0)
    step(i * (tq // tk), True)
    o = acc_ref[...] / l_ref[...]
    o_ref[...] = (o * gate_ref[...].astype(F32)).astype(BF16)


def _attn_bounded_kernel(q_ref, k_ref, v_ref, gate_ref, o_ref):
    t = ATTN_TQ
    half = t // 2
    n_tiles = k_ref.shape[0] // t
    nt = (((1,), (1,)), ((), ()))
    ones_col = jnp.where(lax.broadcasted_iota(jnp.int32, (t, LANES), 1) == 0, 1.0, 0.0).astype(BF16)
    causal_a = (lax.broadcasted_iota(jnp.int32, (half, half), 1)
                <= lax.broadcasted_iota(jnp.int32, (half, half), 0))
    causal_b = (lax.broadcasted_iota(jnp.int32, (half, t), 1)
                <= lax.broadcasted_iota(jnp.int32, (half, t), 0) + half)

    def probs(q, k):
        return jnp.exp2(lax.dot_general(q, k, nt, preferred_element_type=F32))

    def key_tile(q, j):
        k = k_ref[j * t:(j + 1) * t, :]
        v_aug = jnp.concatenate([v_ref[j * t:(j + 1) * t, :], ones_col], axis=1)
        return jnp.dot(probs(q, k).astype(BF16), v_aug, preferred_element_type=F32)

    def diagonal_tile(q, j):
        k = k_ref[j * t:(j + 1) * t, :]
        v_aug = jnp.concatenate([v_ref[j * t:(j + 1) * t, :], ones_col], axis=1)
        p_a = jnp.where(causal_a, probs(q[:half], k[:half]), 0.0).astype(BF16)
        p_b = jnp.where(causal_b, probs(q[half:], k), 0.0).astype(BF16)
        return jnp.concatenate([jnp.dot(p_a, v_aug[:half], preferred_element_type=F32),
                                jnp.dot(p_b, v_aug, preferred_element_type=F32)], axis=0)

    for idx in range(n_tiles):
        rows = slice(idx * t, (idx + 1) * t)
        q = q_ref[rows, :]
        acc = diagonal_tile(q, idx)
        for j in range(idx):
            acc = acc + key_tile(q, j)
        o = acc[:, :MLA_DV] * (1.0 / acc[:, MLA_DV:MLA_DV + 1])
        o_ref[rows, :] = (o * gate_ref[rows, :].astype(F32)).astype(BF16)


def _out_kernel(x_ref, og_ref, om_ref, w_ref, o_ref):
    w = w_ref[...]
    o_ref[...] = (x_ref[...]
                  + jnp.dot(og_ref[...], w[:GLA_WIDTH], preferred_element_type=F32)
                  + jnp.dot(om_ref[...], w[GLA_WIDTH:], preferred_element_type=F32))


def _full(shape):
    return pl.BlockSpec(shape, lambda *idx: (0,) * len(shape), pipeline_mode=pl.Buffered(1))


def _params(n_axes):
    return pltpu.CompilerParams(dimension_semantics=("arbitrary",) * n_axes,
                                vmem_limit_bytes=VMEM_LIMIT)


def _proj_call(x2, prev, cos_t, sin_t, lw, B, S):
    T = x2.shape[0]
    tm = PROJ_TM
    ns = S // tm
    n = T // tm
    fuse_out = prev is not None
    cur = lambda i: jnp.minimum(i, n - 1)
    lag = lambda i: jnp.maximum(i - 1, 0)
    rows = lambda w, at: pl.BlockSpec((tm, w), lambda i: (at(i), 0))
    heads = lambda w: pl.BlockSpec((None, MLA_HEADS, tm, w), lambda i: (lag(i) // ns, 0, lag(i) % ns, 0))
    front_shapes = (
        jax.ShapeDtypeStruct((T, GLA_WIDTH), BF16),
        jax.ShapeDtypeStruct((T, GLA_WIDTH), BF16),
        jax.ShapeDtypeStruct((T, MLA_WIDTH), BF16),
    )
    front_out_specs = (rows(GLA_WIDTH, cur), rows(GLA_WIDTH, cur), rows(MLA_WIDTH, cur))
    post_shapes = (
        jax.ShapeDtypeStruct((T, GLA_KEY_WIDTH), BF16),
        jax.ShapeDtypeStruct((T, GLA_KEY_WIDTH), BF16),
        jax.ShapeDtypeStruct((T, GLA_KEY_WIDTH), BF16),
        jax.ShapeDtypeStruct((T // GLA_CHUNK, GLA_KEY_WIDTH), F32),
        jax.ShapeDtypeStruct((B, MLA_HEADS, S, QK_PAD), BF16),
        jax.ShapeDtypeStruct((B, MLA_HEADS, S, QK_PAD), BF16),
        jax.ShapeDtypeStruct((B, MLA_HEADS, S, MLA_DV), BF16),
    )
    post_specs = (rows(GLA_KEY_WIDTH, lag), rows(GLA_KEY_WIDTH, lag), rows(GLA_KEY_WIDTH, lag),
                  pl.BlockSpec((tm // GLA_CHUNK, GLA_KEY_WIDTH), lambda i: (lag(i), 0)),
                  heads(QK_PAD), heads(QK_PAD), heads(MLA_DV))
    assert len(post_specs) == N_POST_OUT

    front_specs = [rows(D_MODEL, cur)]
    front_args = [x2]
    if fuse_out:
        o_gla, o_mla, w_out = prev
        front_specs += [rows(GLA_WIDTH, cur), rows(MLA_WIDTH, cur), _full((D_MODEL, D_MODEL))]
        front_args += [o_gla, o_mla, w_out]
    front_specs += [_full((1, D_MODEL)), _full((D_IN_PAD, D_MODEL))]
    front_args += [lw["norm_g"], lw["w_in"]]
    post_in_specs = [rows(LANES, lag), rows(LANES, lag),
                     _full((LANES, GLA_KEY_WIDTH)), _full((1, GLA_KEY_WIDTH)),
                     _full((1, MLA_Q_RANK)), _full((MLA_Q_RANK, MLA_HEADS * MLA_QK)),
                     _full((1, MLA_KV_RANK)), _full((MLA_KV_RANK, MLA_HEADS * (MLA_NOPE + MLA_DV))),
                     _full((1, LANES)), _full((1, LANES)), _full((1, LANES)), _full((1, LANES)),
                     _full((1, LANES))]
    post_in_args = [cos_t, sin_t, lw["w_misc"], lw["b_gate"], lw["q_norm_g"], lw["w_uq"],
                    lw["kv_norm_g"], lw["w_ukv"], lw["qg_nope"], lw["qg_rope"], lw["kg_nope"],
                    lw["kg_rope"], lw["k_bias"]]
    assert len(post_in_specs) == N_POST_IN

    out_shape = front_shapes + post_shapes
    out_specs = front_out_specs + post_specs
    if fuse_out:
        out_shape = (jax.ShapeDtypeStruct((T, D_MODEL), F32),) + out_shape
        out_specs = (rows(D_MODEL, cur),) + out_specs
    return pl.pallas_call(
        functools.partial(_proj_kernel, fuse_out=fuse_out),
        grid=(n + 1,), in_specs=front_specs + post_in_specs, out_specs=out_specs, out_shape=out_shape,
        scratch_shapes=[pltpu.VMEM((tm, D_IN_PAD), F32), pltpu.VMEM((tm, D_IN_PAD), F32)],
        compiler_params=_params(1), name="outproj_proj" if fuse_out else "proj",
    )(*front_args, *post_in_args)


def _gla_call(qd, ki, ke, gv, dec, gg, gla_g, S):
    T = qd.shape[0]
    tm = GLA_TM
    rows = lambda w: pl.BlockSpec((tm, w), lambda i: (i, 0))
    return pl.pallas_call(
        functools.partial(_gla_kernel, steps_per_seq=S // tm),
        grid=(T // tm,),
        in_specs=[rows(GLA_KEY_WIDTH), rows(GLA_KEY_WIDTH), rows(GLA_KEY_WIDTH), rows(GLA_WIDTH),
                  pl.BlockSpec((tm // GLA_CHUNK, GLA_KEY_WIDTH), lambda i: (i, 0)),
                  rows(GLA_WIDTH), _full((1, GLA_DV))],
        out_specs=rows(GLA_WIDTH),
        out_shape=jax.ShapeDtypeStruct((T, GLA_WIDTH), BF16),
        scratch_shapes=[pltpu.VMEM((GLA_WIDTH, LANES), F32)],
        compiler_params=_params(1), name="gla",
    )(qd, ki, ke, gv, dec, gg, gla_g)


def _attn_call(q, k, v, mg):
    B, H, S, _ = q.shape
    tq = ATTN_TQ
    nq = S // tq
    q_spec = pl.BlockSpec((None, None, tq, QK_PAD), lambda b, h, i: (b, h, i, 0))
    kv_specs = [pl.BlockSpec((None, None, S, QK_PAD), lambda b, h, i: (b, h, 0, 0)),
                pl.BlockSpec((None, None, S, MLA_DV), lambda b, h, i: (b, h, 0, 0))]
    io_spec = pl.BlockSpec((tq, MLA_DV), lambda b, h, i: (b * nq + i, h))
    return pl.pallas_call(
        _attn_kernel, grid=(B, H, nq),
        in_specs=[q_spec] + kv_specs + [io_spec], out_specs=io_spec,
        out_shape=jax.ShapeDtypeStruct(mg.shape, BF16),
        scratch_shapes=[pltpu.VMEM((tq, 1), F32), pltpu.VMEM((tq, 1), F32), pltpu.VMEM((tq, MLA_DV), F32)],
        compiler_params=_params(3), name="attn_online",
    )(q, k, v, mg)


def _attn_bounded_call(q, k, v, mg):
    B, H, S, _ = q.shape
    seq = lambda w: pl.BlockSpec((None, None, S, w), lambda b, h: (b, h, 0, 0))
    io_spec = pl.BlockSpec((S, MLA_DV), lambda b, h: (b, h))
    return pl.pallas_call(
        _attn_bounded_kernel, grid=(B, H),
        in_specs=[seq(QK_PAD), seq(QK_PAD), seq(MLA_DV), io_spec], out_specs=io_spec,
        out_shape=jax.ShapeDtypeStruct(mg.shape, BF16),
        compiler_params=_params(2), name="attn_bounded",
    )(q, k, v, mg)


def _out_call(x2, og, om, w_out):
    T = x2.shape[0]
    tm = OUT_TM
    rows = lambda w: pl.BlockSpec((tm, w), lambda i: (i, 0))
    return pl.pallas_call(
        _out_kernel, grid=(T // tm,),
        in_specs=[rows(D_MODEL), rows(GLA_WIDTH), rows(MLA_WIDTH), _full((D_MODEL, D_MODEL))],
        out_specs=rows(D_MODEL),
        out_shape=jax.ShapeDtypeStruct((T, D_MODEL), F32),
        compiler_params=_params(1), name="outproj",
    )(x2, og, om, w_out)


def _prep_weights(norm_g, w_in, w_gla_gate_up, b_gla_gate, gla_norm_g, mla_q_norm_g, w_uq,
                  mla_kv_norm_g, w_ukv, q_head_g, k_head_g, w_out):
    depth = w_in.shape[0]
    o = [0]
    for wdt in (GLA_KEY_WIDTH, GLA_KEY_WIDTH, GLA_WIDTH, GLA_GATE_RANK, GLA_WIDTH,
                MLA_Q_RANK, MLA_KV_RANK, MLA_ROPE, MLA_WIDTH):
        o.append(o[-1] + wdt)
    wt = jnp.swapaxes(w_in, 1, 2).astype(BF16)
    pad = jnp.zeros((depth, LANES - MLA_ROPE - GLA_GATE_RANK, D_MODEL), BF16)
    w_in_p = jnp.concatenate([wt[:, :o[3]], wt[:, o[4]:o[8]], wt[:, o[3]:o[4]], pad, wt[:, o[8]:]], axis=1)
    assert w_in_p.shape[1] == D_IN_PAD and o[3] == OFF_GGATE and o[8] - o[4] + o[3] == OFF_MISC + MLA_ROPE

    w_misc = jnp.zeros((depth, LANES, GLA_KEY_WIDTH), F32)
    w_misc = w_misc.at[:, MLA_ROPE:MLA_ROPE + GLA_GATE_RANK, :].set(w_gla_gate_up).astype(BF16)

    wq = w_uq.reshape(depth, MLA_Q_RANK, MLA_HEADS, MLA_QK)
    w_uq_p = jnp.concatenate([wq[..., :MLA_NOPE].reshape(depth, MLA_Q_RANK, -1),
                              wq[..., MLA_NOPE:].reshape(depth, MLA_Q_RANK, -1)], axis=-1).astype(BF16)
    wkv = w_ukv.reshape(depth, MLA_KV_RANK, MLA_HEADS, MLA_NOPE + MLA_DV)
    w_ukv_p = jnp.concatenate([wkv[..., :MLA_NOPE].reshape(depth, MLA_KV_RANK, -1),
                               wkv[..., MLA_NOPE:].reshape(depth, MLA_KV_RANK, -1)], axis=-1).astype(BF16)

    zeros64 = jnp.zeros((depth, 1, LANES - MLA_ROPE), F32)
    row = lambda a: a[:, None, :]
    return dict(
        norm_g=row(norm_g), w_in=w_in_p, w_misc=w_misc, b_gate=row(b_gla_gate),
        gla_g=row(gla_norm_g), q_norm_g=row(mla_q_norm_g), w_uq=w_uq_p,
        kv_norm_g=row(mla_kv_norm_g), w_ukv=w_ukv_p,
        qg_nope=row(q_head_g[:, :MLA_NOPE]),
        qg_rope=jnp.concatenate([row(q_head_g[:, MLA_NOPE:])] * 2, axis=-1),
        kg_nope=row(k_head_g[:, :MLA_NOPE]),
        kg_rope=jnp.concatenate([row(k_head_g[:, MLA_NOPE:]), zeros64], axis=-1),
        w_out=w_out.astype(BF16),
    )


def _score_bounds(q_head_g, k_head_g):
    bound = (MLA_QK ** 0.5 * LOG2E * SCORE_BOUND_SLACK
             * jnp.max(jnp.abs(q_head_g), axis=-1) * jnp.max(jnp.abs(k_head_g), axis=-1))
    use = bound <= SCORE_BOUND_MAX_LOG2
    depth = q_head_g.shape[0]
    bias = jnp.zeros((depth, 1, LANES), F32).at[:, 0, MLA_ROPE].set(jnp.where(use, -bound, 0.0))
    return use, bias


def _rope_tables(positions):
    inv_freq = ROPE_THETA ** (-jnp.arange(0, MLA_ROPE, 2, dtype=F32) / MLA_ROPE)
    reps = LANES // inv_freq.shape[0]
    inv_freq = jnp.tile(inv_freq, reps)
    sign = jnp.tile(jnp.repeat(jnp.array([-1.0, 1.0], F32), MLA_ROPE // 2), LANES // MLA_ROPE)
    ang = positions.astype(F32).reshape(-1, 1) * inv_freq
    return jnp.cos(ang), jnp.sin(ang) * sign


def kernel(x, positions, norm_g, w_in, w_gla_gate_up, b_gla_gate, gla_norm_g, mla_q_norm_g, w_uq,
           mla_kv_norm_g, w_ukv, q_head_g, k_head_g, w_out):
    B, S, D = x.shape
    assert D == D_MODEL and S % max(PROJ_TM, GLA_TM, ATTN_TQ, OUT_TM) == 0 and ATTN_TQ % ATTN_TK == 0
    depth = w_in.shape[0]
    weights = _prep_weights(norm_g, w_in, w_gla_gate_up, b_gla_gate, gla_norm_g, mla_q_norm_g, w_uq,
                            mla_kv_norm_g, w_ukv, q_head_g, k_head_g, w_out)
    use_bounded, weights["k_bias"] = _score_bounds(q_head_g, k_head_g)
    cos_t, sin_t = _rope_tables(positions)
    x2 = x.reshape(B * S, D)
    prev = None
    for l in range(depth):
        lw = {name: w[l] for name, w in weights.items()}
        outs = _proj_call(x2, prev, cos_t, sin_t, lw, B, S)
        if prev is not None:
            x2, outs = outs[0], outs[1:]
        gv, gg, mg, qd, ki, ke, dec, q, k, v = outs
        o_gla = _gla_call(qd, ki, ke, gv, dec, gg, lw["gla_g"], S)
        o_mla = lax.cond(use_bounded[l], _attn_bounded_call, _attn_call, q, k, v, mg)
        prev = (o_gla, o_mla, lw["w_out"])
    x2 = _out_call(x2, *prev)
    return x2.reshape(B, S, D)
```

```python
import functools

import jax
import jax.numpy as jnp
from jax import lax
from jax.experimental import pallas as pl
from jax.experimental.pallas import tpu as pltpu

F32 = jnp.float32
BF16 = jnp.bfloat16

D_MODEL = 1024
GLA_HEADS = 4
GLA_DK = 64
GLA_DV = 128
GLA_KEY_WIDTH = GLA_HEADS * GLA_DK
GLA_WIDTH = GLA_HEADS * GLA_DV
GLA_GATE_RANK = 16
GLA_GATE_TEMP = 16.0
GLA_CHUNK = 64
MLA_HEADS = 4
MLA_NOPE = 128
MLA_ROPE = 64
MLA_QK = MLA_NOPE + MLA_ROPE
MLA_DV = 128
MLA_WIDTH = MLA_HEADS * MLA_DV
MLA_Q_RANK = 256
MLA_KV_RANK = 128
ROPE_THETA = 10000.0
EPS = 1e-6
LOG2E = 1.4426950408889634
SCORE_BOUND_MAX_LOG2 = 40.0 * LOG2E
SCORE_BOUND_SLACK = 1.02

LANES = 128
QK_PAD = 2 * LANES

OFF_GQ = 0
OFF_GK = 256
OFF_GV = 512
OFF_GGATE = 1024
OFF_CQ = 1536
OFF_CKV = 1792
OFF_MISC = 1920
OFF_MGATE = 2048
D_IN_PAD = 2560

PROJ_TM = 512
GLA_TM = 1024
ATTN_TQ = 512
ATTN_TK = 512
OUT_TM = 1024
CUMSUM_BLOCK = 256
PROJ_CHUNK = 256
VMEM_LIMIT = 56 * 1024 * 1024


def _rms(t, g):
    return t * lax.rsqrt(jnp.mean(t * t, axis=-1, keepdims=True) + EPS) * g


def _silu(t):
    return t * jax.nn.sigmoid(t)


def _split_dot(mat_bf16, t):
    hi = t.astype(BF16)
    lo = (t - hi.astype(F32)).astype(BF16)
    return (jnp.dot(mat_bf16, hi, preferred_element_type=F32)
            + jnp.dot(mat_bf16, lo, preferred_element_type=F32))


def _swap_halves(t):
    lane = lax.broadcasted_iota(jnp.int32, t.shape, 1)
    return jnp.where(lane % MLA_ROPE < MLA_ROPE // 2,
                     pltpu.roll(t, LANES - MLA_ROPE // 2, axis=1),
                     pltpu.roll(t, MLA_ROPE // 2, axis=1))


N_POST_IN = 13
N_POST_OUT = 7


def _proj_kernel(*refs, fuse_out):
    refs = list(refs)
    x_ref = refs.pop(0)
    if fuse_out:
        og_ref, om_ref, wout_ref = refs[:3]
        del refs[:3]
    ng_ref, win_ref = refs[:2]
    post_in = refs[2:2 + N_POST_IN]
    del refs[:2 + N_POST_IN]
    if fuse_out:
        xo_ref = refs.pop(0)
    gv_ref, gg_ref, mg_ref = refs[:3]
    post_out = refs[3:3 + N_POST_OUT]
    z0_ref, z1_ref = refs[3 + N_POST_OUT:]
    i = pl.program_id(0)
    nt = (((1,), (1,)), ((), ()))
    direct = {}
    for ref, off, fn in ((gv_ref, OFF_GV, None), (gg_ref, OFF_GGATE, _silu), (mg_ref, OFF_MGATE, _silu)):
        for k in range(ref.shape[1] // PROJ_CHUNK):
            direct[off // PROJ_CHUNK + k] = (ref, k, fn)

    @pl.when(i == 0)
    def _():
        z1_ref[...] = jnp.zeros_like(z1_ref)

    def step(z_read, z_write):
        post = _proj_post(z_read, *post_in, *post_out)
        x = x_ref[...]
        if fuse_out:
            w = wout_ref[...]
            x = (x + jnp.dot(og_ref[...], w[:GLA_WIDTH], preferred_element_type=F32)
                 + jnp.dot(om_ref[...], w[GLA_WIDTH:], preferred_element_type=F32))
            xo_ref[...] = x
        h = _rms(x, ng_ref[...]).astype(BF16)
        order = sorted(range(D_IN_PAD // PROJ_CHUNK), key=lambda c: (c not in direct, c))
        for pos, c in enumerate(order):
            if pos in (0, len(order) // 2):
                next(post)
            cols = slice(c * PROJ_CHUNK, (c + 1) * PROJ_CHUNK)
            zc = lax.dot_general(h, win_ref[cols, :], nt, preferred_element_type=F32)
            if c in direct:
                ref, k, fn = direct[c]
                ref[:, k * PROJ_CHUNK:(k + 1) * PROJ_CHUNK] = (zc if fn is None else fn(zc)).astype(BF16)
            else:
                z_write[:, cols] = zc
        for _ in post:
            pass

    @pl.when(i % 2 == 0)
    def _():
        step(z1_ref, z0_ref)

    @pl.when(i % 2 == 1)
    def _():
        step(z0_ref, z1_ref)


def _proj_post(z, cos_ref, sin_ref, wmisc_ref, bgate_ref,
               qng_ref, wuq_ref, kvng_ref, wukv_ref, qgn_ref, qgr_ref, kgn_ref, kgr_ref, kbias_ref,
               qd_ref, ki_ref, ke_ref, dec_ref, q_ref, k_ref, v_ref):
    tm = z.shape[0]

    misc = z[:, OFF_MISC:OFF_MISC + LANES]
    logit = jnp.dot(misc.astype(BF16), wmisc_ref[...], preferred_element_type=F32) + bgate_ref[...]
    cqn = _rms(z[:, OFF_CQ:OFF_CQ + MLA_Q_RANK], qng_ref[...]).astype(BF16)
    qraw = jnp.dot(cqn, wuq_ref[...], preferred_element_type=F32)
    ckvn = _rms(z[:, OFF_CKV:OFF_CKV + MLA_KV_RANK], kvng_ref[...]).astype(BF16)
    kvraw = jnp.dot(ckvn, wukv_ref[...], preferred_element_type=F32)
    yield

    log_a = (jnp.minimum(logit, 0.0) - jnp.log(1.0 + jnp.exp(-jnp.abs(logit)))) * (1.0 / GLA_GATE_TEMP)

    cb = CUMSUM_BLOCK
    r = lax.broadcasted_iota(jnp.int32, (cb, cb), 0)
    c = lax.broadcasted_iota(jnp.int32, (cb, cb), 1)
    same = (r // GLA_CHUNK) == (c // GLA_CHUNK)
    lower = jnp.where(same & (c <= r), 1.0, 0.0).astype(BF16)
    upper = jnp.where(same & (c > r), 1.0, 0.0).astype(BF16)
    nck = cb // GLA_CHUNK
    sr = lax.broadcasted_iota(jnp.int32, (8, cb), 0)
    sc = lax.broadcasted_iota(jnp.int32, (8, cb), 1)
    chunk_sum = jnp.where((sc // GLA_CHUNK) == sr, 1.0, 0.0).astype(BF16)

    gq = z[:, OFF_GQ:OFF_GQ + GLA_KEY_WIDTH] * (GLA_DK ** -0.5)
    gk = z[:, OFF_GK:OFF_GK + GLA_KEY_WIDTH]
    for blk in range(tm // cb):
        rows = slice(blk * cb, (blk + 1) * cb)
        la = log_a[rows]
        b = _split_dot(lower, la)
        rest = _split_dot(upper, la)
        tot = _split_dot(chunk_sum, la)
        qd_ref[rows, :] = (gq[rows] * jnp.exp(b)).astype(BF16)
        ki_ref[rows, :] = (gk[rows] * jnp.exp(-b)).astype(BF16)
        ke_ref[rows, :] = (gk[rows] * jnp.exp(rest)).astype(BF16)
        dec_ref[blk * nck:(blk + 1) * nck, :] = jnp.exp(tot[:nck])
    yield

    cos = cos_ref[...]
    sin = sin_ref[...]
    lane = lax.broadcasted_iota(jnp.int32, (tm, LANES), 1)
    low64 = lane < MLA_ROPE

    scale = MLA_QK ** -0.5 * LOG2E
    one_col = jnp.where(lane == MLA_ROPE, 1.0, 0.0)
    nope_w = MLA_HEADS * MLA_NOPE
    for pair in range(MLA_HEADS // 2):
        t = qraw[:, nope_w + pair * LANES:nope_w + (pair + 1) * LANES]
        t2 = t * t
        ssq_lo = jnp.sum(jnp.where(low64, t2, 0.0), axis=-1, keepdims=True)
        ssq_hi = jnp.sum(jnp.where(low64, 0.0, t2), axis=-1, keepdims=True)
        tg = t * qgr_ref[...]
        rot = tg * cos + _swap_halves(tg) * sin
        for sub in range(2):
            hd = 2 * pair + sub
            qn = qraw[:, hd * MLA_NOPE:(hd + 1) * MLA_NOPE]
            ssq = jnp.sum(qn * qn, axis=-1, keepdims=True) + (ssq_lo if sub == 0 else ssq_hi)
            rinv = lax.rsqrt(ssq * (1.0 / MLA_QK) + EPS) * scale
            q_ref[hd, :, 0:LANES] = (qn * rinv * qgn_ref[...]).astype(BF16)
            rsel = rot if sub == 0 else pltpu.roll(rot, MLA_ROPE, axis=1)
            q_ref[hd, :, LANES:QK_PAD] = jnp.where(low64, rsel * rinv, one_col).astype(BF16)

    kpe2 = misc * misc
    ssq_pe = jnp.sum(jnp.where(low64, kpe2, 0.0), axis=-1, keepdims=True)
    kg = misc * kgr_ref[...]
    krot = kg * cos + _swap_halves(kg) * sin
    for hd in range(MLA_HEADS):
        kn = kvraw[:, hd * MLA_NOPE:(hd + 1) * MLA_NOPE]
        ssq = jnp.sum(kn * kn, axis=-1, keepdims=True) + ssq_pe
        rinv = lax.rsqrt(ssq * (1.0 / MLA_QK) + EPS)
        k_ref[hd, :, 0:LANES] = (kn * rinv * kgn_ref[...]).astype(BF16)
        k_ref[hd, :, LANES:QK_PAD] = (krot * rinv + kbias_ref[...]).astype(BF16)
        v_ref[hd, :, :] = kvraw[:, nope_w + hd * MLA_DV:nope_w + (hd + 1) * MLA_DV].astype(BF16)


def _gla_kernel(qd_ref, ki_ref, ke_ref, v_ref, dec_ref, gate_ref, g_ref, o_ref, state_ref, *, steps_per_seq):
    @pl.when(pl.program_id(0) % steps_per_seq == 0)
    def _():
        state_ref[...] = jnp.zeros_like(state_ref)

    for _ in _gla_chunks(qd_ref, ki_ref, ke_ref, v_ref, dec_ref, gate_ref, g_ref, o_ref, state_ref):
        pass


def _gla_chunks(qd_ref, ki_ref, ke_ref, v_ref, dec_ref, gate_ref, g_ref, o_ref, state_ref):
    tm = qd_ref.shape[0]

    C = GLA_CHUNK
    klane_head = lax.broadcasted_iota(jnp.int32, (C, GLA_KEY_WIDTH), 1) // GLA_DK
    vlane_head = lax.broadcasted_iota(jnp.int32, (C, GLA_WIDTH), 1) // GLA_DV
    row = lax.broadcasted_iota(jnp.int32, (C, GLA_KEY_WIDTH), 0)
    col_in_head = lax.broadcasted_iota(jnp.int32, (C, GLA_KEY_WIDTH), 1) % GLA_DK
    causal = row >= col_in_head
    pair_rows = 2 * GLA_DV
    own = ((lax.broadcasted_iota(jnp.int32, (pair_rows, LANES), 0) // GLA_DV)
           == (lax.broadcasted_iota(jnp.int32, (pair_rows, LANES), 1) // GLA_DK))
    nt = (((1,), (1,)), ((), ()))
    tn = (((0,), (0,)), ((), ()))
    g = g_ref[...]
    n_pairs = GLA_HEADS // 2

    for ck in range(tm // C):
        rows = slice(ck * C, (ck + 1) * C)
        qd = qd_ref[rows, :]
        ki = ki_ref[rows, :]
        ke = ke_ref[rows, :]
        v = v_ref[rows, :]
        upd = [lax.dot_general(v[:, p * pair_rows:(p + 1) * pair_rows], ke[:, p * LANES:(p + 1) * LANES],
                               tn, preferred_element_type=F32) for p in range(n_pairs)]
        zk = jnp.zeros_like(ki)
        zv = jnp.zeros_like(v)
        kstack = jnp.concatenate([jnp.where(klane_head == hd, ki, zk) for hd in range(GLA_HEADS)], axis=0)
        vstack = jnp.concatenate([jnp.where(vlane_head == hd, v, zv) for hd in range(GLA_HEADS)], axis=0)
        a = lax.dot_general(qd, kstack, nt, preferred_element_type=F32)
        a = jnp.where(causal, a, 0.0).astype(BF16)
        o = jnp.dot(a, vstack, preferred_element_type=F32)
        state = state_ref[...]
        o = o + jnp.concatenate(
            [lax.dot_general(qd[:, p * LANES:(p + 1) * LANES],
                             state[p * pair_rows:(p + 1) * pair_rows].astype(BF16),
                             nt, preferred_element_type=F32) for p in range(n_pairs)], axis=1)
        for p in range(n_pairs):
            srows = slice(p * pair_rows, (p + 1) * pair_rows)
            state_ref[srows, :] = (state[srows] * dec_ref[ck:ck + 1, p * LANES:(p + 1) * LANES]
                                   + jnp.where(own, upd[p], 0.0))
        for hd in range(GLA_HEADS):
            cols = slice(hd * GLA_DV, (hd + 1) * GLA_DV)
            oh = o[:, cols]
            on = oh * lax.rsqrt(jnp.mean(oh * oh, axis=-1, keepdims=True) + EPS) * g
            o_ref[rows, cols] = (on * gate_ref[rows, cols].astype(F32)).astype(BF16)
        yield


def _attn_kernel(q_ref, k_ref, v_ref, gate_ref, o_ref, m_ref, l_ref, acc_ref):
    tq = q_ref.shape[0]
    tk = ATTN_TK
    i = pl.program_id(2)
    q = q_ref[...]
    nt = (((1,), (1,)), ((), ()))
    m_ref[...] = jnp.full_like(m_ref, -jnp.inf)
    l_ref[...] = jnp.zeros_like(l_ref)
    acc_ref[...] = jnp.zeros_like(acc_ref)

    def step(j, masked):
        start = pl.multiple_of(j * tk, tk)
        k = k_ref[pl.ds(start, tk), :]
        v = v_ref[pl.ds(start, tk), :]
        s = lax.dot_general(q, k, nt, preferred_element_type=F32)
        if masked:
            rr = lax.broadcasted_iota(jnp.int32, s.shape, 0)
            cc = lax.broadcasted_iota(jnp.int32, s.shape, 1)
            s = jnp.where(cc <= rr, s, -jnp.inf)
        m_prev = m_ref[...]
        m_new = jnp.maximum(m_prev, jnp.max(s, axis=-1, keepdims=True))
        alpha = jnp.exp2(m_prev - m_new)
        p = jnp.exp2(s - m_new)
        l_ref[...] = alpha * l_ref[...] + jnp.sum(p, axis=-1, keepdims=True)
        acc_ref[...] = alpha * acc_ref[...] + jnp.dot(p.astype(BF16), v, preferred_element_type=F32)
        m_ref[...] = m_new

    def body(j, carry):
        step(j, False)
        return carry

    lax.fori_loop(0, i * (tq // tk), body, 0)
    step(i * (tq // tk), True)
    o = acc_ref[...] / l_ref[...]
    o_ref[...] = (o * gate_ref[...].astype(F32)).astype(BF16)


def _attn_bounded_kernel(q_ref, k_ref, v_ref, gate_ref, o_ref):
    t = ATTN_TQ
    half = t // 2
    n_tiles = k_ref.shape[0] // t
    nt = (((1,), (1,)), ((), ()))
    ones_col = jnp.where(lax.broadcasted_iota(jnp.int32, (t, LANES), 1) == 0, 1.0, 0.0).astype(BF16)
    causal_a = (lax.broadcasted_iota(jnp.int32, (half, half), 1)
                <= lax.broadcasted_iota(jnp.int32, (half, half), 0))
    causal_b = (lax.broadcasted_iota(jnp.int32, (half, t), 1)
                <= lax.broadcasted_iota(jnp.int32, (half, t), 0) + half)

    def probs(q, k):
        return jnp.exp2(lax.dot_general(q, k, nt, preferred_element_type=F32))

    def key_tile(q, j):
        k = k_ref[j * t:(j + 1) * t, :]
        v_aug = jnp.concatenate([v_ref[j * t:(j + 1) * t, :], ones_col], axis=1)
        return jnp.dot(probs(q, k).astype(BF16), v_aug, preferred_element_type=F32)

    def diagonal_tile(q, j):
        k = k_ref[j * t:(j + 1) * t, :]
        v_aug = jnp.concatenate([v_ref[j * t:(j + 1) * t, :], ones_col], axis=1)
        p_a = jnp.where(causal_a, probs(q[:half], k[:half]), 0.0).astype(BF16)
        p_b = jnp.where(causal_b, probs(q[half:], k), 0.0).astype(BF16)
        return jnp.concatenate([jnp.dot(p_a, v_aug[:half], preferred_element_type=F32),
                                jnp.dot(p_b, v_aug, preferred_element_type=F32)], axis=0)

    for idx in range(n_tiles):
        rows = slice(idx * t, (idx + 1) * t)
        q = q_ref[rows, :]
        acc = diagonal_tile(q, idx)
        for j in range(idx):
            acc = acc + key_tile(q, j)
        o = acc[:, :MLA_DV] * (1.0 / acc[:, MLA_DV:MLA_DV + 1])
        o_ref[rows, :] = (o * gate_ref[rows, :].astype(F32)).astype(BF16)


def _out_kernel(x_ref, og_ref, om_ref, w_ref, o_ref):
    w = w_ref[...]
    o_ref[...] = (x_ref[...]
                  + jnp.dot(og_ref[...], w[:GLA_WIDTH], preferred_element_type=F32)
                  + jnp.dot(om_ref[...], w[GLA_WIDTH:], preferred_element_type=F32))


def _full(shape):
    return pl.BlockSpec(shape, lambda *idx: (0,) * len(shape), pipeline_mode=pl.Buffered(1))


def _params(n_axes):
    return pltpu.CompilerParams(dimension_semantics=("arbitrary",) * n_axes,
                                vmem_limit_bytes=VMEM_LIMIT)


def _proj_call(x2, prev, cos_t, sin_t, lw, B, S):
    T = x2.shape[0]
    tm = PROJ_TM
    ns = S // tm
    n = T // tm
    fuse_out = prev is not None
    cur = lambda i: jnp.minimum(i, n - 1)
    lag = lambda i: jnp.maximum(i - 1, 0)
    rows = lambda w, at: pl.BlockSpec((tm, w), lambda i: (at(i), 0))
    heads = lambda w: pl.BlockSpec((None, MLA_HEADS, tm, w), lambda i: (lag(i) // ns, 0, lag(i) % ns, 0))
    front_shapes = (
        jax.ShapeDtypeStruct((T, GLA_WIDTH), BF16),
        jax.ShapeDtypeStruct((T, GLA_WIDTH), BF16),
        jax.ShapeDtypeStruct((T, MLA_WIDTH), BF16),
    )
    front_out_specs = (rows(GLA_WIDTH, cur), rows(GLA_WIDTH, cur), rows(MLA_WIDTH, cur))
    post_shapes = (
        jax.ShapeDtypeStruct((T, GLA_KEY_WIDTH), BF16),
        jax.ShapeDtypeStruct((T, GLA_KEY_WIDTH), BF16),
        jax.ShapeDtypeStruct((T, GLA_KEY_WIDTH), BF16),
        jax.ShapeDtypeStruct((T // GLA_CHUNK, GLA_KEY_WIDTH), F32),
        jax.ShapeDtypeStruct((B, MLA_HEADS, S, QK_PAD), BF16),
        jax.ShapeDtypeStruct((B, MLA_HEADS, S, QK_PAD), BF16),
        jax.ShapeDtypeStruct((B, MLA_HEADS, S, MLA_DV), BF16),
    )
    post_specs = (rows(GLA_KEY_WIDTH, lag), rows(GLA_KEY_WIDTH, lag), rows(GLA_KEY_WIDTH, lag),
                  pl.BlockSpec((tm // GLA_CHUNK, GLA_KEY_WIDTH), lambda i: (lag(i), 0)),
                  heads(QK_PAD), heads(QK_PAD), heads(MLA_DV))
    assert len(post_specs) == N_POST_OUT

    front_specs = [rows(D_MODEL, cur)]
    front_args = [x2]
    if fuse_out:
        o_gla, o_mla, w_out = prev
        front_specs += [rows(GLA_WIDTH, cur), rows(MLA_WIDTH, cur), _full((D_MODEL, D_MODEL))]
        front_args += [o_gla, o_mla, w_out]
    front_specs += [_full((1, D_MODEL)), _full((D_IN_PAD, D_MODEL))]
    front_args += [lw["norm_g"], lw["w_in"]]
    post_in_specs = [rows(LANES, lag), rows(LANES, lag),
                     _full((LANES, GLA_KEY_WIDTH)), _full((1, GLA_KEY_WIDTH)),
                     _full((1, MLA_Q_RANK)), _full((MLA_Q_RANK, MLA_HEADS * MLA_QK)),
                     _full((1, MLA_KV_RANK)), _full((MLA_KV_RANK, MLA_HEADS * (MLA_NOPE + MLA_DV))),
                     _full((1, LANES)), _full((1, LANES)), _full((1, LANES)), _full((1, LANES)),
                     _full((1, LANES))]
    post_in_args = [cos_t, sin_t, lw["w_misc"], lw["b_gate"], lw["q_norm_g"], lw["w_uq"],
                    lw["kv_norm_g"], lw["w_ukv"], lw["qg_nope"], lw["qg_rope"], lw["kg_nope"],
                    lw["kg_rope"], lw["k_bias"]]
    assert len(post_in_specs) == N_POST_IN

    out_shape = front_shapes + post_shapes
    out_specs = front_out_specs + post_specs
    if fuse_out:
        out_shape = (jax.ShapeDtypeStruct((T, D_MODEL), F32),) + out_shape
        out_specs = (rows(D_MODEL, cur),) + out_specs
    return pl.pallas_call(
        functools.partial(_proj_kernel, fuse_out=fuse_out),
        grid=(n + 1,), in_specs=front_specs + post_in_specs, out_specs=out_specs, out_shape=out_shape,
        scratch_shapes=[pltpu.VMEM((tm, D_IN_PAD), F32), pltpu.VMEM((tm, D_IN_PAD), F32)],
        compiler_params=_params(1), name="outproj_proj" if fuse_out else "proj",
    )(*front_args, *post_in_args)


def _gla_call(qd, ki, ke, gv, dec, gg, gla_g, S):
    T = qd.shape[0]
    tm = GLA_TM
    rows = lambda w: pl.BlockSpec((tm, w), lambda i: (i, 0))
    return pl.pallas_call(
        functools.partial(_gla_kernel, steps_per_seq=S // tm),
        grid=(T // tm,),
        in_specs=[rows(GLA_KEY_WIDTH), rows(GLA_KEY_WIDTH), rows(GLA_KEY_WIDTH), rows(GLA_WIDTH),
                  pl.BlockSpec((tm // GLA_CHUNK, GLA_KEY_WIDTH), lambda i: (i, 0)),
                  rows(GLA_WIDTH), _full((1, GLA_DV))],
        out_specs=rows(GLA_WIDTH),
        out_shape=jax.ShapeDtypeStruct((T, GLA_WIDTH), BF16),
        scratch_shapes=[pltpu.VMEM((GLA_WIDTH, LANES), F32)],
        compiler_params=_params(1), name="gla",
    )(qd, ki, ke, gv, dec, gg, gla_g)


def _attn_call(q, k, v, mg):
    B, H, S, _ = q.shape
    tq = ATTN_TQ
    nq = S // tq
    q_spec = pl.BlockSpec((None, None, tq, QK_PAD), lambda b, h, i: (b, h, i, 0))
    kv_specs = [pl.BlockSpec((None, None, S, QK_PAD), lambda b, h, i: (b, h, 0, 0)),
                pl.BlockSpec((None, None, S, MLA_DV), lambda b, h, i: (b, h, 0, 0))]
    io_spec = pl.BlockSpec((tq, MLA_DV), lambda b, h, i: (b * nq + i, h))
    return pl.pallas_call(
        _attn_kernel, grid=(B, H, nq),
        in_specs=[q_spec] + kv_specs + [io_spec], out_specs=io_spec,
        out_shape=jax.ShapeDtypeStruct(mg.shape, BF16),
        scratch_shapes=[pltpu.VMEM((tq, 1), F32), pltpu.VMEM((tq, 1), F32), pltpu.VMEM((tq, MLA_DV), F32)],
        compiler_params=_params(3), name="attn_online",
    )(q, k, v, mg)


def _attn_bounded_call(q, k, v, mg):
    B, H, S, _ = q.shape
    seq = lambda w: pl.BlockSpec((None, None, S, w), lambda b, h: (b, h, 0, 0))
    io_spec = pl.BlockSpec((S, MLA_DV), lambda b, h: (b, h))
    return pl.pallas_call(
        _attn_bounded_kernel, grid=(B, H),
        in_specs=[seq(QK_PAD), seq(QK_PAD), seq(MLA_DV), io_spec], out_specs=io_spec,
        out_shape=jax.ShapeDtypeStruct(mg.shape, BF16),
        compiler_params=_params(2), name="attn_bounded",
    )(q, k, v, mg)


def _out_call(x2, og, om, w_out):
    T = x2.shape[0]
    tm = OUT_TM
    rows = lambda w: pl.BlockSpec((tm, w), lambda i: (i, 0))
    return pl.pallas_call(
        _out_kernel, grid=(T // tm,),
        in_specs=[rows(D_MODEL), rows(GLA_WIDTH), rows(MLA_WIDTH), _full((D_MODEL, D_MODEL))],
        out_specs=rows(D_MODEL),
        out_shape=jax.ShapeDtypeStruct((T, D_MODEL), F32),
        compiler_params=_params(1), name="outproj",
    )(x2, og, om, w_out)


def _prep_weights(norm_g, w_in, w_gla_gate_up, b_gla_gate, gla_norm_g, mla_q_norm_g, w_uq,
                  mla_kv_norm_g, w_ukv, q_head_g, k_head_g, w_out):
    depth = w_in.shape[0]
    o = [0]
    for wdt in (GLA_KEY_WIDTH, GLA_KEY_WIDTH, GLA_WIDTH, GLA_GATE_RANK, GLA_WIDTH,
                MLA_Q_RANK, MLA_KV_RANK, MLA_ROPE, MLA_WIDTH):
        o.append(o[-1] + wdt)
    wt = jnp.swapaxes(w_in, 1, 2).astype(BF16)
    pad = jnp.zeros((depth, LANES - MLA_ROPE - GLA_GATE_RANK, D_MODEL), BF16)
    w_in_p = jnp.concatenate([wt[:, :o[3]], wt[:, o[4]:o[8]], wt[:, o[3]:o[4]], pad, wt[:, o[8]:]], axis=1)
    assert w_in_p.shape[1] == D_IN_PAD and o[3] == OFF_GGATE and o[8] - o[4] + o[3] == OFF_MISC + MLA_ROPE

    w_misc = jnp.zeros((depth, LANES, GLA_KEY_WIDTH), F32)
    w_misc = w_misc.at[:, MLA_ROPE:MLA_ROPE + GLA_GATE_RANK, :].set(w_gla_gate_up).astype(BF16)

    wq = w_uq.reshape(depth, MLA_Q_RANK, MLA_HEADS, MLA_QK)
    w_uq_p = jnp.concatenate([wq[..., :MLA_NOPE].reshape(depth, MLA_Q_RANK, -1),
                              wq[..., MLA_NOPE:].reshape(depth, MLA_Q_RANK, -1)], axis=-1).astype(BF16)
    wkv = w_ukv.reshape(depth, MLA_KV_RANK, MLA_HEADS, MLA_NOPE + MLA_DV)
    w_ukv_p = jnp.concatenate([wkv[..., :MLA_NOPE].reshape(depth, MLA_KV_RANK, -1),
                               wkv[..., MLA_NOPE:].reshape(depth, MLA_KV_RANK, -1)], axis=-1).astype(BF16)

    zeros64 = jnp.zeros((depth, 1, LANES - MLA_ROPE), F32)
    row = lambda a: a[:, None, :]
    return dict(
        norm_g=row(norm_g), w_in=w_in_p, w_misc=w_misc, b_gate=row(b_gla_gate),
        gla_g=row(gla_norm_g), q_norm_g=row(mla_q_norm_g), w_uq=w_uq_p,
        kv_norm_g=row(mla_kv_norm_g), w_ukv=w_ukv_p,
        qg_nope=row(q_head_g[:, :MLA_NOPE]),
        qg_rope=jnp.concatenate([row(q_head_g[:, MLA_NOPE:])] * 2, axis=-1),
        kg_nope=row(k_head_g[:, :MLA_NOPE]),
        kg_rope=jnp.concatenate([row(k_head_g[:, MLA_NOPE:]), zeros64], axis=-1),
        w_out=w_out.astype(BF16),
    )


def _score_bounds(q_head_g, k_head_g):
    bound = (MLA_QK ** 0.5 * LOG2E * SCORE_BOUND_SLACK
             * jnp.max(jnp.abs(q_head_g), axis=-1) * jnp.max(jnp.abs(k_head_g), axis=-1))
    use = bound <= SCORE_BOUND_MAX_LOG2
    depth = q_head_g.shape[0]
    bias = jnp.zeros((depth, 1, LANES), F32).at[:, 0, MLA_ROPE].set(jnp.where(use, -bound, 0.0))
    return use, bias


def _rope_tables(positions):
    inv_freq = ROPE_THETA ** (-jnp.arange(0, MLA_ROPE, 2, dtype=F32) / MLA_ROPE)
    ang = inv_freq[:, None] * positions.astype(F32).reshape(1, -1)
    cos = jnp.cos(ang).T
    sin = jnp.sin(ang).T
    return (jnp.concatenate([cos, cos, cos, cos], axis=-1),
            jnp.concatenate([-sin, sin, -sin, sin], axis=-1))


def kernel(x, positions, norm_g, w_in, w_gla_gate_up, b_gla_gate, gla_norm_g, mla_q_norm_g, w_uq,
           mla_kv_norm_g, w_ukv, q_head_g, k_head_g, w_out):
    B, S, D = x.shape
    assert D == D_MODEL and S % max(PROJ_TM, GLA_TM, ATTN_TQ, OUT_TM) == 0 and ATTN_TQ % ATTN_TK == 0
    depth = w_in.shape[0]
    weights = _prep_weights(norm_g, w_in, w_gla_gate_up, b_gla_gate, gla_norm_g, mla_q_norm_g, w_uq,
                            mla_kv_norm_g, w_ukv, q_head_g, k_head_g, w_out)
    use_bounded, weights["k_bias"] = _score_bounds(q_head_g, k_head_g)
    cos_t, sin_t = _rope_tables(positions)
    x2 = x.reshape(B * S, D)
    prev = None
    for l in range(depth):
        lw = {name: w[l] for name, w in weights.items()}
        outs = _proj_call(x2, prev, cos_t, sin_t, lw, B, S)
        if prev is not None:
            x2, outs = outs[0], outs[1:]
        gv, gg, mg, qd, ki, ke, dec, q, k, v = outs
        o_gla = _gla_call(qd, ki, ke, gv, dec, gg, lw["gla_g"], S)
        o_mla = lax.cond(use_bounded[l], _attn_bounded_call, _attn_call, q, k, v, mg)
        prev = (o_gla, o_mla, lw["w_out"])
    x2 = _out_call(x2, *prev)
    return x2.reshape(B, S, D)
```

```python
import functools

import jax
import jax.numpy as jnp
from jax import lax
from jax.experimental import pallas as pl
from jax.experimental.pallas import tpu as pltpu

F32 = jnp.float32
BF16 = jnp.bfloat16

D_MODEL = 1024
GLA_HEADS = 4
GLA_DK = 64
GLA_DV = 128
GLA_KEY_WIDTH = GLA_HEADS * GLA_DK
GLA_WIDTH = GLA_HEADS * GLA_DV
GLA_GATE_RANK = 16
GLA_GATE_TEMP = 16.0
GLA_CHUNK = 64
MLA_HEADS = 4
MLA_NOPE = 128
MLA_ROPE = 64
MLA_QK = MLA_NOPE + MLA_ROPE
MLA_DV = 128
MLA_WIDTH = MLA_HEADS * MLA_DV
MLA_Q_RANK = 256
MLA_KV_RANK = 128
ROPE_THETA = 10000.0
EPS = 1e-6
LOG2E = 1.4426950408889634
SCORE_BOUND_MAX_LOG2 = 40.0 * LOG2E
SCORE_BOUND_SLACK = 1.02

LANES = 128
QK_PAD = 2 * LANES

OFF_GQ = 0
OFF_GK = 256
OFF_GV = 512
OFF_GGATE = 1024
OFF_CQ = 1536
OFF_CKV = 1792
OFF_MISC = 1920
OFF_MGATE = 2048
D_IN_PAD = 2560
D_IN = 2512
SRC_ROW_OF_CHUNK = (0, 256, 512, 768, 1040, 1296, 1552, None, 2000, 2256)

PROJ_TM = 512
GLA_TM = 2048
ATTN_TQ = 512
ATTN_TK = 512
OUT_TM = 1024
CUMSUM_BLOCK = 256
PROJ_CHUNK = 256
VMEM_LIMIT = 56 * 1024 * 1024


def _rms(t, g):
    return t * lax.rsqrt(jnp.mean(t * t, axis=-1, keepdims=True) + EPS) * g


def _silu(t):
    return t * jax.nn.sigmoid(t)


def _split_dot(mat_bf16, t):
    hi = t.astype(BF16)
    lo = (t - hi.astype(F32)).astype(BF16)
    return (jnp.dot(mat_bf16, hi, preferred_element_type=F32)
            + jnp.dot(mat_bf16, lo, preferred_element_type=F32))


def _swap_halves(t):
    lane = lax.broadcasted_iota(jnp.int32, t.shape, 1)
    return jnp.where(lane % MLA_ROPE < MLA_ROPE // 2,
                     pltpu.roll(t, LANES - MLA_ROPE // 2, axis=1),
                     pltpu.roll(t, MLA_ROPE // 2, axis=1))


N_POST_IN = 13
N_POST_OUT = 7
POST_STAGE_AT = (0, 5)


def _proj_kernel(*refs, fuse_out):
    refs = list(refs)
    x_ref = refs.pop(0)
    if fuse_out:
        og_ref, om_ref, wout_ref = refs[:3]
        del refs[:3]
    ng_ref, win_ref, wmix_ref = refs[:3]
    post_in = refs[3:3 + N_POST_IN]
    del refs[:3 + N_POST_IN]
    if fuse_out:
        xo_ref = refs.pop(0)
    gv_ref, gg_ref, mg_ref = refs[:3]
    post_out = refs[3:3 + N_POST_OUT]
    z0_ref, z1_ref = refs[3 + N_POST_OUT:]
    i = pl.program_id(0)
    nt = (((1,), (1,)), ((), ()))
    direct = {}
    for ref, off, fn in ((gv_ref, OFF_GV, None), (gg_ref, OFF_GGATE, _silu), (mg_ref, OFF_MGATE, _silu)):
        for k in range(ref.shape[1] // PROJ_CHUNK):
            direct[off // PROJ_CHUNK + k] = (ref, k, fn)

    @pl.when(i == 0)
    def _():
        z1_ref[...] = jnp.zeros_like(z1_ref)

    def step(z_read, z_write):
        post = _proj_post(z_read, *post_in, *post_out)
        x = x_ref[...]
        if fuse_out:
            w = wout_ref[...]
            x = (x + jnp.dot(og_ref[...], w[:GLA_WIDTH], preferred_element_type=F32)
                 + jnp.dot(om_ref[...], w[GLA_WIDTH:], preferred_element_type=F32))
            xo_ref[...] = x
        h = _rms(x, ng_ref[...]).astype(BF16)
        order = sorted(range(D_IN_PAD // PROJ_CHUNK), key=lambda c: (c not in direct, c))
        for pos, c in enumerate(order):
            if pos in POST_STAGE_AT:
                next(post)
            cols = slice(c * PROJ_CHUNK, (c + 1) * PROJ_CHUNK)
            src = SRC_ROW_OF_CHUNK[c]
            w_c = wmix_ref[...] if src is None else win_ref[src:src + PROJ_CHUNK, :]
            zc = lax.dot_general(h, w_c, nt, preferred_element_type=F32)
            if c in direct:
                ref, k, fn = direct[c]
                ref[:, k * PROJ_CHUNK:(k + 1) * PROJ_CHUNK] = (zc if fn is None else fn(zc)).astype(BF16)
            else:
                z_write[:, cols] = zc
        for _ in post:
            pass

    @pl.when(i % 2 == 0)
    def _():
        step(z1_ref, z0_ref)

    @pl.when(i % 2 == 1)
    def _():
        step(z0_ref, z1_ref)


def _proj_post(z, cos_ref, sin_ref, wmisc_ref, bgate_ref,
               qng_ref, wuq_ref, kvng_ref, wukv_ref, qgn_ref, qgr_ref, kgn_ref, kgr_ref, kbias_ref,
               qd_ref, ki_ref, ke_ref, dec_ref, q_ref, k_ref, v_ref):
    tm = z.shape[0]

    misc = z[:, OFF_MISC:OFF_MISC + LANES]
    logit = jnp.dot(misc.astype(BF16), wmisc_ref[...], preferred_element_type=F32) + bgate_ref[...]
    cqn = _rms(z[:, OFF_CQ:OFF_CQ + MLA_Q_RANK], qng_ref[...]).astype(BF16)
    qraw = jnp.dot(cqn, wuq_ref[...], preferred_element_type=F32)
    ckvn = _rms(z[:, OFF_CKV:OFF_CKV + MLA_KV_RANK], kvng_ref[...]).astype(BF16)
    kvraw = jnp.dot(ckvn, wukv_ref[...], preferred_element_type=F32)
    yield

    log_a = (jnp.minimum(logit, 0.0) - jnp.log(1.0 + jnp.exp(-jnp.abs(logit)))) * (1.0 / GLA_GATE_TEMP)

    cb = CUMSUM_BLOCK
    r = lax.broadcasted_iota(jnp.int32, (cb, cb), 0)
    c = lax.broadcasted_iota(jnp.int32, (cb, cb), 1)
    same = (r // GLA_CHUNK) == (c // GLA_CHUNK)
    lower = jnp.where(same & (c <= r), 1.0, 0.0).astype(BF16)
    upper = jnp.where(same & (c > r), 1.0, 0.0).astype(BF16)
    nck = cb // GLA_CHUNK
    sr = lax.broadcasted_iota(jnp.int32, (8, cb), 0)
    sc = lax.broadcasted_iota(jnp.int32, (8, cb), 1)
    chunk_sum = jnp.where((sc // GLA_CHUNK) == sr, 1.0, 0.0).astype(BF16)

    gq = z[:, OFF_GQ:OFF_GQ + GLA_KEY_WIDTH] * (GLA_DK ** -0.5)
    gk = z[:, OFF_GK:OFF_GK + GLA_KEY_WIDTH]
    for blk in range(tm // cb):
        rows = slice(blk * cb, (blk + 1) * cb)
        la = log_a[rows]
        b = _split_dot(lower, la)
        rest = _split_dot(upper, la)
        tot = _split_dot(chunk_sum, la)
        qd_ref[rows, :] = (gq[rows] * jnp.exp(b)).astype(BF16)
        ki_ref[rows, :] = (gk[rows] * jnp.exp(-b)).astype(BF16)
        ke_ref[rows, :] = (gk[rows] * jnp.exp(rest)).astype(BF16)
        dec_ref[blk * nck:(blk + 1) * nck, :] = jnp.exp(tot[:nck])
    yield

    cos = cos_ref[...]
    sin = sin_ref[...]
    lane = lax.broadcasted_iota(jnp.int32, (tm, LANES), 1)
    low64 = lane < MLA_ROPE

    scale = MLA_QK ** -0.5 * LOG2E
    one_col = jnp.where(lane == MLA_ROPE, 1.0, 0.0)
    nope_w = MLA_HEADS * MLA_NOPE
    for pair in range(MLA_HEADS // 2):
        t = qraw[:, nope_w + pair * LANES:nope_w + (pair + 1) * LANES]
        t2 = t * t
        ssq_lo = jnp.sum(jnp.where(low64, t2, 0.0), axis=-1, keepdims=True)
        ssq_hi = jnp.sum(jnp.where(low64, 0.0, t2), axis=-1, keepdims=True)
        tg = t * qgr_ref[...]
        rot = tg * cos + _swap_halves(tg) * sin
        for sub in range(2):
            hd = 2 * pair + sub
            qn = qraw[:, hd * MLA_NOPE:(hd + 1) * MLA_NOPE]
            ssq = jnp.sum(qn * qn, axis=-1, keepdims=True) + (ssq_lo if sub == 0 else ssq_hi)
            rinv = lax.rsqrt(ssq * (1.0 / MLA_QK) + EPS) * scale
            q_ref[hd, :, 0:LANES] = (qn * rinv * qgn_ref[...]).astype(BF16)
            rsel = rot if sub == 0 else pltpu.roll(rot, MLA_ROPE, axis=1)
            q_ref[hd, :, LANES:QK_PAD] = jnp.where(low64, rsel * rinv, one_col).astype(BF16)

    kpe2 = misc * misc
    ssq_pe = jnp.sum(jnp.where(low64, kpe2, 0.0), axis=-1, keepdims=True)
    kg = misc * kgr_ref[...]
    krot = kg * cos + _swap_halves(kg) * sin
    for hd in range(MLA_HEADS):
        kn = kvraw[:, hd * MLA_NOPE:(hd + 1) * MLA_NOPE]
        ssq = jnp.sum(kn * kn, axis=-1, keepdims=True) + ssq_pe
        rinv = lax.rsqrt(ssq * (1.0 / MLA_QK) + EPS)
        k_ref[hd, :, 0:LANES] = (kn * rinv * kgn_ref[...]).astype(BF16)
        k_ref[hd, :, LANES:QK_PAD] = (krot * rinv + kbias_ref[...]).astype(BF16)
        v_ref[hd, :, :] = kvraw[:, nope_w + hd * MLA_DV:nope_w + (hd + 1) * MLA_DV].astype(BF16)


def _gla_kernel(qd_ref, ki_ref, ke_ref, v_ref, dec_ref, gate_ref, g_ref, o_ref, state_ref, *, steps_per_seq):
    @pl.when(pl.program_id(0) % steps_per_seq == 0)
    def _():
        state_ref[...] = jnp.zeros_like(state_ref)

    for _ in _gla_chunks(qd_ref, ki_ref, ke_ref, v_ref, dec_ref, gate_ref, g_ref, o_ref, state_ref):
        pass


def _gla_chunks(qd_ref, ki_ref, ke_ref, v_ref, dec_ref, gate_ref, g_ref, o_ref, state_ref):
    tm = qd_ref.shape[0]

    C = GLA_CHUNK
    klane_head = lax.broadcasted_iota(jnp.int32, (C, GLA_KEY_WIDTH), 1) // GLA_DK
    vlane_head = lax.broadcasted_iota(jnp.int32, (C, GLA_WIDTH), 1) // GLA_DV
    row = lax.broadcasted_iota(jnp.int32, (C, GLA_KEY_WIDTH), 0)
    col_in_head = lax.broadcasted_iota(jnp.int32, (C, GLA_KEY_WIDTH), 1) % GLA_DK
    causal = row >= col_in_head
    pair_rows = 2 * GLA_DV
    own = ((lax.broadcasted_iota(jnp.int32, (pair_rows, LANES), 0) // GLA_DV)
           == (lax.broadcasted_iota(jnp.int32, (pair_rows, LANES), 1) // GLA_DK))
    nt = (((1,), (1,)), ((), ()))
    tn = (((0,), (0,)), ((), ()))
    g = g_ref[...]
    n_pairs = GLA_HEADS // 2

    for ck in range(tm // C):
        rows = slice(ck * C, (ck + 1) * C)
        qd = qd_ref[rows, :]
        ki = ki_ref[rows, :]
        ke = ke_ref[rows, :]
        v = v_ref[rows, :]
        upd = [lax.dot_general(v[:, p * pair_rows:(p + 1) * pair_rows], ke[:, p * LANES:(p + 1) * LANES],
                               tn, preferred_element_type=F32) for p in range(n_pairs)]
        zk = jnp.zeros_like(ki)
        zv = jnp.zeros_like(v)
        kstack = jnp.concatenate([jnp.where(klane_head == hd, ki, zk) for hd in range(GLA_HEADS)], axis=0)
        vstack = jnp.concatenate([jnp.where(vlane_head == hd, v, zv) for hd in range(GLA_HEADS)], axis=0)
        a = lax.dot_general(qd, kstack, nt, preferred_element_type=F32)
        a = jnp.where(causal, a, 0.0).astype(BF16)
        o = jnp.dot(a, vstack, preferred_element_type=F32)
        state = state_ref[...]
        o = o + jnp.concatenate(
            [lax.dot_general(qd[:, p * LANES:(p + 1) * LANES],
                             state[p * pair_rows:(p + 1) * pair_rows].astype(BF16),
                             nt, preferred_element_type=F32) for p in range(n_pairs)], axis=1)
        for p in range(n_pairs):
            srows = slice(p * pair_rows, (p + 1) * pair_rows)
            state_ref[srows, :] = (state[srows] * dec_ref[ck:ck + 1, p * LANES:(p + 1) * LANES]
                                   + jnp.where(own, upd[p], 0.0))
        for hd in range(GLA_HEADS):
            cols = slice(hd * GLA_DV, (hd + 1) * GLA_DV)
            oh = o[:, cols]
            on = oh * lax.rsqrt(jnp.mean(oh * oh, axis=-1, keepdims=True) + EPS) * g
            o_ref[rows, cols] = (on * gate_ref[rows, cols].astype(F32)).astype(BF16)
        yield


def _attn_kernel(q_ref, k_ref, v_ref, gate_ref, o_ref, m_ref, l_ref, acc_ref):
    tq = q_ref.shape[0]
    tk = ATTN_TK
    i = pl.program_id(2)
    q = q_ref[...]
    nt = (((1,), (1,)), ((), ()))
    m_ref[...] = jnp.full_like(m_ref, -jnp.inf)
    l_ref[...] = jnp.zeros_like(l_ref)
    acc_ref[...] = jnp.zeros_like(acc_ref)

    def step(j, masked):
        start = pl.multiple_of(j * tk, tk)
        k = k_ref[pl.ds(start, tk), :]
        v = v_ref[pl.ds(start, tk), :]
        s = lax.dot_general(q, k, nt, preferred_element_type=F32)
        if masked:
            rr = lax.broadcasted_iota(jnp.int32, s.shape, 0)
            cc = lax.broadcasted_iota(jnp.int32, s.shape, 1)
            s = jnp.where(cc <= rr, s, -jnp.inf)
        m_prev = m_ref[...]
        m_new = jnp.maximum(m_prev, jnp.max(s, axis=-1, keepdims=True))
        alpha = jnp.exp2(m_prev - m_new)
        p = jnp.exp2(s - m_new)
        l_ref[...] = alpha * l_ref[...] + jnp.sum(p, axis=-1, keepdims=True)
        acc_ref[...] = alpha * acc_ref[...] + jnp.dot(p.astype(BF16), v, preferred_element_type=F32)
        m_ref[...] = m_new

    def body(j, carry):
        step(j, False)
        return carry

    lax.fori_loop(0, i * (tq // tk), body, 0)
    step(i * (tq // tk), True)
    o = acc_ref[...] / l_ref[...]
    o_ref[...] = (o * gate_ref[...].astype(F32)).astype(BF16)


def _attn_bounded_kernel(q_ref, k_ref, v_ref, gate_ref, o_ref):
    t = ATTN_TQ
    half = t // 2
    n_tiles = k_ref.shape[0] // t
    nt = (((1,), (1,)), ((), ()))
    ones_col = jnp.where(lax.broadcasted_iota(jnp.int32, (t, LANES), 1) == 0, 1.0, 0.0).astype(BF16)
    causal_a = (lax.broadcasted_iota(jnp.int32, (half, half), 1)
                <= lax.broadcasted_iota(jnp.int32, (half, half), 0))
    causal_b = (lax.broadcasted_iota(jnp.int32, (half, t), 1)
                <= lax.broadcasted_iota(jnp.int32, (half, t), 0) + half)

    def probs(q, k):
        return jnp.exp2(lax.dot_general(q, k, nt, preferred_element_type=F32))

    def key_tile(q, j):
        k = k_ref[j * t:(j + 1) * t, :]
        v_aug = jnp.concatenate([v_ref[j * t:(j + 1) * t, :], ones_col], axis=1)
        return jnp.dot(probs(q, k).astype(BF16), v_aug, preferred_element_type=F32)

    def diagonal_tile(q, j):
        k = k_ref[j * t:(j + 1) * t, :]
        v_aug = jnp.concatenate([v_ref[j * t:(j + 1) * t, :], ones_col], axis=1)
        p_a = jnp.where(causal_a, probs(q[:half], k[:half]), 0.0).astype(BF16)
        p_b = jnp.where(causal_b, probs(q[half:], k), 0.0).astype(BF16)
        return jnp.concatenate([jnp.dot(p_a, v_aug[:half], preferred_element_type=F32),
                                jnp.dot(p_b, v_aug, preferred_element_type=F32)], axis=0)

    for idx in range(n_tiles):
        rows = slice(idx * t, (idx + 1) * t)
        q = q_ref[rows, :]
        acc = diagonal_tile(q, idx)
        for j in range(idx):
            acc = acc + key_tile(q, j)
        o = acc[:, :MLA_DV] * (1.0 / acc[:, MLA_DV:MLA_DV + 1])
        o_ref[rows, :] = (o * gate_ref[rows, :].astype(F32)).astype(BF16)


def _out_kernel(x_ref, og_ref, om_ref, w_ref, o_ref):
    w = w_ref[...]
    o_ref[...] = (x_ref[...]
                  + jnp.dot(og_ref[...], w[:GLA_WIDTH], preferred_element_type=F32)
                  + jnp.dot(om_ref[...], w[GLA_WIDTH:], preferred_element_type=F32))


def _layer(shape, l):
    return pl.BlockSpec((None,) + tuple(shape), lambda *idx: (l,) + (0,) * len(shape),
                        pipeline_mode=pl.Buffered(1))


def _params(n_axes):
    return pltpu.CompilerParams(dimension_semantics=("arbitrary",) * n_axes,
                                vmem_limit_bytes=VMEM_LIMIT)


def _proj_call(x2, prev, cos_t, sin_t, w, l, B, S):
    T = x2.shape[0]
    tm = PROJ_TM
    ns = S // tm
    n = T // tm
    fuse_out = prev is not None
    cur = lambda i: jnp.minimum(i, n - 1)
    lag = lambda i: jnp.maximum(i - 1, 0)
    rows = lambda w, at: pl.BlockSpec((tm, w), lambda i: (at(i), 0))
    heads = lambda w: pl.BlockSpec((None, MLA_HEADS, tm, w), lambda i: (lag(i) // ns, 0, lag(i) % ns, 0))
    front_shapes = (
        jax.ShapeDtypeStruct((T, GLA_WIDTH), BF16),
        jax.ShapeDtypeStruct((T, GLA_WIDTH), BF16),
        jax.ShapeDtypeStruct((T, MLA_WIDTH), BF16),
    )
    front_out_specs = (rows(GLA_WIDTH, cur), rows(GLA_WIDTH, cur), rows(MLA_WIDTH, cur))
    post_shapes = (
        jax.ShapeDtypeStruct((T, GLA_KEY_WIDTH), BF16),
        jax.ShapeDtypeStruct((T, GLA_KEY_WIDTH), BF16),
        jax.ShapeDtypeStruct((T, GLA_KEY_WIDTH), BF16),
        jax.ShapeDtypeStruct((T // GLA_CHUNK, GLA_KEY_WIDTH), F32),
        jax.ShapeDtypeStruct((B, MLA_HEADS, S, QK_PAD), BF16),
        jax.ShapeDtypeStruct((B, MLA_HEADS, S, QK_PAD), BF16),
        jax.ShapeDtypeStruct((B, MLA_HEADS, S, MLA_DV), BF16),
    )
    post_specs = (rows(GLA_KEY_WIDTH, lag), rows(GLA_KEY_WIDTH, lag), rows(GLA_KEY_WIDTH, lag),
                  pl.BlockSpec((tm // GLA_CHUNK, GLA_KEY_WIDTH), lambda i: (lag(i), 0)),
                  heads(QK_PAD), heads(QK_PAD), heads(MLA_DV))
    assert len(post_specs) == N_POST_OUT

    front_specs = [rows(D_MODEL, cur)]
    front_args = [x2]
    if fuse_out:
        o_gla, o_mla = prev
        front_specs += [rows(GLA_WIDTH, cur), rows(MLA_WIDTH, cur), _layer((D_MODEL, D_MODEL), l - 1)]
        front_args += [o_gla, o_mla, w["w_out"]]
    front_names = ["norm_g", "w_in", "w_mix"]
    post_names = ["w_misc", "b_gate", "q_norm_g", "w_uq", "kv_norm_g", "w_ukv",
                  "qg_nope", "qg_rope", "kg_nope", "kg_rope", "k_bias"]
    front_specs += [_layer(w[name].shape[1:], l) for name in front_names]
    front_args += [w[name] for name in front_names]
    post_in_specs = [rows(LANES, lag), rows(LANES, lag)] + [_layer(w[name].shape[1:], l) for name in post_names]
    post_in_args = [cos_t, sin_t] + [w[name] for name in post_names]
    assert len(post_in_specs) == N_POST_IN

    out_shape = front_shapes + post_shapes
    out_specs = front_out_specs + post_specs
    if fuse_out:
        out_shape = (jax.ShapeDtypeStruct((T, D_MODEL), F32),) + out_shape
        out_specs = (rows(D_MODEL, cur),) + out_specs
    return pl.pallas_call(
        functools.partial(_proj_kernel, fuse_out=fuse_out),
        grid=(n + 1,), in_specs=front_specs + post_in_specs, out_specs=out_specs, out_shape=out_shape,
        scratch_shapes=[pltpu.VMEM((tm, D_IN_PAD), F32), pltpu.VMEM((tm, D_IN_PAD), F32)],
        compiler_params=_params(1), name="outproj_proj" if fuse_out else "proj",
    )(*front_args, *post_in_args)


def _gla_call(qd, ki, ke, gv, dec, gg, gla_g, l, S):
    T = qd.shape[0]
    tm = GLA_TM
    rows = lambda w: pl.BlockSpec((tm, w), lambda i: (i, 0))
    return pl.pallas_call(
        functools.partial(_gla_kernel, steps_per_seq=S // tm),
        grid=(T // tm,),
        in_specs=[rows(GLA_KEY_WIDTH), rows(GLA_KEY_WIDTH), rows(GLA_KEY_WIDTH), rows(GLA_WIDTH),
                  pl.BlockSpec((tm // GLA_CHUNK, GLA_KEY_WIDTH), lambda i: (i, 0)),
                  rows(GLA_WIDTH), _layer((1, GLA_DV), l)],
        out_specs=rows(GLA_WIDTH),
        out_shape=jax.ShapeDtypeStruct((T, GLA_WIDTH), BF16),
        scratch_shapes=[pltpu.VMEM((GLA_WIDTH, LANES), F32)],
        compiler_params=_params(1), name="gla",
    )(qd, ki, ke, gv, dec, gg, gla_g)


def _attn_call(q, k, v, mg):
    B, H, S, _ = q.shape
    tq = ATTN_TQ
    nq = S // tq
    q_spec = pl.BlockSpec((None, None, tq, QK_PAD), lambda b, h, i: (b, h, i, 0))
    kv_specs = [pl.BlockSpec((None, None, S, QK_PAD), lambda b, h, i: (b, h, 0, 0)),
                pl.BlockSpec((None, None, S, MLA_DV), lambda b, h, i: (b, h, 0, 0))]
    io_spec = pl.BlockSpec((tq, MLA_DV), lambda b, h, i: (b * nq + i, h))
    return pl.pallas_call(
        _attn_kernel, grid=(B, H, nq),
        in_specs=[q_spec] + kv_specs + [io_spec], out_specs=io_spec,
        out_shape=jax.ShapeDtypeStruct(mg.shape, BF16),
        scratch_shapes=[pltpu.VMEM((tq, 1), F32), pltpu.VMEM((tq, 1), F32), pltpu.VMEM((tq, MLA_DV), F32)],
        compiler_params=_params(3), name="attn_online",
    )(q, k, v, mg)


def _attn_bounded_call(q, k, v, mg):
    B, H, S, _ = q.shape
    seq = lambda w: pl.BlockSpec((None, None, S, w), lambda b, h: (b, h, 0, 0))
    io_spec = pl.BlockSpec((S, MLA_DV), lambda b, h: (b, h))
    return pl.pallas_call(
        _attn_bounded_kernel, grid=(B, H),
        in_specs=[seq(QK_PAD), seq(QK_PAD), seq(MLA_DV), io_spec], out_specs=io_spec,
        out_shape=jax.ShapeDtypeStruct(mg.shape, BF16),
        compiler_params=_params(2), name="attn_bounded",
    )(q, k, v, mg)


def _out_call(x2, og, om, w_out, l):
    T = x2.shape[0]
    tm = OUT_TM
    rows = lambda w: pl.BlockSpec((tm, w), lambda i: (i, 0))
    return pl.pallas_call(
        _out_kernel, grid=(T // tm,),
        in_specs=[rows(D_MODEL), rows(GLA_WIDTH), rows(MLA_WIDTH), _layer((D_MODEL, D_MODEL), l)],
        out_specs=rows(D_MODEL),
        out_shape=jax.ShapeDtypeStruct((T, D_MODEL), F32),
        compiler_params=_params(1), name="outproj",
    )(x2, og, om, w_out)


def _prep_weights(norm_g, w_in, w_gla_gate_up, b_gla_gate, gla_norm_g, mla_q_norm_g, w_uq,
                  mla_kv_norm_g, w_ukv, q_head_g, k_head_g, w_out):
    depth = w_in.shape[0]
    o = [0]
    for wdt in (GLA_KEY_WIDTH, GLA_KEY_WIDTH, GLA_WIDTH, GLA_GATE_RANK, GLA_WIDTH,
                MLA_Q_RANK, MLA_KV_RANK, MLA_ROPE, MLA_WIDTH):
        o.append(o[-1] + wdt)
    wt = jnp.swapaxes(w_in, 1, 2).astype(BF16)
    pad = jnp.zeros((depth, LANES - MLA_ROPE - GLA_GATE_RANK, D_MODEL), BF16)
    w_mix = jnp.concatenate([wt[:, o[6]:o[8]], wt[:, o[3]:o[4]], pad], axis=1)
    assert w_mix.shape[1] == PROJ_CHUNK and o[9] == D_IN
    assert [o[0], o[1], o[2], o[2] + PROJ_CHUNK, o[4], o[4] + PROJ_CHUNK, o[5], None, o[8], o[8] + PROJ_CHUNK] \
        == list(SRC_ROW_OF_CHUNK)

    w_misc = jnp.zeros((depth, LANES, GLA_KEY_WIDTH), F32)
    w_misc = w_misc.at[:, MLA_ROPE:MLA_ROPE + GLA_GATE_RANK, :].set(w_gla_gate_up).astype(BF16)

    wq = w_uq.reshape(depth, MLA_Q_RANK, MLA_HEADS, MLA_QK)
    w_uq_p = jnp.concatenate([wq[..., :MLA_NOPE].reshape(depth, MLA_Q_RANK, -1),
                              wq[..., MLA_NOPE:].reshape(depth, MLA_Q_RANK, -1)], axis=-1).astype(BF16)
    wkv = w_ukv.reshape(depth, MLA_KV_RANK, MLA_HEADS, MLA_NOPE + MLA_DV)
    w_ukv_p = jnp.concatenate([wkv[..., :MLA_NOPE].reshape(depth, MLA_KV_RANK, -1),
                               wkv[..., MLA_NOPE:].reshape(depth, MLA_KV_RANK, -1)], axis=-1).astype(BF16)

    zeros64 = jnp.zeros((depth, 1, LANES - MLA_ROPE), F32)
    row = lambda a: a[:, None, :]
    return dict(
        norm_g=row(norm_g), w_in=wt, w_mix=w_mix, w_misc=w_misc, b_gate=row(b_gla_gate),
        gla_g=row(gla_norm_g), q_norm_g=row(mla_q_norm_g), w_uq=w_uq_p,
        kv_norm_g=row(mla_kv_norm_g), w_ukv=w_ukv_p,
        qg_nope=row(q_head_g[:, :MLA_NOPE]),
        qg_rope=jnp.concatenate([row(q_head_g[:, MLA_NOPE:])] * 2, axis=-1),
        kg_nope=row(k_head_g[:, :MLA_NOPE]),
        kg_rope=jnp.concatenate([row(k_head_g[:, MLA_NOPE:]), zeros64], axis=-1),
        w_out=w_out.astype(BF16),
    )


def _score_bounds(q_head_g, k_head_g):
    bound = (MLA_QK ** 0.5 * LOG2E * SCORE_BOUND_SLACK
             * jnp.max(jnp.abs(q_head_g), axis=-1) * jnp.max(jnp.abs(k_head_g), axis=-1))
    use = bound <= SCORE_BOUND_MAX_LOG2
    depth = q_head_g.shape[0]
    bias = jnp.zeros((depth, 1, LANES), F32).at[:, 0, MLA_ROPE].set(jnp.where(use, -bound, 0.0))
    return use, bias


def _rope_tables(positions):
    inv_freq = ROPE_THETA ** (-jnp.arange(0, MLA_ROPE, 2, dtype=F32) / MLA_ROPE)
    reps = LANES // inv_freq.shape[0]
    inv_freq = jnp.tile(inv_freq, reps)
    sign = jnp.tile(jnp.repeat(jnp.array([-1.0, 1.0], F32), MLA_ROPE // 2), LANES // MLA_ROPE)
    ang = positions.astype(F32).reshape(-1, 1) * inv_freq
    return jnp.cos(ang), jnp.sin(ang) * sign


def kernel(x, positions, norm_g, w_in, w_gla_gate_up, b_gla_gate, gla_norm_g, mla_q_norm_g, w_uq,
           mla_kv_norm_g, w_ukv, q_head_g, k_head_g, w_out):
    B, S, D = x.shape
    assert D == D_MODEL and S % max(PROJ_TM, GLA_TM, ATTN_TQ, OUT_TM) == 0 and ATTN_TQ % ATTN_TK == 0
    depth = w_in.shape[0]
    weights = _prep_weights(norm_g, w_in, w_gla_gate_up, b_gla_gate, gla_norm_g, mla_q_norm_g, w_uq,
                            mla_kv_norm_g, w_ukv, q_head_g, k_head_g, w_out)
    use_bounded, weights["k_bias"] = _score_bounds(q_head_g, k_head_g)
    cos_t, sin_t = _rope_tables(positions)
    x2 = x.reshape(B * S, D)
    prev = None
    for l in range(depth):
        outs = _proj_call(x2, prev, cos_t, sin_t, weights, l, B, S)
        if prev is not None:
            x2, outs = outs[0], outs[1:]
        gv, gg, mg, qd, ki, ke, dec, q, k, v = outs
        o_gla = _gla_call(qd, ki, ke, gv, dec, gg, weights["gla_g"], l, S)
        o_mla = lax.cond(use_bounded[l], _attn_bounded_call, _attn_call, q, k, v, mg)
        prev = (o_gla, o_mla)
    x2 = _out_call(x2, *prev, weights["w_out"], depth - 1)
    return x2.reshape(B, S, D)
```

```python
import functools

import jax
import jax.numpy as jnp
from jax import lax
from jax.experimental import pallas as pl
from jax.experimental.pallas import tpu as pltpu

F32 = jnp.float32
BF16 = jnp.bfloat16

D_MODEL = 1024
GLA_HEADS = 4
GLA_DK = 64
GLA_DV = 128
GLA_KEY_WIDTH = GLA_HEADS * GLA_DK
GLA_WIDTH = GLA_HEADS * GLA_DV
GLA_GATE_RANK = 16
GLA_GATE_TEMP = 16.0
GLA_CHUNK = 64
MLA_HEADS = 4
MLA_NOPE = 128
MLA_ROPE = 64
MLA_QK = MLA_NOPE + MLA_ROPE
MLA_DV = 128
MLA_WIDTH = MLA_HEADS * MLA_DV
MLA_Q_RANK = 256
MLA_KV_RANK = 128
ROPE_THETA = 10000.0
EPS = 1e-6
LOG2E = 1.4426950408889634
SCORE_BOUND_MAX_LOG2 = 40.0 * LOG2E
SCORE_BOUND_SLACK = 1.02

LANES = 128
QK_PAD = 2 * LANES

OFF_GQ = 0
OFF_GK = 256
OFF_GV = 512
OFF_GGATE = 1024
OFF_CQ = 1536
OFF_CKV = 1792
OFF_MISC = 1920
OFF_MGATE = 2048
D_IN_PAD = 2560
D_IN = 2512
SRC_ROW_OF_CHUNK = (0, 256, 512, 768, 1040, 1296, 1552, None, 2000, 2256)

PROJ_TM = 512
GLA_TM = 2048
ATTN_TQ = 512
ATTN_TK = 512
OUT_TM = 2048
CUMSUM_BLOCK = 256
PROJ_CHUNK = 256
VMEM_LIMIT = 56 * 1024 * 1024


def _rms(t, g):
    return t * lax.rsqrt(jnp.mean(t * t, axis=-1, keepdims=True) + EPS) * g


def _silu(t):
    return t * jax.nn.sigmoid(t)


def _split_dot(mat_bf16, t):
    hi = t.astype(BF16)
    lo = (t - hi.astype(F32)).astype(BF16)
    return (jnp.dot(mat_bf16, hi, preferred_element_type=F32)
            + jnp.dot(mat_bf16, lo, preferred_element_type=F32))


def _swap_halves(t):
    lane = lax.broadcasted_iota(jnp.int32, t.shape, 1)
    return jnp.where(lane % MLA_ROPE < MLA_ROPE // 2,
                     pltpu.roll(t, LANES - MLA_ROPE // 2, axis=1),
                     pltpu.roll(t, MLA_ROPE // 2, axis=1))


N_POST_IN = 13
N_POST_OUT = 7
POST_STAGE_AT = (0, 5)


def _proj_kernel(*refs, fuse_out):
    refs = list(refs)
    x_ref = refs.pop(0)
    if fuse_out:
        og_ref, om_ref, wout_ref = refs[:3]
        del refs[:3]
    ng_ref, win_ref, wmix_ref = refs[:3]
    post_in = refs[3:3 + N_POST_IN]
    del refs[:3 + N_POST_IN]
    if fuse_out:
        xo_ref = refs.pop(0)
    gv_ref, gg_ref, mg_ref = refs[:3]
    post_out = refs[3:3 + N_POST_OUT]
    z0_ref, z1_ref = refs[3 + N_POST_OUT:]
    i = pl.program_id(0)
    nt = (((1,), (1,)), ((), ()))
    direct = {}
    for ref, off, fn in ((gv_ref, OFF_GV, None), (gg_ref, OFF_GGATE, _silu), (mg_ref, OFF_MGATE, _silu)):
        for k in range(ref.shape[1] // PROJ_CHUNK):
            direct[off // PROJ_CHUNK + k] = (ref, k, fn)

    @pl.when(i == 0)
    def _():
        z1_ref[...] = jnp.zeros_like(z1_ref)

    def step(z_read, z_write):
        post = _proj_post(z_read, *post_in, *post_out)
        x = x_ref[...]
        if fuse_out:
            w = wout_ref[...]
            x = (x + jnp.dot(og_ref[...], w[:GLA_WIDTH], preferred_element_type=F32)
                 + jnp.dot(om_ref[...], w[GLA_WIDTH:], preferred_element_type=F32))
            xo_ref[...] = x
        h = _rms(x, ng_ref[...]).astype(BF16)
        order = sorted(range(D_IN_PAD // PROJ_CHUNK), key=lambda c: ((c not in direct) == fuse_out, c))
        for pos, c in enumerate(order):
            if pos in POST_STAGE_AT:
                next(post)
            cols = slice(c * PROJ_CHUNK, (c + 1) * PROJ_CHUNK)
            src = SRC_ROW_OF_CHUNK[c]
            w_c = wmix_ref[...] if src is None else win_ref[src:src + PROJ_CHUNK, :]
            zc = lax.dot_general(h, w_c, nt, preferred_element_type=F32)
            if c in direct:
                ref, k, fn = direct[c]
                ref[:, k * PROJ_CHUNK:(k + 1) * PROJ_CHUNK] = (zc if fn is None else fn(zc)).astype(BF16)
            else:
                z_write[:, cols] = zc
        for _ in post:
            pass

    @pl.when(i % 2 == 0)
    def _():
        step(z1_ref, z0_ref)

    @pl.when(i % 2 == 1)
    def _():
        step(z0_ref, z1_ref)


def _proj_post(z, cos_ref, sin_ref, wmisc_ref, bgate_ref,
               qng_ref, wuq_ref, kvng_ref, wukv_ref, qgn_ref, qgr_ref, kgn_ref, kgr_ref, kbias_ref,
               qd_ref, ki_ref, ke_ref, dec_ref, q_ref, k_ref, v_ref):
    tm = z.shape[0]

    misc = z[:, OFF_MISC:OFF_MISC + LANES]
    logit = jnp.dot(misc.astype(BF16), wmisc_ref[...], preferred_element_type=F32) + bgate_ref[...]
    cqn = _rms(z[:, OFF_CQ:OFF_CQ + MLA_Q_RANK], qng_ref[...]).astype(BF16)
    qraw = jnp.dot(cqn, wuq_ref[...], preferred_element_type=F32)
    ckvn = _rms(z[:, OFF_CKV:OFF_CKV + MLA_KV_RANK], kvng_ref[...]).astype(BF16)
    kvraw = jnp.dot(ckvn, wukv_ref[...], preferred_element_type=F32)
    yield

    log_a = (jnp.minimum(logit, 0.0) - jnp.log(1.0 + jnp.exp(-jnp.abs(logit)))) * (1.0 / GLA_GATE_TEMP)

    cb = CUMSUM_BLOCK
    r = lax.broadcasted_iota(jnp.int32, (cb, cb), 0)
    c = lax.broadcasted_iota(jnp.int32, (cb, cb), 1)
    same = (r // GLA_CHUNK) == (c // GLA_CHUNK)
    lower = jnp.where(same & (c <= r), 1.0, 0.0).astype(BF16)
    upper = jnp.where(same & (c > r), 1.0, 0.0).astype(BF16)
    nck = cb // GLA_CHUNK
    sr = lax.broadcasted_iota(jnp.int32, (8, cb), 0)
    sc = lax.broadcasted_iota(jnp.int32, (8, cb), 1)
    chunk_sum = jnp.where((sc // GLA_CHUNK) == sr, 1.0, 0.0).astype(BF16)

    gq = z[:, OFF_GQ:OFF_GQ + GLA_KEY_WIDTH] * (GLA_DK ** -0.5)
    gk = z[:, OFF_GK:OFF_GK + GLA_KEY_WIDTH]
    for blk in range(tm // cb):
        rows = slice(blk * cb, (blk + 1) * cb)
        la = log_a[rows]
        b = _split_dot(lower, la)
        rest = _split_dot(upper, la)
        tot = _split_dot(chunk_sum, la)
        qd_ref[rows, :] = (gq[rows] * jnp.exp(b)).astype(BF16)
        ki_ref[rows, :] = (gk[rows] * jnp.exp(-b)).astype(BF16)
        ke_ref[rows, :] = (gk[rows] * jnp.exp(rest)).astype(BF16)
        dec_ref[blk * nck:(blk + 1) * nck, :] = jnp.exp(tot[:nck])
    yield

    cos = cos_ref[...]
    sin = sin_ref[...]
    lane = lax.broadcasted_iota(jnp.int32, (tm, LANES), 1)
    low64 = lane < MLA_ROPE

    scale = MLA_QK ** -0.5 * LOG2E
    one_col = jnp.where(lane == MLA_ROPE, 1.0, 0.0)
    nope_w = MLA_HEADS * MLA_NOPE
    for pair in range(MLA_HEADS // 2):
        t = qraw[:, nope_w + pair * LANES:nope_w + (pair + 1) * LANES]
        t2 = t * t
        ssq_lo = jnp.sum(jnp.where(low64, t2, 0.0), axis=-1, keepdims=True)
        ssq_hi = jnp.sum(jnp.where(low64, 0.0, t2), axis=-1, keepdims=True)
        tg = t * qgr_ref[...]
        rot = tg * cos + _swap_halves(tg) * sin
        for sub in range(2):
            hd = 2 * pair + sub
            qn = qraw[:, hd * MLA_NOPE:(hd + 1) * MLA_NOPE]
            ssq = jnp.sum(qn * qn, axis=-1, keepdims=True) + (ssq_lo if sub == 0 else ssq_hi)
            rinv = lax.rsqrt(ssq * (1.0 / MLA_QK) + EPS) * scale
            q_ref[hd, :, 0:LANES] = (qn * rinv * qgn_ref[...]).astype(BF16)
            rsel = rot if sub == 0 else pltpu.roll(rot, MLA_ROPE, axis=1)
            q_ref[hd, :, LANES:QK_PAD] = jnp.where(low64, rsel * rinv, one_col).astype(BF16)

    kpe2 = misc * misc
    ssq_pe = jnp.sum(jnp.where(low64, kpe2, 0.0), axis=-1, keepdims=True)
    kg = misc * kgr_ref[...]
    krot = kg * cos + _swap_halves(kg) * sin
    for hd in range(MLA_HEADS):
        kn = kvraw[:, hd * MLA_NOPE:(hd + 1) * MLA_NOPE]
        ssq = jnp.sum(kn * kn, axis=-1, keepdims=True) + ssq_pe
        rinv = lax.rsqrt(ssq * (1.0 / MLA_QK) + EPS)
        k_ref[hd, :, 0:LANES] = (kn * rinv * kgn_ref[...]).astype(BF16)
        k_ref[hd, :, LANES:QK_PAD] = (krot * rinv + kbias_ref[...]).astype(BF16)
        v_ref[hd, :, :] = kvraw[:, nope_w + hd * MLA_DV:nope_w + (hd + 1) * MLA_DV].astype(BF16)


def _gla_kernel(qd_ref, ki_ref, ke_ref, v_ref, dec_ref, gate_ref, g_ref, o_ref, state_ref, *, steps_per_seq):
    @pl.when(pl.program_id(0) % steps_per_seq == 0)
    def _():
        state_ref[...] = jnp.zeros_like(state_ref)

    for _ in _gla_chunks(qd_ref, ki_ref, ke_ref, v_ref, dec_ref, gate_ref, g_ref, o_ref, state_ref):
        pass


def _gla_chunks(qd_ref, ki_ref, ke_ref, v_ref, dec_ref, gate_ref, g_ref, o_ref, state_ref):
    tm = qd_ref.shape[0]

    C = GLA_CHUNK
    klane_head = lax.broadcasted_iota(jnp.int32, (C, GLA_KEY_WIDTH), 1) // GLA_DK
    vlane_head = lax.broadcasted_iota(jnp.int32, (C, GLA_WIDTH), 1) // GLA_DV
    row = lax.broadcasted_iota(jnp.int32, (C, GLA_KEY_WIDTH), 0)
    col_in_head = lax.broadcasted_iota(jnp.int32, (C, GLA_KEY_WIDTH), 1) % GLA_DK
    causal = row >= col_in_head
    pair_rows = 2 * GLA_DV
    own = ((lax.broadcasted_iota(jnp.int32, (pair_rows, LANES), 0) // GLA_DV)
           == (lax.broadcasted_iota(jnp.int32, (pair_rows, LANES), 1) // GLA_DK))
    nt = (((1,), (1,)), ((), ()))
    tn = (((0,), (0,)), ((), ()))
    g = g_ref[...]
    n_pairs = GLA_HEADS // 2

    for ck in range(tm // C):
        rows = slice(ck * C, (ck + 1) * C)
        qd = qd_ref[rows, :]
        ki = ki_ref[rows, :]
        ke = ke_ref[rows, :]
        v = v_ref[rows, :]
        upd = [lax.dot_general(v[:, p * pair_rows:(p + 1) * pair_rows], ke[:, p * LANES:(p + 1) * LANES],
                               tn, preferred_element_type=F32) for p in range(n_pairs)]
        zk = jnp.zeros_like(ki)
        zv = jnp.zeros_like(v)
        kstack = jnp.concatenate([jnp.where(klane_head == hd, ki, zk) for hd in range(GLA_HEADS)], axis=0)
        vstack = jnp.concatenate([jnp.where(vlane_head == hd, v, zv) for hd in range(GLA_HEADS)], axis=0)
        a = lax.dot_general(qd, kstack, nt, preferred_element_type=F32)
        a = jnp.where(causal, a, 0.0).astype(BF16)
        o = jnp.dot(a, vstack, preferred_element_type=F32)
        state = state_ref[...]
        o = o + jnp.concatenate(
            [lax.dot_general(qd[:, p * LANES:(p + 1) * LANES],
                             state[p * pair_rows:(p + 1) * pair_rows].astype(BF16),
                             nt, preferred_element_type=F32) for p in range(n_pairs)], axis=1)
        for p in range(n_pairs):
            srows = slice(p * pair_rows, (p + 1) * pair_rows)
            state_ref[srows, :] = (state[srows] * dec_ref[ck:ck + 1, p * LANES:(p + 1) * LANES]
                                   + jnp.where(own, upd[p], 0.0))
        for hd in range(GLA_HEADS):
            cols = slice(hd * GLA_DV, (hd + 1) * GLA_DV)
            oh = o[:, cols]
            on = oh * lax.rsqrt(jnp.mean(oh * oh, axis=-1, keepdims=True) + EPS) * g
            o_ref[rows, cols] = (on * gate_ref[rows, cols].astype(F32)).astype(BF16)
        yield


def _attn_kernel(q_ref, k_ref, v_ref, gate_ref, o_ref, m_ref, l_ref, acc_ref):
    tq = q_ref.shape[0]
    tk = ATTN_TK
    i = pl.program_id(2)
    q = q_ref[...]
    nt = (((1,), (1,)), ((), ()))
    m_ref[...] = jnp.full_like(m_ref, -jnp.inf)
    l_ref[...] = jnp.zeros_like(l_ref)
    acc_ref[...] = jnp.zeros_like(acc_ref)

    def step(j, masked):
        start = pl.multiple_of(j * tk, tk)
        k = k_ref[pl.ds(start, tk), :]
        v = v_ref[pl.ds(start, tk), :]
        s = lax.dot_general(q, k, nt, preferred_element_type=F32)
        if masked:
            rr = lax.broadcasted_iota(jnp.int32, s.shape, 0)
            cc = lax.broadcasted_iota(jnp.int32, s.shape, 1)
            s = jnp.where(cc <= rr, s, -jnp.inf)
        m_prev = m_ref[...]
        m_new = jnp.maximum(m_prev, jnp.max(s, axis=-1, keepdims=True))
        alpha = jnp.exp2(m_prev - m_new)
        p = jnp.exp2(s - m_new)
        l_ref[...] = alpha * l_ref[...] + jnp.sum(p, axis=-1, keepdims=True)
        acc_ref[...] = alpha * acc_ref[...] + jnp.dot(p.astype(BF16), v, preferred_element_type=F32)
        m_ref[...] = m_new

    def body(j, carry):
        step(j, False)
        return carry

    lax.fori_loop(0, i * (tq // tk), body, 0)
    step(i * (tq // tk), True)
    o = acc_ref[...] / l_ref[...]
    o_ref[...] = (o * gate_ref[...].astype(F32)).astype(BF16)


def _attn_bounded_kernel(q_ref, k_ref, v_ref, gate_ref, o_ref):
    t = ATTN_TQ
    half = t // 2
    n_tiles = k_ref.shape[0] // t
    nt = (((1,), (1,)), ((), ()))
    ones_col = jnp.where(lax.broadcasted_iota(jnp.int32, (t, LANES), 1) == 0, 1.0, 0.0).astype(BF16)
    causal_a = (lax.broadcasted_iota(jnp.int32, (half, half), 1)
                <= lax.broadcasted_iota(jnp.int32, (half, half), 0))
    causal_b = (lax.broadcasted_iota(jnp.int32, (half, t), 1)
                <= lax.broadcasted_iota(jnp.int32, (half, t), 0) + half)

    def probs(q, k):
        return jnp.exp2(lax.dot_general(q, k, nt, preferred_element_type=F32))

    def key_tile(q, j):
        k = k_ref[j * t:(j + 1) * t, :]
        v_aug = jnp.concatenate([v_ref[j * t:(j + 1) * t, :], ones_col], axis=1)
        return jnp.dot(probs(q, k).astype(BF16), v_aug, preferred_element_type=F32)

    def diagonal_tile(q, j):
        k = k_ref[j * t:(j + 1) * t, :]
        v_aug = jnp.concatenate([v_ref[j * t:(j + 1) * t, :], ones_col], axis=1)
        p_a = jnp.where(causal_a, probs(q[:half], k[:half]), 0.0).astype(BF16)
        p_b = jnp.where(causal_b, probs(q[half:], k), 0.0).astype(BF16)
        return jnp.concatenate([jnp.dot(p_a, v_aug[:half], preferred_element_type=F32),
                                jnp.dot(p_b, v_aug, preferred_element_type=F32)], axis=0)

    for idx in range(n_tiles):
        rows = slice(idx * t, (idx + 1) * t)
        q = q_ref[rows, :]
        acc = diagonal_tile(q, idx)
        for j in range(idx):
            acc = acc + key_tile(q, j)
        o = acc[:, :MLA_DV] * (1.0 / acc[:, MLA_DV:MLA_DV + 1])
        o_ref[rows, :] = (o * gate_ref[rows, :].astype(F32)).astype(BF16)


def _out_kernel(x_ref, og_ref, om_ref, w_ref, o_ref):
    w = w_ref[...]
    o_ref[...] = (x_ref[...]
                  + jnp.dot(og_ref[...], w[:GLA_WIDTH], preferred_element_type=F32)
                  + jnp.dot(om_ref[...], w[GLA_WIDTH:], preferred_element_type=F32))


def _layer(shape, l):
    return pl.BlockSpec((None,) + tuple(shape), lambda *idx: (l,) + (0,) * len(shape),
                        pipeline_mode=pl.Buffered(1))


def _params(n_axes):
    return pltpu.CompilerParams(dimension_semantics=("arbitrary",) * n_axes,
                                vmem_limit_bytes=VMEM_LIMIT)


def _proj_call(x2, prev, cos_t, sin_t, w, l, B, S):
    T = x2.shape[0]
    tm = PROJ_TM
    ns = S // tm
    n = T // tm
    fuse_out = prev is not None
    cur = lambda i: jnp.minimum(i, n - 1)
    lag = lambda i: jnp.maximum(i - 1, 0)
    rows = lambda w, at: pl.BlockSpec((tm, w), lambda i: (at(i), 0))
    heads = lambda w: pl.BlockSpec((None, MLA_HEADS, tm, w), lambda i: (lag(i) // ns, 0, lag(i) % ns, 0))
    front_shapes = (
        jax.ShapeDtypeStruct((T, GLA_WIDTH), BF16),
        jax.ShapeDtypeStruct((T, GLA_WIDTH), BF16),
        jax.ShapeDtypeStruct((T, MLA_WIDTH), BF16),
    )
    front_out_specs = (rows(GLA_WIDTH, cur), rows(GLA_WIDTH, cur), rows(MLA_WIDTH, cur))
    post_shapes = (
        jax.ShapeDtypeStruct((T, GLA_KEY_WIDTH), BF16),
        jax.ShapeDtypeStruct((T, GLA_KEY_WIDTH), BF16),
        jax.ShapeDtypeStruct((T, GLA_KEY_WIDTH), BF16),
        jax.ShapeDtypeStruct((T // GLA_CHUNK, GLA_KEY_WIDTH), F32),
        jax.ShapeDtypeStruct((B, MLA_HEADS, S, QK_PAD), BF16),
        jax.ShapeDtypeStruct((B, MLA_HEADS, S, QK_PAD), BF16),
        jax.ShapeDtypeStruct((B, MLA_HEADS, S, MLA_DV), BF16),
    )
    post_specs = (rows(GLA_KEY_WIDTH, lag), rows(GLA_KEY_WIDTH, lag), rows(GLA_KEY_WIDTH, lag),
                  pl.BlockSpec((tm // GLA_CHUNK, GLA_KEY_WIDTH), lambda i: (lag(i), 0)),
                  heads(QK_PAD), heads(QK_PAD), heads(MLA_DV))
    assert len(post_specs) == N_POST_OUT

    front_specs = [rows(D_MODEL, cur)]
    front_args = [x2]
    if fuse_out:
        o_gla, o_mla = prev
        front_specs += [rows(GLA_WIDTH, cur), rows(MLA_WIDTH, cur), _layer((D_MODEL, D_MODEL), l - 1)]
        front_args += [o_gla, o_mla, w["w_out"]]
    front_names = ["norm_g", "w_in", "w_mix"]
    post_names = ["w_misc", "b_gate", "q_norm_g", "w_uq", "kv_norm_g", "w_ukv",
                  "qg_nope", "qg_rope", "kg_nope", "kg_rope", "k_bias"]
    front_specs += [_layer(w[name].shape[1:], l) for name in front_names]
    front_args += [w[name] for name in front_names]
    post_in_specs = [rows(LANES, lag), rows(LANES, lag)] + [_layer(w[name].shape[1:], l) for name in post_names]
    post_in_args = [cos_t, sin_t] + [w[name] for name in post_names]
    assert len(post_in_specs) == N_POST_IN

    out_shape = front_shapes + post_shapes
    out_specs = front_out_specs + post_specs
    if fuse_out:
        out_shape = (jax.ShapeDtypeStruct((T, D_MODEL), F32),) + out_shape
        out_specs = (rows(D_MODEL, cur),) + out_specs
    return pl.pallas_call(
        functools.partial(_proj_kernel, fuse_out=fuse_out),
        grid=(n + 1,), in_specs=front_specs + post_in_specs, out_specs=out_specs, out_shape=out_shape,
        scratch_shapes=[pltpu.VMEM((tm, D_IN_PAD), F32), pltpu.VMEM((tm, D_IN_PAD), F32)],
        compiler_params=_params(1), name="outproj_proj" if fuse_out else "proj",
    )(*front_args, *post_in_args)


def _gla_call(qd, ki, ke, gv, dec, gg, gla_g, l, S):
    T = qd.shape[0]
    tm = GLA_TM
    rows = lambda w: pl.BlockSpec((tm, w), lambda i: (i, 0))
    return pl.pallas_call(
        functools.partial(_gla_kernel, steps_per_seq=S // tm),
        grid=(T // tm,),
        in_specs=[rows(GLA_KEY_WIDTH), rows(GLA_KEY_WIDTH), rows(GLA_KEY_WIDTH), rows(GLA_WIDTH),
                  pl.BlockSpec((tm // GLA_CHUNK, GLA_KEY_WIDTH), lambda i: (i, 0)),
                  rows(GLA_WIDTH), _layer((1, GLA_DV), l)],
        out_specs=rows(GLA_WIDTH),
        out_shape=jax.ShapeDtypeStruct((T, GLA_WIDTH), BF16),
        scratch_shapes=[pltpu.VMEM((GLA_WIDTH, LANES), F32)],
        compiler_params=_params(1), name="gla",
    )(qd, ki, ke, gv, dec, gg, gla_g)


def _attn_call(q, k, v, mg):
    B, H, S, _ = q.shape
    tq = ATTN_TQ
    nq = S // tq
    q_spec = pl.BlockSpec((None, None, tq, QK_PAD), lambda b, h, i: (b, h, i, 0))
    kv_specs = [pl.BlockSpec((None, None, S, QK_PAD), lambda b, h, i: (b, h, 0, 0)),
                pl.BlockSpec((None, None, S, MLA_DV), lambda b, h, i: (b, h, 0, 0))]
    io_spec = pl.BlockSpec((tq, MLA_DV), lambda b, h, i: (b * nq + i, h))
    return pl.pallas_call(
        _attn_kernel, grid=(B, H, nq),
        in_specs=[q_spec] + kv_specs + [io_spec], out_specs=io_spec,
        out_shape=jax.ShapeDtypeStruct(mg.shape, BF16),
        scratch_shapes=[pltpu.VMEM((tq, 1), F32), pltpu.VMEM((tq, 1), F32), pltpu.VMEM((tq, MLA_DV), F32)],
        compiler_params=_params(3), name="attn_online",
    )(q, k, v, mg)


def _attn_bounded_call(q, k, v, mg):
    B, H, S, _ = q.shape
    seq = lambda w: pl.BlockSpec((None, None, S, w), lambda b, h: (b, h, 0, 0))
    io_spec = pl.BlockSpec((S, MLA_DV), lambda b, h: (b, h))
    return pl.pallas_call(
        _attn_bounded_kernel, grid=(B, H),
        in_specs=[seq(QK_PAD), seq(QK_PAD), seq(MLA_DV), io_spec], out_specs=io_spec,
        out_shape=jax.ShapeDtypeStruct(mg.shape, BF16),
        compiler_params=_params(2), name="attn_bounded",
    )(q, k, v, mg)


def _out_call(x2, og, om, w_out, l):
    T = x2.shape[0]
    tm = OUT_TM
    rows = lambda w: pl.BlockSpec((tm, w), lambda i: (i, 0))
    return pl.pallas_call(
        _out_kernel, grid=(T // tm,),
        in_specs=[rows(D_MODEL), rows(GLA_WIDTH), rows(MLA_WIDTH), _layer((D_MODEL, D_MODEL), l)],
        out_specs=rows(D_MODEL),
        out_shape=jax.ShapeDtypeStruct((T, D_MODEL), F32),
        compiler_params=_params(1), name="outproj",
    )(x2, og, om, w_out)


def _prep_weights(norm_g, w_in, w_gla_gate_up, b_gla_gate, gla_norm_g, mla_q_norm_g, w_uq,
                  mla_kv_norm_g, w_ukv, q_head_g, k_head_g, w_out):
    depth = w_in.shape[0]
    o = [0]
    for wdt in (GLA_KEY_WIDTH, GLA_KEY_WIDTH, GLA_WIDTH, GLA_GATE_RANK, GLA_WIDTH,
                MLA_Q_RANK, MLA_KV_RANK, MLA_ROPE, MLA_WIDTH):
        o.append(o[-1] + wdt)
    wt = jnp.swapaxes(w_in, 1, 2).astype(BF16)
    pad = jnp.zeros((depth, LANES - MLA_ROPE - GLA_GATE_RANK, D_MODEL), BF16)
    w_mix = jnp.concatenate([wt[:, o[6]:o[8]], wt[:, o[3]:o[4]], pad], axis=1)
    assert w_mix.shape[1] == PROJ_CHUNK and o[9] == D_IN
    assert [o[0], o[1], o[2], o[2] + PROJ_CHUNK, o[4], o[4] + PROJ_CHUNK, o[5], None, o[8], o[8] + PROJ_CHUNK] \
        == list(SRC_ROW_OF_CHUNK)

    w_misc = jnp.zeros((depth, LANES, GLA_KEY_WIDTH), F32)
    w_misc = w_misc.at[:, MLA_ROPE:MLA_ROPE + GLA_GATE_RANK, :].set(w_gla_gate_up).astype(BF16)

    wq = w_uq.reshape(depth, MLA_Q_RANK, MLA_HEADS, MLA_QK)
    w_uq_p = jnp.concatenate([wq[..., :MLA_NOPE].reshape(depth, MLA_Q_RANK, -1),
                              wq[..., MLA_NOPE:].reshape(depth, MLA_Q_RANK, -1)], axis=-1).astype(BF16)
    wkv = w_ukv.reshape(depth, MLA_KV_RANK, MLA_HEADS, MLA_NOPE + MLA_DV)
    w_ukv_p = jnp.concatenate([wkv[..., :MLA_NOPE].reshape(depth, MLA_KV_RANK, -1),
                               wkv[..., MLA_NOPE:].reshape(depth, MLA_KV_RANK, -1)], axis=-1).astype(BF16)

    zeros64 = jnp.zeros((depth, 1, LANES - MLA_ROPE), F32)
    row = lambda a: a[:, None, :]
    return dict(
        norm_g=row(norm_g), w_in=wt, w_mix=w_mix, w_misc=w_misc, b_gate=row(b_gla_gate),
        gla_g=row(gla_norm_g), q_norm_g=row(mla_q_norm_g), w_uq=w_uq_p,
        kv_norm_g=row(mla_kv_norm_g), w_ukv=w_ukv_p,
        qg_nope=row(q_head_g[:, :MLA_NOPE]),
        qg_rope=jnp.concatenate([row(q_head_g[:, MLA_NOPE:])] * 2, axis=-1),
        kg_nope=row(k_head_g[:, :MLA_NOPE]),
        kg_rope=jnp.concatenate([row(k_head_g[:, MLA_NOPE:]), zeros64], axis=-1),
        w_out=w_out.astype(BF16),
    )


def _score_bounds(q_head_g, k_head_g):
    bound = (MLA_QK ** 0.5 * LOG2E * SCORE_BOUND_SLACK
             * jnp.max(jnp.abs(q_head_g), axis=-1) * jnp.max(jnp.abs(k_head_g), axis=-1))
    use = bound <= SCORE_BOUND_MAX_LOG2
    depth = q_head_g.shape[0]
    bias = jnp.zeros((depth, 1, LANES), F32).at[:, 0, MLA_ROPE].set(jnp.where(use, -bound, 0.0))
    return use, bias


def _rope_tables(positions):
    inv_freq = ROPE_THETA ** (-jnp.arange(0, MLA_ROPE, 2, dtype=F32) / MLA_ROPE)
    reps = LANES // inv_freq.shape[0]
    inv_freq = jnp.tile(inv_freq, reps)
    sign = jnp.tile(jnp.repeat(jnp.array([-1.0, 1.0], F32), MLA_ROPE // 2), LANES // MLA_ROPE)
    ang = positions.astype(F32).reshape(-1, 1) * inv_freq
    return jnp.cos(ang), jnp.sin(ang) * sign


def kernel(x, positions, norm_g, w_in, w_gla_gate_up, b_gla_gate, gla_norm_g, mla_q_norm_g, w_uq,
           mla_kv_norm_g, w_ukv, q_head_g, k_head_g, w_out):
    B, S, D = x.shape
    assert D == D_MODEL and S % max(PROJ_TM, GLA_TM, ATTN_TQ, OUT_TM) == 0 and ATTN_TQ % ATTN_TK == 0
    depth = w_in.shape[0]
    weights = _prep_weights(norm_g, w_in, w_gla_gate_up, b_gla_gate, gla_norm_g, mla_q_norm_g, w_uq,
                            mla_kv_norm_g, w_ukv, q_head_g, k_head_g, w_out)
    use_bounded, weights["k_bias"] = _score_bounds(q_head_g, k_head_g)
    cos_t, sin_t = _rope_tables(positions)
    x2 = x.reshape(B * S, D)
    prev = None
    for l in range(depth):
        outs = _proj_call(x2, prev, cos_t, sin_t, weights, l, B, S)
        if prev is not None:
            x2, outs = outs[0], outs[1:]
        gv, gg, mg, qd, ki, ke, dec, q, k, v = outs
        o_gla = _gla_call(qd, ki, ke, gv, dec, gg, weights["gla_g"], l, S)
        o_mla = lax.cond(use_bounded[l], _attn_bounded_call, _attn_call, q, k, v, mg)
        prev = (o_gla, o_mla)
    x2 = _out_call(x2, *prev, weights["w_out"], depth - 1)
    return x2.reshape(B, S, D)
```

```python
import functools

import jax
import jax.numpy as jnp
from jax import lax
from jax.experimental import pallas as pl
from jax.experimental.pallas import tpu as pltpu

F32 = jnp.float32
BF16 = jnp.bfloat16

D_MODEL = 1024
GLA_HEADS = 4
GLA_DK = 64
GLA_DV = 128
GLA_KEY_WIDTH = GLA_HEADS * GLA_DK
GLA_WIDTH = GLA_HEADS * GLA_DV
GLA_GATE_RANK = 16
GLA_GATE_TEMP = 16.0
GLA_CHUNK = 64
MLA_HEADS = 4
MLA_NOPE = 128
MLA_ROPE = 64
MLA_QK = MLA_NOPE + MLA_ROPE
MLA_DV = 128
MLA_WIDTH = MLA_HEADS * MLA_DV
MLA_Q_RANK = 256
MLA_KV_RANK = 128
ROPE_THETA = 10000.0
EPS = 1e-6
LOG2E = 1.4426950408889634
SCORE_BOUND_MAX_LOG2 = 40.0 * LOG2E
SCORE_BOUND_SLACK = 1.02

LANES = 128
SUBLANES = 8
QK_PAD = 2 * LANES

OFF_GQ = 0
OFF_GK = 256
OFF_GV = 512
OFF_GGATE = 1024
OFF_CQ = 1536
OFF_CKV = 1792
OFF_MISC = 1920
OFF_MGATE = 2048
D_IN_PAD = 2560
D_IN = 2512
SRC_ROW_OF_CHUNK = (0, 256, 512, 768, 1040, 1296, 1552, None, 2000, 2256)

PROJ_TM = 512
GLA_TM = 2048
ATTN_TQ = 512
ATTN_TK = 512
OUT_TM = 1024
CUMSUM_BLOCK = 256
PROJ_CHUNK = 256
VMEM_LIMIT = 56 * 1024 * 1024


def _rms(t, g):
    return t * lax.rsqrt(jnp.mean(t * t, axis=-1, keepdims=True) + EPS) * g


def _silu(t):
    return t * jax.nn.sigmoid(t)


def _split_dot(mat_bf16, t):
    hi = t.astype(BF16)
    lo = (t - hi.astype(F32)).astype(BF16)
    return (jnp.dot(mat_bf16, hi, preferred_element_type=F32)
            + jnp.dot(mat_bf16, lo, preferred_element_type=F32))


def _swap_halves(t):
    lane = lax.broadcasted_iota(jnp.int32, t.shape, 1)
    return jnp.where(lane % MLA_ROPE < MLA_ROPE // 2,
                     pltpu.roll(t, LANES - MLA_ROPE // 2, axis=1),
                     pltpu.roll(t, MLA_ROPE // 2, axis=1))


N_POST_IN = 13
N_POST_OUT = 7
POST_STAGE_AT = (0, 5)


def _proj_kernel(*refs, fuse_out):
    refs = list(refs)
    x_ref = refs.pop(0)
    if fuse_out:
        og_ref, om_ref, wout_ref = refs[:3]
        del refs[:3]
    ng_ref, win_ref, wmix_ref = refs[:3]
    post_in = refs[3:3 + N_POST_IN]
    del refs[:3 + N_POST_IN]
    if fuse_out:
        xo_ref = refs.pop(0)
    gv_ref, gg_ref, mg_ref = refs[:3]
    post_out = refs[3:3 + N_POST_OUT]
    z0_ref, z1_ref = refs[3 + N_POST_OUT:]
    i = pl.program_id(0)
    nt = (((1,), (1,)), ((), ()))
    direct = {}
    for ref, off, fn in ((gv_ref, OFF_GV, None), (gg_ref, OFF_GGATE, _silu), (mg_ref, OFF_MGATE, _silu)):
        for k in range(ref.shape[1] // PROJ_CHUNK):
            direct[off // PROJ_CHUNK + k] = (ref, k, fn)

    @pl.when(i == 0)
    def _():
        z1_ref[...] = jnp.zeros_like(z1_ref)

    def step(z_read, z_write):
        post = _proj_post(z_read, *post_in, *post_out)
        x = x_ref[...]
        if fuse_out:
            w = wout_ref[...]
            x = (x + jnp.dot(og_ref[...], w[:GLA_WIDTH], preferred_element_type=F32)
                 + jnp.dot(om_ref[...], w[GLA_WIDTH:], preferred_element_type=F32))
            xo_ref[...] = x
        h = _rms(x, ng_ref[...]).astype(BF16)
        order = sorted(range(D_IN_PAD // PROJ_CHUNK), key=lambda c: ((c not in direct) == fuse_out, c))
        for pos, c in enumerate(order):
            if pos in POST_STAGE_AT:
                next(post)
            cols = slice(c * PROJ_CHUNK, (c + 1) * PROJ_CHUNK)
            src = SRC_ROW_OF_CHUNK[c]
            w_c = wmix_ref[...] if src is None else win_ref[src:src + PROJ_CHUNK, :]
            zc = lax.dot_general(h, w_c, nt, preferred_element_type=F32)
            if c in direct:
                ref, k, fn = direct[c]
                ref[:, k * PROJ_CHUNK:(k + 1) * PROJ_CHUNK] = (zc if fn is None else fn(zc)).astype(BF16)
            else:
                z_write[:, cols] = zc
        for _ in post:
            pass

    @pl.when(i % 2 == 0)
    def _():
        step(z1_ref, z0_ref)

    @pl.when(i % 2 == 1)
    def _():
        step(z0_ref, z1_ref)


def _proj_post(z, cos_ref, sin_ref, wmisc_ref, bgate_ref,
               qng_ref, wuq_ref, kvng_ref, wukv_ref, qgn_ref, qgr_ref, kgn_ref, kgr_ref, kbias_ref,
               qd_ref, ki_ref, ke_ref, dec_ref, q_ref, k_ref, v_ref):
    tm = z.shape[0]

    misc = z[:, OFF_MISC:OFF_MISC + LANES]
    logit = jnp.dot(misc.astype(BF16), wmisc_ref[...], preferred_element_type=F32) + bgate_ref[...]
    cqn = _rms(z[:, OFF_CQ:OFF_CQ + MLA_Q_RANK], qng_ref[...]).astype(BF16)
    qraw = jnp.dot(cqn, wuq_ref[...], preferred_element_type=F32)
    ckvn = _rms(z[:, OFF_CKV:OFF_CKV + MLA_KV_RANK], kvng_ref[...]).astype(BF16)
    kvraw = jnp.dot(ckvn, wukv_ref[...], preferred_element_type=F32)
    yield

    log_a = (jnp.minimum(logit, 0.0) - jnp.log(1.0 + jnp.exp(-jnp.abs(logit)))) * (1.0 / GLA_GATE_TEMP)

    cb = CUMSUM_BLOCK
    r = lax.broadcasted_iota(jnp.int32, (cb, cb), 0)
    c = lax.broadcasted_iota(jnp.int32, (cb, cb), 1)
    same = (r // GLA_CHUNK) == (c // GLA_CHUNK)
    lower = jnp.where(same & (c <= r), 1.0, 0.0).astype(BF16)
    upper = jnp.where(same & (c > r), 1.0, 0.0).astype(BF16)
    nck = cb // GLA_CHUNK
    sr = lax.broadcasted_iota(jnp.int32, (SUBLANES, cb), 0)
    sc = lax.broadcasted_iota(jnp.int32, (SUBLANES, cb), 1)
    chunk_sum = jnp.where((sc // GLA_CHUNK) == sr, 1.0, 0.0).astype(BF16)

    gq = z[:, OFF_GQ:OFF_GQ + GLA_KEY_WIDTH] * (GLA_DK ** -0.5)
    gk = z[:, OFF_GK:OFF_GK + GLA_KEY_WIDTH]
    for blk in range(tm // cb):
        rows = slice(blk * cb, (blk + 1) * cb)
        la = log_a[rows]
        b = _split_dot(lower, la)
        rest = _split_dot(upper, la)
        tot = _split_dot(chunk_sum, la)
        qd_ref[rows, :] = (gq[rows] * jnp.exp(b)).astype(BF16)
        ki_ref[rows, :] = (gk[rows] * jnp.exp(-b)).astype(BF16)
        ke_ref[rows, :] = (gk[rows] * jnp.exp(rest)).astype(BF16)
        dec_ref[blk * nck:(blk + 1) * nck, :] = jnp.exp(tot[:nck])
    yield

    cos = cos_ref[...]
    sin = sin_ref[...]
    lane = lax.broadcasted_iota(jnp.int32, (tm, LANES), 1)
    low64 = lane < MLA_ROPE

    scale = MLA_QK ** -0.5 * LOG2E
    one_col = jnp.where(lane == MLA_ROPE, 1.0, 0.0)
    nope_w = MLA_HEADS * MLA_NOPE
    for pair in range(MLA_HEADS // 2):
        t = qraw[:, nope_w + pair * LANES:nope_w + (pair + 1) * LANES]
        t2 = t * t
        ssq_lo = jnp.sum(jnp.where(low64, t2, 0.0), axis=-1, keepdims=True)
        ssq_hi = jnp.sum(jnp.where(low64, 0.0, t2), axis=-1, keepdims=True)
        tg = t * qgr_ref[...]
        rot = tg * cos + _swap_halves(tg) * sin
        for sub in range(2):
            hd = 2 * pair + sub
            qn = qraw[:, hd * MLA_NOPE:(hd + 1) * MLA_NOPE]
            ssq = jnp.sum(qn * qn, axis=-1, keepdims=True) + (ssq_lo if sub == 0 else ssq_hi)
            rinv = lax.rsqrt(ssq * (1.0 / MLA_QK) + EPS) * scale
            q_ref[hd, :, 0:LANES] = (qn * rinv * qgn_ref[...]).astype(BF16)
            rsel = rot if sub == 0 else pltpu.roll(rot, MLA_ROPE, axis=1)
            q_ref[hd, :, LANES:QK_PAD] = jnp.where(low64, rsel * rinv, one_col).astype(BF16)

    kpe2 = misc * misc
    ssq_pe = jnp.sum(jnp.where(low64, kpe2, 0.0), axis=-1, keepdims=True)
    kg = misc * kgr_ref[...]
    krot = kg * cos + _swap_halves(kg) * sin
    for hd in range(MLA_HEADS):
        kn = kvraw[:, hd * MLA_NOPE:(hd + 1) * MLA_NOPE]
        ssq = jnp.sum(kn * kn, axis=-1, keepdims=True) + ssq_pe
        rinv = lax.rsqrt(ssq * (1.0 / MLA_QK) + EPS)
        k_ref[hd, :, 0:LANES] = (kn * rinv * kgn_ref[...]).astype(BF16)
        k_ref[hd, :, LANES:QK_PAD] = (krot * rinv + kbias_ref[...]).astype(BF16)
        v_ref[hd, :, :] = kvraw[:, nope_w + hd * MLA_DV:nope_w + (hd + 1) * MLA_DV].astype(BF16)


def _gla_kernel(qd_ref, ki_ref, ke_ref, v_ref, dec_ref, gate_ref, g_ref, o_ref, state_ref, *, steps_per_seq):
    @pl.when(pl.program_id(0) % steps_per_seq == 0)
    def _():
        state_ref[...] = jnp.zeros_like(state_ref)

    for _ in _gla_chunks(qd_ref, ki_ref, ke_ref, v_ref, dec_ref, gate_ref, g_ref, o_ref, state_ref):
        pass


def _gla_chunks(qd_ref, ki_ref, ke_ref, v_ref, dec_ref, gate_ref, g_ref, o_ref, state_ref):
    tm = qd_ref.shape[0]

    C = GLA_CHUNK
    klane_head = lax.broadcasted_iota(jnp.int32, (C, GLA_KEY_WIDTH), 1) // GLA_DK
    vlane_head = lax.broadcasted_iota(jnp.int32, (C, GLA_WIDTH), 1) // GLA_DV
    row = lax.broadcasted_iota(jnp.int32, (C, GLA_KEY_WIDTH), 0)
    col_in_head = lax.broadcasted_iota(jnp.int32, (C, GLA_KEY_WIDTH), 1) % GLA_DK
    causal = row >= col_in_head
    pair_rows = 2 * GLA_DV
    own = ((lax.broadcasted_iota(jnp.int32, (pair_rows, LANES), 0) // GLA_DV)
           == (lax.broadcasted_iota(jnp.int32, (pair_rows, LANES), 1) // GLA_DK))
    nt = (((1,), (1,)), ((), ()))
    tn = (((0,), (0,)), ((), ()))
    g = g_ref[...]
    n_pairs = GLA_HEADS // 2

    for ck in range(tm // C):
        rows = slice(ck * C, (ck + 1) * C)
        qd = qd_ref[rows, :]
        ki = ki_ref[rows, :]
        ke = ke_ref[rows, :]
        v = v_ref[rows, :]
        upd = [lax.dot_general(v[:, p * pair_rows:(p + 1) * pair_rows], ke[:, p * LANES:(p + 1) * LANES],
                               tn, preferred_element_type=F32) for p in range(n_pairs)]
        zk = jnp.zeros_like(ki)
        zv = jnp.zeros_like(v)
        kstack = jnp.concatenate([jnp.where(klane_head == hd, ki, zk) for hd in range(GLA_HEADS)], axis=0)
        vstack = jnp.concatenate([jnp.where(vlane_head == hd, v, zv) for hd in range(GLA_HEADS)], axis=0)
        a = lax.dot_general(qd, kstack, nt, preferred_element_type=F32)
        a = jnp.where(causal, a, 0.0).astype(BF16)
        o = jnp.dot(a, vstack, preferred_element_type=F32)
        state = state_ref[...]
        o = o + jnp.concatenate(
            [lax.dot_general(qd[:, p * LANES:(p + 1) * LANES],
                             state[p * pair_rows:(p + 1) * pair_rows].astype(BF16),
                             nt, preferred_element_type=F32) for p in range(n_pairs)], axis=1)
        for p in range(n_pairs):
            srows = slice(p * pair_rows, (p + 1) * pair_rows)
            state_ref[srows, :] = (state[srows] * dec_ref[ck:ck + 1, p * LANES:(p + 1) * LANES]
                                   + jnp.where(own, upd[p], 0.0))
        for hd in range(GLA_HEADS):
            cols = slice(hd * GLA_DV, (hd + 1) * GLA_DV)
            oh = o[:, cols]
            on = oh * lax.rsqrt(jnp.mean(oh * oh, axis=-1, keepdims=True) + EPS) * g
            o_ref[rows, cols] = (on * gate_ref[rows, cols].astype(F32)).astype(BF16)
        yield


def _attn_kernel(q_ref, k_ref, v_ref, gate_ref, o_ref, m_ref, l_ref, acc_ref):
    tq = q_ref.shape[0]
    tk = ATTN_TK
    i = pl.program_id(2)
    q = q_ref[...]
    nt = (((1,), (1,)), ((), ()))
    m_ref[...] = jnp.full_like(m_ref, -jnp.inf)
    l_ref[...] = jnp.zeros_like(l_ref)
    acc_ref[...] = jnp.zeros_like(acc_ref)

    def step(j, masked):
        start = pl.multiple_of(j * tk, tk)
        k = k_ref[pl.ds(start, tk), :]
        v = v_ref[pl.ds(start, tk), :]
        s = lax.dot_general(q, k, nt, preferred_element_type=F32)
        if masked:
            rr = lax.broadcasted_iota(jnp.int32, s.shape, 0)
            cc = lax.broadcasted_iota(jnp.int32, s.shape, 1)
            s = jnp.where(cc <= rr, s, -jnp.inf)
        m_prev = m_ref[...]
        m_new = jnp.maximum(m_prev, jnp.max(s, axis=-1, keepdims=True))
        alpha = jnp.exp2(m_prev - m_new)
        p = jnp.exp2(s - m_new)
        l_ref[...] = alpha * l_ref[...] + jnp.sum(p, axis=-1, keepdims=True)
        acc_ref[...] = alpha * acc_ref[...] + jnp.dot(p.astype(BF16), v, preferred_element_type=F32)
        m_ref[...] = m_new

    def body(j, carry):
        step(j, False)
        return carry

    lax.fori_loop(0, i * (tq // tk), body, 0)
    step(i * (tq // tk), True)
    o = acc_ref[...] / l_ref[...]
    o_ref[...] = (o * gate_ref[...].astype(F32)).astype(BF16)


def _attn_bounded_kernel(q_ref, k_ref, v_ref, gate_ref, o_ref):
    t = ATTN_TQ
    half = t // 2
    n_tiles = k_ref.shape[0] // t
    nt = (((1,), (1,)), ((), ()))
    ones_col = jnp.where(lax.broadcasted_iota(jnp.int32, (t, LANES), 1) == 0, 1.0, 0.0).astype(BF16)
    causal_a = (lax.broadcasted_iota(jnp.int32, (half, half), 1)
                <= lax.broadcasted_iota(jnp.int32, (half, half), 0))
    causal_b = (lax.broadcasted_iota(jnp.int32, (half, t), 1)
                <= lax.broadcasted_iota(jnp.int32, (half, t), 0) + half)

    def probs(q, k):
        return jnp.exp2(lax.dot_general(q, k, nt, preferred_element_type=F32))

    def key_tile(q, j):
        k = k_ref[j * t:(j + 1) * t, :]
        v_aug = jnp.concatenate([v_ref[j * t:(j + 1) * t, :], ones_col], axis=1)
        return jnp.dot(probs(q, k).astype(BF16), v_aug, preferred_element_type=F32)

    def diagonal_tile(q, j):
        k = k_ref[j * t:(j + 1) * t, :]
        v_aug = jnp.concatenate([v_ref[j * t:(j + 1) * t, :], ones_col], axis=1)
        p_a = jnp.where(causal_a, probs(q[:half], k[:half]), 0.0).astype(BF16)
        p_b = jnp.where(causal_b, probs(q[half:], k), 0.0).astype(BF16)
        return jnp.concatenate([jnp.dot(p_a, v_aug[:half], preferred_element_type=F32),
                                jnp.dot(p_b, v_aug, preferred_element_type=F32)], axis=0)

    for idx in range(n_tiles):
        rows = slice(idx * t, (idx + 1) * t)
        q = q_ref[rows, :]
        acc = diagonal_tile(q, idx)
        for j in range(idx):
            acc = acc + key_tile(q, j)
        o = acc[:, :MLA_DV] * (1.0 / acc[:, MLA_DV:MLA_DV + 1])
        o_ref[rows, :] = (o * gate_ref[rows, :].astype(F32)).astype(BF16)


def _out_kernel(x_ref, og_ref, om_ref, w_ref, o_ref):
    w = w_ref[...]
    o_ref[...] = (x_ref[...]
                  + jnp.dot(og_ref[...], w[:GLA_WIDTH], preferred_element_type=F32)
                  + jnp.dot(om_ref[...], w[GLA_WIDTH:], preferred_element_type=F32))


def _layer(shape, l):
    return pl.BlockSpec((None,) + tuple(shape), lambda *idx: (l,) + (0,) * len(shape),
                        pipeline_mode=pl.Buffered(1))


def _params(n_axes):
    return pltpu.CompilerParams(dimension_semantics=("arbitrary",) * n_axes,
                                vmem_limit_bytes=VMEM_LIMIT)


def _proj_call(x2, prev, cos_t, sin_t, w, l, B, S):
    T = x2.shape[0]
    tm = PROJ_TM
    ns = S // tm
    n = T // tm
    fuse_out = prev is not None
    cur = lambda i: jnp.minimum(i, n - 1)
    lag = lambda i: jnp.maximum(i - 1, 0)
    rows = lambda w, at: pl.BlockSpec((tm, w), lambda i: (at(i), 0))
    heads = lambda w: pl.BlockSpec((None, MLA_HEADS, tm, w), lambda i: (lag(i) // ns, 0, lag(i) % ns, 0))
    front_shapes = (
        jax.ShapeDtypeStruct((T, GLA_WIDTH), BF16),
        jax.ShapeDtypeStruct((T, GLA_WIDTH), BF16),
        jax.ShapeDtypeStruct((T, MLA_WIDTH), BF16),
    )
    front_out_specs = (rows(GLA_WIDTH, cur), rows(GLA_WIDTH, cur), rows(MLA_WIDTH, cur))
    post_shapes = (
        jax.ShapeDtypeStruct((T, GLA_KEY_WIDTH), BF16),
        jax.ShapeDtypeStruct((T, GLA_KEY_WIDTH), BF16),
        jax.ShapeDtypeStruct((T, GLA_KEY_WIDTH), BF16),
        jax.ShapeDtypeStruct((T // GLA_CHUNK, GLA_KEY_WIDTH), F32),
        jax.ShapeDtypeStruct((B, MLA_HEADS, S, QK_PAD), BF16),
        jax.ShapeDtypeStruct((B, MLA_HEADS, S, QK_PAD), BF16),
        jax.ShapeDtypeStruct((B, MLA_HEADS, S, MLA_DV), BF16),
    )
    post_specs = (rows(GLA_KEY_WIDTH, lag), rows(GLA_KEY_WIDTH, lag), rows(GLA_KEY_WIDTH, lag),
                  pl.BlockSpec((tm // GLA_CHUNK, GLA_KEY_WIDTH), lambda i: (lag(i), 0)),
                  heads(QK_PAD), heads(QK_PAD), heads(MLA_DV))
    assert len(post_specs) == N_POST_OUT

    front_specs = [rows(D_MODEL, cur)]
    front_args = [x2]
    if fuse_out:
        o_gla, o_mla = prev
        front_specs += [rows(GLA_WIDTH, cur), rows(MLA_WIDTH, cur), _layer((D_MODEL, D_MODEL), l - 1)]
        front_args += [o_gla, o_mla, w["w_out"]]
    front_names = ["norm_g", "w_in", "w_mix"]
    post_names = ["w_misc", "b_gate", "q_norm_g", "w_uq", "kv_norm_g", "w_ukv",
                  "qg_nope", "qg_rope", "kg_nope", "kg_rope", "k_bias"]
    front_specs += [_layer(w[name].shape[1:], l) for name in front_names]
    front_args += [w[name] for name in front_names]
    post_in_specs = [rows(LANES, lag), rows(LANES, lag)] + [_layer(w[name].shape[1:], l) for name in post_names]
    post_in_args = [cos_t, sin_t] + [w[name] for name in post_names]
    assert len(post_in_specs) == N_POST_IN

    out_shape = front_shapes + post_shapes
    out_specs = front_out_specs + post_specs
    if fuse_out:
        out_shape = (jax.ShapeDtypeStruct((T, D_MODEL), F32),) + out_shape
        out_specs = (rows(D_MODEL, cur),) + out_specs
    return pl.pallas_call(
        functools.partial(_proj_kernel, fuse_out=fuse_out),
        grid=(n + 1,), in_specs=front_specs + post_in_specs, out_specs=out_specs, out_shape=out_shape,
        scratch_shapes=[pltpu.VMEM((tm, D_IN_PAD), F32), pltpu.VMEM((tm, D_IN_PAD), F32)],
        compiler_params=_params(1), name="outproj_proj" if fuse_out else "proj",
    )(*front_args, *post_in_args)


def _gla_call(qd, ki, ke, gv, dec, gg, gla_g, l, S):
    T = qd.shape[0]
    tm = GLA_TM
    rows = lambda w: pl.BlockSpec((tm, w), lambda i: (i, 0))
    return pl.pallas_call(
        functools.partial(_gla_kernel, steps_per_seq=S // tm),
        grid=(T // tm,),
        in_specs=[rows(GLA_KEY_WIDTH), rows(GLA_KEY_WIDTH), rows(GLA_KEY_WIDTH), rows(GLA_WIDTH),
                  pl.BlockSpec((tm // GLA_CHUNK, GLA_KEY_WIDTH), lambda i: (i, 0)),
                  rows(GLA_WIDTH), _layer((1, GLA_DV), l)],
        out_specs=rows(GLA_WIDTH),
        out_shape=jax.ShapeDtypeStruct((T, GLA_WIDTH), BF16),
        scratch_shapes=[pltpu.VMEM((GLA_WIDTH, LANES), F32)],
        compiler_params=_params(1), name="gla",
    )(qd, ki, ke, gv, dec, gg, gla_g)


def _attn_call(q, k, v, mg):
    B, H, S, _ = q.shape
    tq = ATTN_TQ
    nq = S // tq
    q_spec = pl.BlockSpec((None, None, tq, QK_PAD), lambda b, h, i: (b, h, i, 0))
    kv_specs = [pl.BlockSpec((None, None, S, QK_PAD), lambda b, h, i: (b, h, 0, 0)),
                pl.BlockSpec((None, None, S, MLA_DV), lambda b, h, i: (b, h, 0, 0))]
    io_spec = pl.BlockSpec((tq, MLA_DV), lambda b, h, i: (b * nq + i, h))
    return pl.pallas_call(
        _attn_kernel, grid=(B, H, nq),
        in_specs=[q_spec] + kv_specs + [io_spec], out_specs=io_spec,
        out_shape=jax.ShapeDtypeStruct(mg.shape, BF16),
        scratch_shapes=[pltpu.VMEM((tq, 1), F32), pltpu.VMEM((tq, 1), F32), pltpu.VMEM((tq, MLA_DV), F32)],
        compiler_params=_params(3), name="attn_online",
    )(q, k, v, mg)


def _attn_bounded_call(q, k, v, mg):
    B, H, S, _ = q.shape
    seq = lambda w: pl.BlockSpec((None, None, S, w), lambda b, h: (b, h, 0, 0))
    io_spec = pl.BlockSpec((S, MLA_DV), lambda b, h: (b, h))
    return pl.pallas_call(
        _attn_bounded_kernel, grid=(B, H),
        in_specs=[seq(QK_PAD), seq(QK_PAD), seq(MLA_DV), io_spec], out_specs=io_spec,
        out_shape=jax.ShapeDtypeStruct(mg.shape, BF16),
        compiler_params=_params(2), name="attn_bounded",
    )(q, k, v, mg)


def _out_call(x2, og, om, w_out, l):
    T = x2.shape[0]
    tm = OUT_TM
    rows = lambda w: pl.BlockSpec((tm, w), lambda i: (i, 0))
    return pl.pallas_call(
        _out_kernel, grid=(T // tm,),
        in_specs=[rows(D_MODEL), rows(GLA_WIDTH), rows(MLA_WIDTH), _layer((D_MODEL, D_MODEL), l)],
        out_specs=rows(D_MODEL),
        out_shape=jax.ShapeDtypeStruct((T, D_MODEL), F32),
        compiler_params=_params(1), name="outproj",
    )(x2, og, om, w_out)


def _prep_weights(norm_g, w_in, w_gla_gate_up, b_gla_gate, gla_norm_g, mla_q_norm_g, w_uq,
                  mla_kv_norm_g, w_ukv, q_head_g, k_head_g, w_out):
    depth = w_in.shape[0]
    o = [0]
    for wdt in (GLA_KEY_WIDTH, GLA_KEY_WIDTH, GLA_WIDTH, GLA_GATE_RANK, GLA_WIDTH,
                MLA_Q_RANK, MLA_KV_RANK, MLA_ROPE, MLA_WIDTH):
        o.append(o[-1] + wdt)
    wt = jnp.swapaxes(w_in, 1, 2).astype(BF16)
    pad = jnp.zeros((depth, LANES - MLA_ROPE - GLA_GATE_RANK, D_MODEL), BF16)
    w_mix = jnp.concatenate([wt[:, o[6]:o[8]], wt[:, o[3]:o[4]], pad], axis=1)
    assert w_mix.shape[1] == PROJ_CHUNK and o[9] == D_IN
    assert [o[0], o[1], o[2], o[2] + PROJ_CHUNK, o[4], o[4] + PROJ_CHUNK, o[5], None, o[8], o[8] + PROJ_CHUNK] \
        == list(SRC_ROW_OF_CHUNK)

    w_misc = jnp.zeros((depth, LANES, GLA_KEY_WIDTH), F32)
    w_misc = w_misc.at[:, MLA_ROPE:MLA_ROPE + GLA_GATE_RANK, :].set(w_gla_gate_up).astype(BF16)

    wq = w_uq.reshape(depth, MLA_Q_RANK, MLA_HEADS, MLA_QK)
    w_uq_p = jnp.concatenate([wq[..., :MLA_NOPE].reshape(depth, MLA_Q_RANK, -1),
                              wq[..., MLA_NOPE:].reshape(depth, MLA_Q_RANK, -1)], axis=-1).astype(BF16)
    wkv = w_ukv.reshape(depth, MLA_KV_RANK, MLA_HEADS, MLA_NOPE + MLA_DV)
    w_ukv_p = jnp.concatenate([wkv[..., :MLA_NOPE].reshape(depth, MLA_KV_RANK, -1),
                               wkv[..., MLA_NOPE:].reshape(depth, MLA_KV_RANK, -1)], axis=-1).astype(BF16)

    zeros64 = jnp.zeros((depth, 1, LANES - MLA_ROPE), F32)
    row = lambda a: a[:, None, :]
    return dict(
        norm_g=row(norm_g), w_in=wt, w_mix=w_mix, w_misc=w_misc, b_gate=row(b_gla_gate),
        gla_g=row(gla_norm_g), q_norm_g=row(mla_q_norm_g), w_uq=w_uq_p,
        kv_norm_g=row(mla_kv_norm_g), w_ukv=w_ukv_p,
        qg_nope=row(q_head_g[:, :MLA_NOPE]),
        qg_rope=jnp.concatenate([row(q_head_g[:, MLA_NOPE:])] * 2, axis=-1),
        kg_nope=row(k_head_g[:, :MLA_NOPE]),
        kg_rope=jnp.concatenate([row(k_head_g[:, MLA_NOPE:]), zeros64], axis=-1),
        w_out=w_out.astype(BF16),
    )


def _score_bounds(q_head_g, k_head_g):
    bound = (MLA_QK ** 0.5 * LOG2E * SCORE_BOUND_SLACK
             * jnp.max(jnp.abs(q_head_g), axis=-1) * jnp.max(jnp.abs(k_head_g), axis=-1))
    use = bound <= SCORE_BOUND_MAX_LOG2
    depth = q_head_g.shape[0]
    bias = jnp.zeros((depth, 1, LANES), F32).at[:, 0, MLA_ROPE].set(jnp.where(use, -bound, 0.0))
    return use, bias


def _rope_tables(positions):
    inv_freq = ROPE_THETA ** (-jnp.arange(0, MLA_ROPE, 2, dtype=F32) / MLA_ROPE)
    reps = LANES // inv_freq.shape[0]
    inv_freq = jnp.tile(inv_freq, reps)
    sign = jnp.tile(jnp.repeat(jnp.array([-1.0, 1.0], F32), MLA_ROPE // 2), LANES // MLA_ROPE)
    ang = positions.astype(F32).reshape(-1, 1) * inv_freq
    return jnp.cos(ang), jnp.sin(ang) * sign


def kernel(x, positions, norm_g, w_in, w_gla_gate_up, b_gla_gate, gla_norm_g, mla_q_norm_g, w_uq,
           mla_kv_norm_g, w_ukv, q_head_g, k_head_g, w_out):
    B, S, D = x.shape
    assert D == D_MODEL and S % max(PROJ_TM, GLA_TM, ATTN_TQ, OUT_TM) == 0 and ATTN_TQ % ATTN_TK == 0
    depth = w_in.shape[0]
    weights = _prep_weights(norm_g, w_in, w_gla_gate_up, b_gla_gate, gla_norm_g, mla_q_norm_g, w_uq,
                            mla_kv_norm_g, w_ukv, q_head_g, k_head_g, w_out)
    use_bounded, weights["k_bias"] = _score_bounds(q_head_g, k_head_g)
    cos_t, sin_t = _rope_tables(positions)
    x2 = x.reshape(B * S, D)
    prev = None
    for l in range(depth):
        outs = _proj_call(x2, prev, cos_t, sin_t, weights, l, B, S)
        if prev is not None:
            x2, outs = outs[0], outs[1:]
        gv, gg, mg, qd, ki, ke, dec, q, k, v = outs
        o_gla = _gla_call(qd, ki, ke, gv, dec, gg, weights["gla_g"], l, S)
        o_mla = lax.cond(use_bounded[l], _attn_bounded_call, _attn_call, q, k, v, mg)
        prev = (o_gla, o_mla)
    x2 = _out_call(x2, *prev, weights["w_out"], depth - 1)
    return x2.reshape(B, S, D)
```

```python
import functools

import jax
import jax.numpy as jnp
from jax import lax
from jax.experimental import pallas as pl
from jax.experimental.pallas import tpu as pltpu

F32 = jnp.float32
BF16 = jnp.bfloat16

D_MODEL = 1024
GLA_HEADS = 4
GLA_DK = 64
GLA_DV = 128
GLA_KEY_WIDTH = GLA_HEADS * GLA_DK
GLA_WIDTH = GLA_HEADS * GLA_DV
GLA_GATE_RANK = 16
GLA_GATE_TEMP = 16.0
GLA_CHUNK = 64
MLA_HEADS = 4
MLA_NOPE = 128
MLA_ROPE = 64
MLA_QK = MLA_NOPE + MLA_ROPE
MLA_DV = 128
MLA_WIDTH = MLA_HEADS * MLA_DV
MLA_Q_RANK = 256
MLA_KV_RANK = 128
ROPE_THETA = 10000.0
EPS = 1e-6
LOG2E = 1.4426950408889634
SCORE_BOUND_MAX_LOG2 = 40.0 * LOG2E
SCORE_BOUND_SLACK = 1.02

LANES = 128
SUBLANES = 8
QK_PAD = 2 * LANES

OFF_GQ = 0
OFF_GK = 256
OFF_GV = 512
OFF_GGATE = 1024
OFF_CQ = 1536
OFF_CKV = 1792
OFF_MISC = 1920
OFF_MGATE = 2048
D_IN_PAD = 2560
D_IN = 2512
SRC_ROW_OF_CHUNK = (0, 256, 512, 768, 1040, 1296, 1552, None, 2000, 2256)

PROJ_TM = 512
GLA_TM = 2048
ATTN_TQ = 512
ATTN_TK = 512
ATTN_HEADS_PER_STEP = 2
OUT_TM = 1024
CUMSUM_BLOCK = 256
PROJ_CHUNK = 256
VMEM_LIMIT = 56 * 1024 * 1024


def _rms(t, g):
    return t * lax.rsqrt(jnp.mean(t * t, axis=-1, keepdims=True) + EPS) * g


def _silu(t):
    return t * jax.nn.sigmoid(t)


def _split_dot(mat_bf16, t):
    hi = t.astype(BF16)
    lo = (t - hi.astype(F32)).astype(BF16)
    return (jnp.dot(mat_bf16, hi, preferred_element_type=F32)
            + jnp.dot(mat_bf16, lo, preferred_element_type=F32))


def _swap_halves(t):
    lane = lax.broadcasted_iota(jnp.int32, t.shape, 1)
    return jnp.where(lane % MLA_ROPE < MLA_ROPE // 2,
                     pltpu.roll(t, LANES - MLA_ROPE // 2, axis=1),
                     pltpu.roll(t, MLA_ROPE // 2, axis=1))


N_POST_IN = 13
N_POST_OUT = 7
POST_STAGE_AT = (0, 5)


def _proj_kernel(*refs, fuse_out):
    refs = list(refs)
    x_ref = refs.pop(0)
    if fuse_out:
        og_ref, om_ref, wout_ref = refs[:3]
        del refs[:3]
    ng_ref, win_ref, wmix_ref = refs[:3]
    post_in = refs[3:3 + N_POST_IN]
    del refs[:3 + N_POST_IN]
    if fuse_out:
        xo_ref = refs.pop(0)
    gv_ref, gg_ref, mg_ref = refs[:3]
    post_out = refs[3:3 + N_POST_OUT]
    z0_ref, z1_ref = refs[3 + N_POST_OUT:]
    i = pl.program_id(0)
    nt = (((1,), (1,)), ((), ()))
    direct = {}
    for ref, off, fn in ((gv_ref, OFF_GV, None), (gg_ref, OFF_GGATE, _silu), (mg_ref, OFF_MGATE, _silu)):
        for k in range(ref.shape[1] // PROJ_CHUNK):
            direct[off // PROJ_CHUNK + k] = (ref, k, fn)

    @pl.when(i == 0)
    def _():
        z1_ref[...] = jnp.zeros_like(z1_ref)

    def step(z_read, z_write):
        post = _proj_post(z_read, *post_in, *post_out)
        x = x_ref[...]
        if fuse_out:
            w = wout_ref[...]
            x = (x + jnp.dot(og_ref[...], w[:GLA_WIDTH], preferred_element_type=F32)
                 + jnp.dot(om_ref[...], w[GLA_WIDTH:], preferred_element_type=F32))
            xo_ref[...] = x
        h = _rms(x, ng_ref[...]).astype(BF16)
        order = sorted(range(D_IN_PAD // PROJ_CHUNK), key=lambda c: ((c not in direct) == fuse_out, c))
        for pos, c in enumerate(order):
            if pos in POST_STAGE_AT:
                next(post)
            cols = slice(c * PROJ_CHUNK, (c + 1) * PROJ_CHUNK)
            src = SRC_ROW_OF_CHUNK[c]
            w_c = wmix_ref[...] if src is None else win_ref[src:src + PROJ_CHUNK, :]
            zc = lax.dot_general(h, w_c, nt, preferred_element_type=F32)
            if c in direct:
                ref, k, fn = direct[c]
                ref[:, k * PROJ_CHUNK:(k + 1) * PROJ_CHUNK] = (zc if fn is None else fn(zc)).astype(BF16)
            else:
                z_write[:, cols] = zc
        for _ in post:
            pass

    @pl.when(i % 2 == 0)
    def _():
        step(z1_ref, z0_ref)

    @pl.when(i % 2 == 1)
    def _():
        step(z0_ref, z1_ref)


def _proj_post(z, cos_ref, sin_ref, wmisc_ref, bgate_ref,
               qng_ref, wuq_ref, kvng_ref, wukv_ref, qgn_ref, qgr_ref, kgn_ref, kgr_ref, kbias_ref,
               qd_ref, ki_ref, ke_ref, dec_ref, q_ref, k_ref, v_ref):
    tm = z.shape[0]

    misc = z[:, OFF_MISC:OFF_MISC + LANES]
    logit = jnp.dot(misc.astype(BF16), wmisc_ref[...], preferred_element_type=F32) + bgate_ref[...]
    cqn = _rms(z[:, OFF_CQ:OFF_CQ + MLA_Q_RANK], qng_ref[...]).astype(BF16)
    qraw = jnp.dot(cqn, wuq_ref[...], preferred_element_type=F32)
    ckvn = _rms(z[:, OFF_CKV:OFF_CKV + MLA_KV_RANK], kvng_ref[...]).astype(BF16)
    kvraw = jnp.dot(ckvn, wukv_ref[...], preferred_element_type=F32)
    yield

    log_a = (jnp.minimum(logit, 0.0) - jnp.log(1.0 + jnp.exp(-jnp.abs(logit)))) * (1.0 / GLA_GATE_TEMP)

    cb = CUMSUM_BLOCK
    r = lax.broadcasted_iota(jnp.int32, (cb, cb), 0)
    c = lax.broadcasted_iota(jnp.int32, (cb, cb), 1)
    same = (r // GLA_CHUNK) == (c // GLA_CHUNK)
    lower = jnp.where(same & (c <= r), 1.0, 0.0).astype(BF16)
    upper = jnp.where(same & (c > r), 1.0, 0.0).astype(BF16)
    nck = cb // GLA_CHUNK
    sr = lax.broadcasted_iota(jnp.int32, (SUBLANES, cb), 0)
    sc = lax.broadcasted_iota(jnp.int32, (SUBLANES, cb), 1)
    chunk_sum = jnp.where((sc // GLA_CHUNK) == sr, 1.0, 0.0).astype(BF16)

    gq = z[:, OFF_GQ:OFF_GQ + GLA_KEY_WIDTH] * (GLA_DK ** -0.5)
    gk = z[:, OFF_GK:OFF_GK + GLA_KEY_WIDTH]
    for blk in range(tm // cb):
        rows = slice(blk * cb, (blk + 1) * cb)
        la = log_a[rows]
        b = _split_dot(lower, la)
        rest = _split_dot(upper, la)
        tot = _split_dot(chunk_sum, la)
        qd_ref[rows, :] = (gq[rows] * jnp.exp(b)).astype(BF16)
        ki_ref[rows, :] = (gk[rows] * jnp.exp(-b)).astype(BF16)
        ke_ref[rows, :] = (gk[rows] * jnp.exp(rest)).astype(BF16)
        dec_ref[blk * nck:(blk + 1) * nck, :] = jnp.exp(tot[:nck])
    yield

    cos = cos_ref[...]
    sin = sin_ref[...]
    lane = lax.broadcasted_iota(jnp.int32, (tm, LANES), 1)
    low64 = lane < MLA_ROPE

    scale = MLA_QK ** -0.5 * LOG2E
    one_col = jnp.where(lane == MLA_ROPE, 1.0, 0.0)
    nope_w = MLA_HEADS * MLA_NOPE
    for pair in range(MLA_HEADS // 2):
        t = qraw[:, nope_w + pair * LANES:nope_w + (pair + 1) * LANES]
        t2 = t * t
        ssq_lo = jnp.sum(jnp.where(low64, t2, 0.0), axis=-1, keepdims=True)
        ssq_hi = jnp.sum(jnp.where(low64, 0.0, t2), axis=-1, keepdims=True)
        tg = t * qgr_ref[...]
        rot = tg * cos + _swap_halves(tg) * sin
        for sub in range(2):
            hd = 2 * pair + sub
            qn = qraw[:, hd * MLA_NOPE:(hd + 1) * MLA_NOPE]
            ssq = jnp.sum(qn * qn, axis=-1, keepdims=True) + (ssq_lo if sub == 0 else ssq_hi)
            rinv = lax.rsqrt(ssq * (1.0 / MLA_QK) + EPS) * scale
            q_ref[hd, :, 0:LANES] = (qn * rinv * qgn_ref[...]).astype(BF16)
            rsel = rot if sub == 0 else pltpu.roll(rot, MLA_ROPE, axis=1)
            q_ref[hd, :, LANES:QK_PAD] = jnp.where(low64, rsel * rinv, one_col).astype(BF16)

    kpe2 = misc * misc
    ssq_pe = jnp.sum(jnp.where(low64, kpe2, 0.0), axis=-1, keepdims=True)
    kg = misc * kgr_ref[...]
    krot = kg * cos + _swap_halves(kg) * sin
    for hd in range(MLA_HEADS):
        kn = kvraw[:, hd * MLA_NOPE:(hd + 1) * MLA_NOPE]
        ssq = jnp.sum(kn * kn, axis=-1, keepdims=True) + ssq_pe
        rinv = lax.rsqrt(ssq * (1.0 / MLA_QK) + EPS)
        k_ref[hd, :, 0:LANES] = (kn * rinv * kgn_ref[...]).astype(BF16)
        k_ref[hd, :, LANES:QK_PAD] = (krot * rinv + kbias_ref[...]).astype(BF16)
        v_ref[hd, :, :] = kvraw[:, nope_w + hd * MLA_DV:nope_w + (hd + 1) * MLA_DV].astype(BF16)


def _gla_kernel(qd_ref, ki_ref, ke_ref, v_ref, dec_ref, gate_ref, g_ref, o_ref, state_ref, *, steps_per_seq):
    @pl.when(pl.program_id(0) % steps_per_seq == 0)
    def _():
        state_ref[...] = jnp.zeros_like(state_ref)

    for _ in _gla_chunks(qd_ref, ki_ref, ke_ref, v_ref, dec_ref, gate_ref, g_ref, o_ref, state_ref):
        pass


def _gla_chunks(qd_ref, ki_ref, ke_ref, v_ref, dec_ref, gate_ref, g_ref, o_ref, state_ref):
    tm = qd_ref.shape[0]

    C = GLA_CHUNK
    klane_head = lax.broadcasted_iota(jnp.int32, (C, GLA_KEY_WIDTH), 1) // GLA_DK
    vlane_head = lax.broadcasted_iota(jnp.int32, (C, GLA_WIDTH), 1) // GLA_DV
    row = lax.broadcasted_iota(jnp.int32, (C, GLA_KEY_WIDTH), 0)
    col_in_head = lax.broadcasted_iota(jnp.int32, (C, GLA_KEY_WIDTH), 1) % GLA_DK
    causal = row >= col_in_head
    pair_rows = 2 * GLA_DV
    own = ((lax.broadcasted_iota(jnp.int32, (pair_rows, LANES), 0) // GLA_DV)
           == (lax.broadcasted_iota(jnp.int32, (pair_rows, LANES), 1) // GLA_DK))
    nt = (((1,), (1,)), ((), ()))
    tn = (((0,), (0,)), ((), ()))
    g = g_ref[...]
    n_pairs = GLA_HEADS // 2

    for ck in range(tm // C):
        rows = slice(ck * C, (ck + 1) * C)
        qd = qd_ref[rows, :]
        ki = ki_ref[rows, :]
        ke = ke_ref[rows, :]
        v = v_ref[rows, :]
        upd = [lax.dot_general(v[:, p * pair_rows:(p + 1) * pair_rows], ke[:, p * LANES:(p + 1) * LANES],
                               tn, preferred_element_type=F32) for p in range(n_pairs)]
        zk = jnp.zeros_like(ki)
        zv = jnp.zeros_like(v)
        kstack = jnp.concatenate([jnp.where(klane_head == hd, ki, zk) for hd in range(GLA_HEADS)], axis=0)
        vstack = jnp.concatenate([jnp.where(vlane_head == hd, v, zv) for hd in range(GLA_HEADS)], axis=0)
        a = lax.dot_general(qd, kstack, nt, preferred_element_type=F32)
        a = jnp.where(causal, a, 0.0).astype(BF16)
        o = jnp.dot(a, vstack, preferred_element_type=F32)
        state = state_ref[...]
        o = o + jnp.concatenate(
            [lax.dot_general(qd[:, p * LANES:(p + 1) * LANES],
                             state[p * pair_rows:(p + 1) * pair_rows].astype(BF16),
                             nt, preferred_element_type=F32) for p in range(n_pairs)], axis=1)
        for p in range(n_pairs):
            srows = slice(p * pair_rows, (p + 1) * pair_rows)
            state_ref[srows, :] = (state[srows] * dec_ref[ck:ck + 1, p * LANES:(p + 1) * LANES]
                                   + jnp.where(own, upd[p], 0.0))
        for hd in range(GLA_HEADS):
            cols = slice(hd * GLA_DV, (hd + 1) * GLA_DV)
            oh = o[:, cols]
            on = oh * lax.rsqrt(jnp.mean(oh * oh, axis=-1, keepdims=True) + EPS) * g
            o_ref[rows, cols] = (on * gate_ref[rows, cols].astype(F32)).astype(BF16)
        yield


def _attn_kernel(q_ref, k_ref, v_ref, gate_ref, o_ref, m_ref, l_ref, acc_ref):
    tq = q_ref.shape[0]
    tk = ATTN_TK
    i = pl.program_id(2)
    q = q_ref[...]
    nt = (((1,), (1,)), ((), ()))
    m_ref[...] = jnp.full_like(m_ref, -jnp.inf)
    l_ref[...] = jnp.zeros_like(l_ref)
    acc_ref[...] = jnp.zeros_like(acc_ref)

    def step(j, masked):
        start = pl.multiple_of(j * tk, tk)
        k = k_ref[pl.ds(start, tk), :]
        v = v_ref[pl.ds(start, tk), :]
        s = lax.dot_general(q, k, nt, preferred_element_type=F32)
        if masked:
            rr = lax.broadcasted_iota(jnp.int32, s.shape, 0)
            cc = lax.broadcasted_iota(jnp.int32, s.shape, 1)
            s = jnp.where(cc <= rr, s, -jnp.inf)
        m_prev = m_ref[...]
        m_new = jnp.maximum(m_prev, jnp.max(s, axis=-1, keepdims=True))
        alpha = jnp.exp2(m_prev - m_new)
        p = jnp.exp2(s - m_new)
        l_ref[...] = alpha * l_ref[...] + jnp.sum(p, axis=-1, keepdims=True)
        acc_ref[...] = alpha * acc_ref[...] + jnp.dot(p.astype(BF16), v, preferred_element_type=F32)
        m_ref[...] = m_new

    def body(j, carry):
        step(j, False)
        return carry

    lax.fori_loop(0, i * (tq // tk), body, 0)
    step(i * (tq // tk), True)
    o = acc_ref[...] / l_ref[...]
    o_ref[...] = (o * gate_ref[...].astype(F32)).astype(BF16)


def _attn_bounded_kernel(q_ref, k_ref, v_ref, gate_ref, o_ref):
    t = ATTN_TQ
    half = t // 2
    n_tiles = k_ref.shape[1] // t
    nt = (((1,), (1,)), ((), ()))
    ones_col = jnp.where(lax.broadcasted_iota(jnp.int32, (t, LANES), 1) == 0, 1.0, 0.0).astype(BF16)
    causal_a = (lax.broadcasted_iota(jnp.int32, (half, half), 1)
                <= lax.broadcasted_iota(jnp.int32, (half, half), 0))
    causal_b = (lax.broadcasted_iota(jnp.int32, (half, t), 1)
                <= lax.broadcasted_iota(jnp.int32, (half, t), 0) + half)

    def probs(q, k):
        return jnp.exp2(lax.dot_general(q, k, nt, preferred_element_type=F32))

    def key_tile(hd, q, j):
        k = k_ref[hd, j * t:(j + 1) * t, :]
        v_aug = jnp.concatenate([v_ref[hd, j * t:(j + 1) * t, :], ones_col], axis=1)
        return jnp.dot(probs(q, k).astype(BF16), v_aug, preferred_element_type=F32)

    def diagonal_tile(hd, q, j):
        k = k_ref[hd, j * t:(j + 1) * t, :]
        v_aug = jnp.concatenate([v_ref[hd, j * t:(j + 1) * t, :], ones_col], axis=1)
        p_a = jnp.where(causal_a, probs(q[:half], k[:half]), 0.0).astype(BF16)
        p_b = jnp.where(causal_b, probs(q[half:], k), 0.0).astype(BF16)
        return jnp.concatenate([jnp.dot(p_a, v_aug[:half], preferred_element_type=F32),
                                jnp.dot(p_b, v_aug, preferred_element_type=F32)], axis=0)

    for hd in range(q_ref.shape[0]):
        cols = slice(hd * MLA_DV, (hd + 1) * MLA_DV)
        for idx in range(n_tiles):
            rows = slice(idx * t, (idx + 1) * t)
            q = q_ref[hd, rows, :]
            acc = diagonal_tile(hd, q, idx)
            for j in range(idx):
                acc = acc + key_tile(hd, q, j)
            o = acc[:, :MLA_DV] * (1.0 / acc[:, MLA_DV:MLA_DV + 1])
            o_ref[rows, cols] = (o * gate_ref[rows, cols].astype(F32)).astype(BF16)


def _out_kernel(x_ref, og_ref, om_ref, w_ref, o_ref):
    w = w_ref[...]
    o_ref[...] = (x_ref[...]
                  + jnp.dot(og_ref[...], w[:GLA_WIDTH], preferred_element_type=F32)
                  + jnp.dot(om_ref[...], w[GLA_WIDTH:], preferred_element_type=F32))


def _layer(shape, l):
    return pl.BlockSpec((None,) + tuple(shape), lambda *idx: (l,) + (0,) * len(shape),
                        pipeline_mode=pl.Buffered(1))


def _params(n_axes):
    return pltpu.CompilerParams(dimension_semantics=("arbitrary",) * n_axes,
                                vmem_limit_bytes=VMEM_LIMIT)


def _proj_call(x2, prev, cos_t, sin_t, w, l, B, S):
    T = x2.shape[0]
    tm = PROJ_TM
    ns = S // tm
    n = T // tm
    fuse_out = prev is not None
    cur = lambda i: jnp.minimum(i, n - 1)
    lag = lambda i: jnp.maximum(i - 1, 0)
    rows = lambda w, at: pl.BlockSpec((tm, w), lambda i: (at(i), 0))
    heads = lambda w: pl.BlockSpec((None, MLA_HEADS, tm, w), lambda i: (lag(i) // ns, 0, lag(i) % ns, 0))
    front_shapes = (
        jax.ShapeDtypeStruct((T, GLA_WIDTH), BF16),
        jax.ShapeDtypeStruct((T, GLA_WIDTH), BF16),
        jax.ShapeDtypeStruct((T, MLA_WIDTH), BF16),
    )
    front_out_specs = (rows(GLA_WIDTH, cur), rows(GLA_WIDTH, cur), rows(MLA_WIDTH, cur))
    post_shapes = (
        jax.ShapeDtypeStruct((T, GLA_KEY_WIDTH), BF16),
        jax.ShapeDtypeStruct((T, GLA_KEY_WIDTH), BF16),
        jax.ShapeDtypeStruct((T, GLA_KEY_WIDTH), BF16),
        jax.ShapeDtypeStruct((T // GLA_CHUNK, GLA_KEY_WIDTH), F32),
        jax.ShapeDtypeStruct((B, MLA_HEADS, S, QK_PAD), BF16),
        jax.ShapeDtypeStruct((B, MLA_HEADS, S, QK_PAD), BF16),
        jax.ShapeDtypeStruct((B, MLA_HEADS, S, MLA_DV), BF16),
    )
    post_specs = (rows(GLA_KEY_WIDTH, lag), rows(GLA_KEY_WIDTH, lag), rows(GLA_KEY_WIDTH, lag),
                  pl.BlockSpec((tm // GLA_CHUNK, GLA_KEY_WIDTH), lambda i: (lag(i), 0)),
                  heads(QK_PAD), heads(QK_PAD), heads(MLA_DV))
    assert len(post_specs) == N_POST_OUT

    front_specs = [rows(D_MODEL, cur)]
    front_args = [x2]
    if fuse_out:
        o_gla, o_mla = prev
        front_specs += [rows(GLA_WIDTH, cur), rows(MLA_WIDTH, cur), _layer((D_MODEL, D_MODEL), l - 1)]
        front_args += [o_gla, o_mla, w["w_out"]]
    front_names = ["norm_g", "w_in", "w_mix"]
    post_names = ["w_misc", "b_gate", "q_norm_g", "w_uq", "kv_norm_g", "w_ukv",
                  "qg_nope", "qg_rope", "kg_nope", "kg_rope", "k_bias"]
    front_specs += [_layer(w[name].shape[1:], l) for name in front_names]
    front_args += [w[name] for name in front_names]
    post_in_specs = [rows(LANES, lag), rows(LANES, lag)] + [_layer(w[name].shape[1:], l) for name in post_names]
    post_in_args = [cos_t, sin_t] + [w[name] for name in post_names]
    assert len(post_in_specs) == N_POST_IN

    out_shape = front_shapes + post_shapes
    out_specs = front_out_specs + post_specs
    if fuse_out:
        out_shape = (jax.ShapeDtypeStruct((T, D_MODEL), F32),) + out_shape
        out_specs = (rows(D_MODEL, cur),) + out_specs
    return pl.pallas_call(
        functools.partial(_proj_kernel, fuse_out=fuse_out),
        grid=(n + 1,), in_specs=front_specs + post_in_specs, out_specs=out_specs, out_shape=out_shape,
        scratch_shapes=[pltpu.VMEM((tm, D_IN_PAD), F32), pltpu.VMEM((tm, D_IN_PAD), F32)],
        compiler_params=_params(1), name="outproj_proj" if fuse_out else "proj",
    )(*front_args, *post_in_args)


def _gla_call(qd, ki, ke, gv, dec, gg, gla_g, l, S):
    T = qd.shape[0]
    tm = GLA_TM
    rows = lambda w: pl.BlockSpec((tm, w), lambda i: (i, 0))
    return pl.pallas_call(
        functools.partial(_gla_kernel, steps_per_seq=S // tm),
        grid=(T // tm,),
        in_specs=[rows(GLA_KEY_WIDTH), rows(GLA_KEY_WIDTH), rows(GLA_KEY_WIDTH), rows(GLA_WIDTH),
                  pl.BlockSpec((tm // GLA_CHUNK, GLA_KEY_WIDTH), lambda i: (i, 0)),
                  rows(GLA_WIDTH), _layer((1, GLA_DV), l)],
        out_specs=rows(GLA_WIDTH),
        out_shape=jax.ShapeDtypeStruct((T, GLA_WIDTH), BF16),
        scratch_shapes=[pltpu.VMEM((GLA_WIDTH, LANES), F32)],
        compiler_params=_params(1), name="gla",
    )(qd, ki, ke, gv, dec, gg, gla_g)


def _attn_call(q, k, v, mg):
    B, H, S, _ = q.shape
    tq = ATTN_TQ
    nq = S // tq
    q_spec = pl.BlockSpec((None, None, tq, QK_PAD), lambda b, h, i: (b, h, i, 0))
    kv_specs = [pl.BlockSpec((None, None, S, QK_PAD), lambda b, h, i: (b, h, 0, 0)),
                pl.BlockSpec((None, None, S, MLA_DV), lambda b, h, i: (b, h, 0, 0))]
    io_spec = pl.BlockSpec((tq, MLA_DV), lambda b, h, i: (b * nq + i, h))
    return pl.pallas_call(
        _attn_kernel, grid=(B, H, nq),
        in_specs=[q_spec] + kv_specs + [io_spec], out_specs=io_spec,
        out_shape=jax.ShapeDtypeStruct(mg.shape, BF16),
        scratch_shapes=[pltpu.VMEM((tq, 1), F32), pltpu.VMEM((tq, 1), F32), pltpu.VMEM((tq, MLA_DV), F32)],
        compiler_params=_params(3), name="attn_online",
    )(q, k, v, mg)


def _attn_bounded_call(q, k, v, mg):
    B, H, S, _ = q.shape
    hp = ATTN_HEADS_PER_STEP
    seq = lambda w: pl.BlockSpec((None, hp, S, w), lambda b, h: (b, h, 0, 0))
    io_spec = pl.BlockSpec((S, hp * MLA_DV), lambda b, h: (b, h))
    return pl.pallas_call(
        _attn_bounded_kernel, grid=(B, H // hp),
        in_specs=[seq(QK_PAD), seq(QK_PAD), seq(MLA_DV), io_spec], out_specs=io_spec,
        out_shape=jax.ShapeDtypeStruct(mg.shape, BF16),
        compiler_params=_params(2), name="attn_bounded",
    )(q, k, v, mg)


def _out_call(x2, og, om, w_out, l):
    T = x2.shape[0]
    tm = OUT_TM
    rows = lambda w: pl.BlockSpec((tm, w), lambda i: (i, 0))
    return pl.pallas_call(
        _out_kernel, grid=(T // tm,),
        in_specs=[rows(D_MODEL), rows(GLA_WIDTH), rows(MLA_WIDTH), _layer((D_MODEL, D_MODEL), l)],
        out_specs=rows(D_MODEL),
        out_shape=jax.ShapeDtypeStruct((T, D_MODEL), F32),
        compiler_params=_params(1), name="outproj",
    )(x2, og, om, w_out)


def _prep_weights(norm_g, w_in, w_gla_gate_up, b_gla_gate, gla_norm_g, mla_q_norm_g, w_uq,
                  mla_kv_norm_g, w_ukv, q_head_g, k_head_g, w_out):
    depth = w_in.shape[0]
    o = [0]
    for wdt in (GLA_KEY_WIDTH, GLA_KEY_WIDTH, GLA_WIDTH, GLA_GATE_RANK, GLA_WIDTH,
                MLA_Q_RANK, MLA_KV_RANK, MLA_ROPE, MLA_WIDTH):
        o.append(o[-1] + wdt)
    wt = jnp.swapaxes(w_in, 1, 2).astype(BF16)
    pad = jnp.zeros((depth, LANES - MLA_ROPE - GLA_GATE_RANK, D_MODEL), BF16)
    w_mix = jnp.concatenate([wt[:, o[6]:o[8]], wt[:, o[3]:o[4]], pad], axis=1)
    assert w_mix.shape[1] == PROJ_CHUNK and o[9] == D_IN
    assert [o[0], o[1], o[2], o[2] + PROJ_CHUNK, o[4], o[4] + PROJ_CHUNK, o[5], None, o[8], o[8] + PROJ_CHUNK] \
        == list(SRC_ROW_OF_CHUNK)

    w_misc = jnp.zeros((depth, LANES, GLA_KEY_WIDTH), F32)
    w_misc = w_misc.at[:, MLA_ROPE:MLA_ROPE + GLA_GATE_RANK, :].set(w_gla_gate_up).astype(BF16)

    wq = w_uq.reshape(depth, MLA_Q_RANK, MLA_HEADS, MLA_QK)
    w_uq_p = jnp.concatenate([wq[..., :MLA_NOPE].reshape(depth, MLA_Q_RANK, -1),
                              wq[..., MLA_NOPE:].reshape(depth, MLA_Q_RANK, -1)], axis=-1).astype(BF16)
    wkv = w_ukv.reshape(depth, MLA_KV_RANK, MLA_HEADS, MLA_NOPE + MLA_DV)
    w_ukv_p = jnp.concatenate([wkv[..., :MLA_NOPE].reshape(depth, MLA_KV_RANK, -1),
                               wkv[..., MLA_NOPE:].reshape(depth, MLA_KV_RANK, -1)], axis=-1).astype(BF16)

    zeros64 = jnp.zeros((depth, 1, LANES - MLA_ROPE), F32)
    row = lambda a: a[:, None, :]
    return dict(
        norm_g=row(norm_g), w_in=wt, w_mix=w_mix, w_misc=w_misc, b_gate=row(b_gla_gate),
        gla_g=row(gla_norm_g), q_norm_g=row(mla_q_norm_g), w_uq=w_uq_p,
        kv_norm_g=row(mla_kv_norm_g), w_ukv=w_ukv_p,
        qg_nope=row(q_head_g[:, :MLA_NOPE]),
        qg_rope=jnp.concatenate([row(q_head_g[:, MLA_NOPE:])] * 2, axis=-1),
        kg_nope=row(k_head_g[:, :MLA_NOPE]),
        kg_rope=jnp.concatenate([row(k_head_g[:, MLA_NOPE:]), zeros64], axis=-1),
        w_out=w_out.astype(BF16),
    )


def _score_bounds(q_head_g, k_head_g):
    bound = (MLA_QK ** 0.5 * LOG2E * SCORE_BOUND_SLACK
             * jnp.max(jnp.abs(q_head_g), axis=-1) * jnp.max(jnp.abs(k_head_g), axis=-1))
    use = bound <= SCORE_BOUND_MAX_LOG2
    depth = q_head_g.shape[0]
    bias = jnp.zeros((depth, 1, LANES), F32).at[:, 0, MLA_ROPE].set(jnp.where(use, -bound, 0.0))
    return use, bias


def _rope_tables(positions):
    inv_freq = ROPE_THETA ** (-jnp.arange(0, MLA_ROPE, 2, dtype=F32) / MLA_ROPE)
    reps = LANES // inv_freq.shape[0]
    inv_freq = jnp.tile(inv_freq, reps)
    sign = jnp.tile(jnp.repeat(jnp.array([-1.0, 1.0], F32), MLA_ROPE // 2), LANES // MLA_ROPE)
    ang = positions.astype(F32).reshape(-1, 1) * inv_freq
    return jnp.cos(ang), jnp.sin(ang) * sign


def kernel(x, positions, norm_g, w_in, w_gla_gate_up, b_gla_gate, gla_norm_g, mla_q_norm_g, w_uq,
           mla_kv_norm_g, w_ukv, q_head_g, k_head_g, w_out):
    B, S, D = x.shape
    assert D == D_MODEL and S % max(PROJ_TM, GLA_TM, ATTN_TQ, OUT_TM) == 0 and ATTN_TQ % ATTN_TK == 0
    depth = w_in.shape[0]
    weights = _prep_weights(norm_g, w_in, w_gla_gate_up, b_gla_gate, gla_norm_g, mla_q_norm_g, w_uq,
                            mla_kv_norm_g, w_ukv, q_head_g, k_head_g, w_out)
    use_bounded, weights["k_bias"] = _score_bounds(q_head_g, k_head_g)
    cos_t, sin_t = _rope_tables(positions)
    x2 = x.reshape(B * S, D)
    prev = None
    for l in range(depth):
        outs = _proj_call(x2, prev, cos_t, sin_t, weights, l, B, S)
        if prev is not None:
            x2, outs = outs[0], outs[1:]
        gv, gg, mg, qd, ki, ke, dec, q, k, v = outs
        o_gla = _gla_call(qd, ki, ke, gv, dec, gg, weights["gla_g"], l, S)
        o_mla = lax.cond(use_bounded[l], _attn_bounded_call, _attn_call, q, k, v, mg)
        prev = (o_gla, o_mla)
    x2 = _out_call(x2, *prev, weights["w_out"], depth - 1)
    return x2.reshape(B, S, D)
```

```python
import functools

import jax
import jax.numpy as jnp
from jax import lax
from jax.experimental import pallas as pl
from jax.experimental.pallas import tpu as pltpu

F32 = jnp.float32
BF16 = jnp.bfloat16

D_MODEL = 1024
GLA_HEADS = 4
GLA_DK = 64
GLA_DV = 128
GLA_KEY_WIDTH = GLA_HEADS * GLA_DK
GLA_WIDTH = GLA_HEADS * GLA_DV
GLA_GATE_RANK = 16
GLA_GATE_TEMP = 16.0
GLA_CHUNK = 64
MLA_HEADS = 4
MLA_NOPE = 128
MLA_ROPE = 64
MLA_QK = MLA_NOPE + MLA_ROPE
MLA_DV = 128
MLA_WIDTH = MLA_HEADS * MLA_DV
MLA_Q_RANK = 256
MLA_KV_RANK = 128
ROPE_THETA = 10000.0
EPS = 1e-6
LOG2E = 1.4426950408889634
SCORE_BOUND_MAX_LOG2 = 40.0 * LOG2E
SCORE_BOUND_SLACK = 1.02

LANES = 128
SUBLANES = 8
QK_PAD = 2 * LANES

OFF_GQ = 0
OFF_GK = 256
OFF_GV = 512
OFF_GGATE = 1024
OFF_CQ = 1536
OFF_CKV = 1792
OFF_MISC = 1920
OFF_MGATE = 2048
D_IN_PAD = 2560
D_IN = 2512
SRC_ROW_OF_CHUNK = (0, 256, 512, 768, 1040, 1296, 1552, None, 2000, 2256)

PROJ_TM = 512
GLA_TM = 2048
ATTN_TQ = 512
ATTN_TK = 512
ATTN_HEADS_PER_STEP = 2
OUT_TM = 1024
CUMSUM_BLOCK = 256
PROJ_CHUNK = 256
VMEM_LIMIT = 56 * 1024 * 1024


def _rms(t, g):
    return t * lax.rsqrt(jnp.mean(t * t, axis=-1, keepdims=True) + EPS) * g


def _silu(t):
    return t * jax.nn.sigmoid(t)


def _split_dot(mat_bf16, t):
    hi = t.astype(BF16)
    lo = (t - hi.astype(F32)).astype(BF16)
    return (jnp.dot(mat_bf16, hi, preferred_element_type=F32)
            + jnp.dot(mat_bf16, lo, preferred_element_type=F32))


def _swap_halves(t):
    lane = lax.broadcasted_iota(jnp.int32, t.shape, 1)
    return jnp.where(lane % MLA_ROPE < MLA_ROPE // 2,
                     pltpu.roll(t, LANES - MLA_ROPE // 2, axis=1),
                     pltpu.roll(t, MLA_ROPE // 2, axis=1))


N_POST_IN = 13
N_POST_OUT = 7
POST_STAGE_AT = (0, 5)


def _proj_kernel(*refs, fuse_out):
    refs = list(refs)
    x_ref = refs.pop(0)
    if fuse_out:
        og_ref, om_ref, wout_ref = refs[:3]
        del refs[:3]
    ng_ref, win_ref, wmix_ref = refs[:3]
    post_in = refs[3:3 + N_POST_IN]
    del refs[:3 + N_POST_IN]
    if fuse_out:
        xo_ref = refs.pop(0)
    gv_ref, gg_ref, mg_ref = refs[:3]
    post_out = refs[3:3 + N_POST_OUT]
    z0_ref, z1_ref = refs[3 + N_POST_OUT:]
    i = pl.program_id(0)
    nt = (((1,), (1,)), ((), ()))
    direct = {}
    for ref, off, fn in ((gv_ref, OFF_GV, None), (gg_ref, OFF_GGATE, _silu), (mg_ref, OFF_MGATE, _silu)):
        for k in range(ref.shape[1] // PROJ_CHUNK):
            direct[off // PROJ_CHUNK + k] = (ref, k, fn)

    @pl.when(i == 0)
    def _():
        z1_ref[...] = jnp.zeros_like(z1_ref)

    def step(z_read, z_write):
        post = _proj_post(z_read, *post_in, *post_out)
        x = x_ref[...]
        if fuse_out:
            w = wout_ref[...]
            x = (x + jnp.dot(og_ref[...], w[:GLA_WIDTH], preferred_element_type=F32)
                 + jnp.dot(om_ref[...], w[GLA_WIDTH:], preferred_element_type=F32))
            xo_ref[...] = x
        h = _rms(x, ng_ref[...]).astype(BF16)
        order = sorted(range(D_IN_PAD // PROJ_CHUNK), key=lambda c: ((c not in direct) == fuse_out, c))
        for pos, c in enumerate(order):
            if pos in POST_STAGE_AT:
                next(post)
            cols = slice(c * PROJ_CHUNK, (c + 1) * PROJ_CHUNK)
            src = SRC_ROW_OF_CHUNK[c]
            w_c = wmix_ref[...] if src is None else win_ref[src:src + PROJ_CHUNK, :]
            zc = lax.dot_general(h, w_c, nt, preferred_element_type=F32)
            if c in direct:
                ref, k, fn = direct[c]
                ref[:, k * PROJ_CHUNK:(k + 1) * PROJ_CHUNK] = (zc if fn is None else fn(zc)).astype(BF16)
            else:
                z_write[:, cols] = zc
        for _ in post:
            pass

    @pl.when(i % 2 == 0)
    def _():
        step(z1_ref, z0_ref)

    @pl.when(i % 2 == 1)
    def _():
        step(z0_ref, z1_ref)


def _proj_post(z, cos_ref, sin_ref, wmisc_ref, bgate_ref,
               qng_ref, wuq_ref, kvng_ref, wukv_ref, qgn_ref, qgr_ref, kgn_ref, kgr_ref, kbias_ref,
               qd_ref, ki_ref, ke_ref, dec_ref, q_ref, k_ref, v_ref):
    tm = z.shape[0]

    misc = z[:, OFF_MISC:OFF_MISC + LANES]
    logit = jnp.dot(misc.astype(BF16), wmisc_ref[...], preferred_element_type=F32) + bgate_ref[...]
    cqn = _rms(z[:, OFF_CQ:OFF_CQ + MLA_Q_RANK], qng_ref[...]).astype(BF16)
    qraw = jnp.dot(cqn, wuq_ref[...], preferred_element_type=F32)
    ckvn = _rms(z[:, OFF_CKV:OFF_CKV + MLA_KV_RANK], kvng_ref[...]).astype(BF16)
    kvraw = jnp.dot(ckvn, wukv_ref[...], preferred_element_type=F32)
    yield

    log_a = (jnp.minimum(logit, 0.0) - jnp.log(1.0 + jnp.exp(-jnp.abs(logit)))) * (1.0 / GLA_GATE_TEMP)

    cb = CUMSUM_BLOCK
    r = lax.broadcasted_iota(jnp.int32, (cb, cb), 0)
    c = lax.broadcasted_iota(jnp.int32, (cb, cb), 1)
    same = (r // GLA_CHUNK) == (c // GLA_CHUNK)
    lower = jnp.where(same & (c <= r), 1.0, 0.0).astype(BF16)
    upper = jnp.where(same & (c > r), 1.0, 0.0).astype(BF16)
    nck = cb // GLA_CHUNK
    sr = lax.broadcasted_iota(jnp.int32, (SUBLANES, cb), 0)
    sc = lax.broadcasted_iota(jnp.int32, (SUBLANES, cb), 1)
    chunk_sum = jnp.where((sc // GLA_CHUNK) == sr, 1.0, 0.0).astype(BF16)

    gq = z[:, OFF_GQ:OFF_GQ + GLA_KEY_WIDTH] * (GLA_DK ** -0.5)
    gk = z[:, OFF_GK:OFF_GK + GLA_KEY_WIDTH]
    for blk in range(tm // cb):
        rows = slice(blk * cb, (blk + 1) * cb)
        la = log_a[rows]
        b = _split_dot(lower, la)
        rest = _split_dot(upper, la)
        tot = _split_dot(chunk_sum, la)
        qd_ref[rows, :] = (gq[rows] * jnp.exp(b)).astype(BF16)
        ki_ref[rows, :] = (gk[rows] * jnp.exp(-b)).astype(BF16)
        ke_ref[rows, :] = (gk[rows] * jnp.exp(rest)).astype(BF16)
        dec_ref[blk * nck:(blk + 1) * nck, :] = jnp.exp(tot[:nck])
    yield

    cos = cos_ref[...]
    sin = sin_ref[...]
    lane = lax.broadcasted_iota(jnp.int32, (tm, LANES), 1)
    low64 = lane < MLA_ROPE

    scale = MLA_QK ** -0.5 * LOG2E
    one_col = jnp.where(lane == MLA_ROPE, 1.0, 0.0)
    nope_w = MLA_HEADS * MLA_NOPE
    for pair in range(MLA_HEADS // 2):
        t = qraw[:, nope_w + pair * LANES:nope_w + (pair + 1) * LANES]
        t2 = t * t
        ssq_lo = jnp.sum(jnp.where(low64, t2, 0.0), axis=-1, keepdims=True)
        ssq_hi = jnp.sum(jnp.where(low64, 0.0, t2), axis=-1, keepdims=True)
        tg = t * qgr_ref[...]
        rot = tg * cos + _swap_halves(tg) * sin
        for sub in range(2):
            hd = 2 * pair + sub
            qn = qraw[:, hd * MLA_NOPE:(hd + 1) * MLA_NOPE]
            ssq = jnp.sum(qn * qn, axis=-1, keepdims=True) + (ssq_lo if sub == 0 else ssq_hi)
            rinv = lax.rsqrt(ssq * (1.0 / MLA_QK) + EPS) * scale
            q_ref[hd, :, 0:LANES] = (qn * rinv * qgn_ref[...]).astype(BF16)
            rsel = rot if sub == 0 else pltpu.roll(rot, MLA_ROPE, axis=1)
            q_ref[hd, :, LANES:QK_PAD] = jnp.where(low64, rsel * rinv, one_col).astype(BF16)

    kpe2 = misc * misc
    ssq_pe = jnp.sum(jnp.where(low64, kpe2, 0.0), axis=-1, keepdims=True)
    kg = misc * kgr_ref[...]
    krot = kg * cos + _swap_halves(kg) * sin
    for hd in range(MLA_HEADS):
        kn = kvraw[:, hd * MLA_NOPE:(hd + 1) * MLA_NOPE]
        ssq = jnp.sum(kn * kn, axis=-1, keepdims=True) + ssq_pe
        rinv = lax.rsqrt(ssq * (1.0 / MLA_QK) + EPS)
        k_ref[hd, :, 0:LANES] = (kn * rinv * kgn_ref[...]).astype(BF16)
        k_ref[hd, :, LANES:QK_PAD] = (krot * rinv + kbias_ref[...]).astype(BF16)
        v_ref[hd, :, :] = kvraw[:, nope_w + hd * MLA_DV:nope_w + (hd + 1) * MLA_DV].astype(BF16)


def _gla_kernel(qd_ref, ki_ref, ke_ref, v_ref, dec_ref, gate_ref, g_ref, o_ref, state_ref, *, steps_per_seq):
    @pl.when(pl.program_id(0) % steps_per_seq == 0)
    def _():
        state_ref[...] = jnp.zeros_like(state_ref)

    for _ in _gla_chunks(qd_ref, ki_ref, ke_ref, v_ref, dec_ref, gate_ref, g_ref, o_ref, state_ref):
        pass


def _gla_chunks(qd_ref, ki_ref, ke_ref, v_ref, dec_ref, gate_ref, g_ref, o_ref, state_ref):
    tm = qd_ref.shape[0]

    C = GLA_CHUNK
    klane_head = lax.broadcasted_iota(jnp.int32, (C, GLA_KEY_WIDTH), 1) // GLA_DK
    vlane_head = lax.broadcasted_iota(jnp.int32, (C, GLA_WIDTH), 1) // GLA_DV
    row = lax.broadcasted_iota(jnp.int32, (C, GLA_KEY_WIDTH), 0)
    col_in_head = lax.broadcasted_iota(jnp.int32, (C, GLA_KEY_WIDTH), 1) % GLA_DK
    causal = row >= col_in_head
    pair_rows = 2 * GLA_DV
    own = ((lax.broadcasted_iota(jnp.int32, (pair_rows, LANES), 0) // GLA_DV)
           == (lax.broadcasted_iota(jnp.int32, (pair_rows, LANES), 1) // GLA_DK))
    nt = (((1,), (1,)), ((), ()))
    tn = (((0,), (0,)), ((), ()))
    g = g_ref[...]
    n_pairs = GLA_HEADS // 2

    for ck in range(tm // C):
        rows = slice(ck * C, (ck + 1) * C)
        qd = qd_ref[rows, :]
        ki = ki_ref[rows, :]
        ke = ke_ref[rows, :]
        v = v_ref[rows, :]
        upd = [lax.dot_general(v[:, p * pair_rows:(p + 1) * pair_rows], ke[:, p * LANES:(p + 1) * LANES],
                               tn, preferred_element_type=F32) for p in range(n_pairs)]
        zk = jnp.zeros_like(ki)
        zv = jnp.zeros_like(v)
        kstack = jnp.concatenate([jnp.where(klane_head == hd, ki, zk) for hd in range(GLA_HEADS)], axis=0)
        vstack = jnp.concatenate([jnp.where(vlane_head == hd, v, zv) for hd in range(GLA_HEADS)], axis=0)
        a = lax.dot_general(qd, kstack, nt, preferred_element_type=F32)
        a = jnp.where(causal, a, 0.0).astype(BF16)
        o = jnp.dot(a, vstack, preferred_element_type=F32)
        state = state_ref[...]
        o = o + jnp.concatenate(
            [lax.dot_general(qd[:, p * LANES:(p + 1) * LANES],
                             state[p * pair_rows:(p + 1) * pair_rows].astype(BF16),
                             nt, preferred_element_type=F32) for p in range(n_pairs)], axis=1)
        for p in range(n_pairs):
            srows = slice(p * pair_rows, (p + 1) * pair_rows)
            state_ref[srows, :] = (state[srows] * dec_ref[ck:ck + 1, p * LANES:(p + 1) * LANES]
                                   + jnp.where(own, upd[p], 0.0))
        for hd in range(GLA_HEADS):
            cols = slice(hd * GLA_DV, (hd + 1) * GLA_DV)
            oh = o[:, cols]
            on = oh * lax.rsqrt(jnp.mean(oh * oh, axis=-1, keepdims=True) + EPS) * g
            o_ref[rows, cols] = (on * gate_ref[rows, cols].astype(F32)).astype(BF16)
        yield


def _attn_kernel(q_ref, k_ref, v_ref, gate_ref, o_ref, m_ref, l_ref, acc_ref):
    tq = q_ref.shape[0]
    tk = ATTN_TK
    i = pl.program_id(2)
    q = q_ref[...]
    nt = (((1,), (1,)), ((), ()))
    m_ref[...] = jnp.full_like(m_ref, -jnp.inf)
    l_ref[...] = jnp.zeros_like(l_ref)
    acc_ref[...] = jnp.zeros_like(acc_ref)

    def step(j, masked):
        start = pl.multiple_of(j * tk, tk)
        k = k_ref[pl.ds(start, tk), :]
        v = v_ref[pl.ds(start, tk), :]
        s = lax.dot_general(q, k, nt, preferred_element_type=F32)
        if masked:
            rr = lax.broadcasted_iota(jnp.int32, s.shape, 0)
            cc = lax.broadcasted_iota(jnp.int32, s.shape, 1)
            s = jnp.where(cc <= rr, s, -jnp.inf)
        m_prev = m_ref[...]
        m_new = jnp.maximum(m_prev, jnp.max(s, axis=-1, keepdims=True))
        alpha = jnp.exp2(m_prev - m_new)
        p = jnp.exp2(s - m_new)
        l_ref[...] = alpha * l_ref[...] + jnp.sum(p, axis=-1, keepdims=True)
        acc_ref[...] = alpha * acc_ref[...] + jnp.dot(p.astype(BF16), v, preferred_element_type=F32)
        m_ref[...] = m_new

    def body(j, carry):
        step(j, False)
        return carry

    lax.fori_loop(0, i * (tq // tk), body, 0)
    step(i * (tq // tk), True)
    o = acc_ref[...] / l_ref[...]
    o_ref[...] = (o * gate_ref[...].astype(F32)).astype(BF16)


def _attn_bounded_kernel(q_ref, k_ref, v_ref, gate_ref, o_ref):
    t = ATTN_TQ
    half = t // 2
    n_tiles = k_ref.shape[1] // t
    nt = (((1,), (1,)), ((), ()))
    ones_col = jnp.where(lax.broadcasted_iota(jnp.int32, (t, LANES), 1) == 0, 1.0, 0.0).astype(BF16)
    causal_a = (lax.broadcasted_iota(jnp.int32, (half, half), 1)
                <= lax.broadcasted_iota(jnp.int32, (half, half), 0))
    causal_b = (lax.broadcasted_iota(jnp.int32, (half, t), 1)
                <= lax.broadcasted_iota(jnp.int32, (half, t), 0) + half)

    def probs(q, k):
        return jnp.exp2(lax.dot_general(q, k, nt, preferred_element_type=F32))

    def key_tile(hd, q, j):
        k = k_ref[hd, j * t:(j + 1) * t, :]
        v_aug = jnp.concatenate([v_ref[hd, j * t:(j + 1) * t, :], ones_col], axis=1)
        return jnp.dot(probs(q, k).astype(BF16), v_aug, preferred_element_type=F32)

    def diagonal_tile(hd, q, j):
        k = k_ref[hd, j * t:(j + 1) * t, :]
        v_aug = jnp.concatenate([v_ref[hd, j * t:(j + 1) * t, :], ones_col], axis=1)
        p_a = jnp.where(causal_a, probs(q[:half], k[:half]), 0.0).astype(BF16)
        p_b = jnp.where(causal_b, probs(q[half:], k), 0.0).astype(BF16)
        return jnp.concatenate([jnp.dot(p_a, v_aug[:half], preferred_element_type=F32),
                                jnp.dot(p_b, v_aug, preferred_element_type=F32)], axis=0)

    for hd in range(q_ref.shape[0]):
        cols = slice(hd * MLA_DV, (hd + 1) * MLA_DV)
        for idx in range(n_tiles):
            rows = slice(idx * t, (idx + 1) * t)
            q = q_ref[hd, rows, :]
            acc = diagonal_tile(hd, q, idx)
            for j in range(idx):
                acc = acc + key_tile(hd, q, j)
            o = acc[:, :MLA_DV] * (1.0 / acc[:, MLA_DV:MLA_DV + 1])
            o_ref[rows, cols] = (o * gate_ref[rows, cols].astype(F32)).astype(BF16)


def _out_kernel(x_ref, og_ref, om_ref, w_ref, o_ref):
    w = w_ref[...]
    o_ref[...] = (x_ref[...]
                  + jnp.dot(og_ref[...], w[:GLA_WIDTH], preferred_element_type=F32)
                  + jnp.dot(om_ref[...], w[GLA_WIDTH:], preferred_element_type=F32))


def _layer(shape, l):
    return pl.BlockSpec((None,) + tuple(shape), lambda *idx: (l,) + (0,) * len(shape),
                        pipeline_mode=pl.Buffered(1))


def _params(n_axes):
    return pltpu.CompilerParams(dimension_semantics=("arbitrary",) * n_axes,
                                vmem_limit_bytes=VMEM_LIMIT)


def _proj_call(x2, prev, cos_t, sin_t, w, l, B, S):
    T = x2.shape[0]
    tm = PROJ_TM
    ns = S // tm
    n = T // tm
    fuse_out = prev is not None
    cur = lambda i: jnp.minimum(i, n - 1)
    lag = lambda i: jnp.maximum(i - 1, 0)
    rows = lambda w, at: pl.BlockSpec((tm, w), lambda i: (at(i), 0))
    heads = lambda w: pl.BlockSpec((None, MLA_HEADS, tm, w), lambda i: (lag(i) // ns, 0, lag(i) % ns, 0))
    front_shapes = (
        jax.ShapeDtypeStruct((T, GLA_WIDTH), BF16),
        jax.ShapeDtypeStruct((T, GLA_WIDTH), BF16),
        jax.ShapeDtypeStruct((T, MLA_WIDTH), BF16),
    )
    front_out_specs = (rows(GLA_WIDTH, cur), rows(GLA_WIDTH, cur), rows(MLA_WIDTH, cur))
    post_shapes = (
        jax.ShapeDtypeStruct((T, GLA_KEY_WIDTH), BF16),
        jax.ShapeDtypeStruct((T, GLA_KEY_WIDTH), BF16),
        jax.ShapeDtypeStruct((T, GLA_KEY_WIDTH), BF16),
        jax.ShapeDtypeStruct((T // GLA_CHUNK, GLA_KEY_WIDTH), F32),
        jax.ShapeDtypeStruct((B, MLA_HEADS, S, QK_PAD), BF16),
        jax.ShapeDtypeStruct((B, MLA_HEADS, S, QK_PAD), BF16),
        jax.ShapeDtypeStruct((B, MLA_HEADS, S, MLA_DV), BF16),
    )
    post_specs = (rows(GLA_KEY_WIDTH, lag), rows(GLA_KEY_WIDTH, lag), rows(GLA_KEY_WIDTH, lag),
                  pl.BlockSpec((tm // GLA_CHUNK, GLA_KEY_WIDTH), lambda i: (lag(i), 0)),
                  heads(QK_PAD), heads(QK_PAD), heads(MLA_DV))
    assert len(post_specs) == N_POST_OUT

    front_specs = [rows(D_MODEL, cur)]
    front_args = [x2]
    if fuse_out:
        o_gla, o_mla = prev
        front_specs += [rows(GLA_WIDTH, cur), rows(MLA_WIDTH, cur), _layer((D_MODEL, D_MODEL), l - 1)]
        front_args += [o_gla, o_mla, w["w_out"]]
    front_names = ["norm_g", "w_in", "w_mix"]
    post_names = ["w_misc", "b_gate", "q_norm_g", "w_uq", "kv_norm_g", "w_ukv",
                  "qg_nope", "qg_rope", "kg_nope", "kg_rope", "k_bias"]
    front_specs += [_layer(w[name].shape[1:], l) for name in front_names]
    front_args += [w[name] for name in front_names]
    post_in_specs = [rows(LANES, lag), rows(LANES, lag)] + [_layer(w[name].shape[1:], l) for name in post_names]
    post_in_args = [cos_t, sin_t] + [w[name] for name in post_names]
    assert len(post_in_specs) == N_POST_IN

    out_shape = front_shapes + post_shapes
    out_specs = front_out_specs + post_specs
    if fuse_out:
        out_shape = (jax.ShapeDtypeStruct((T, D_MODEL), F32),) + out_shape
        out_specs = (rows(D_MODEL, cur),) + out_specs
    return pl.pallas_call(
        functools.partial(_proj_kernel, fuse_out=fuse_out),
        grid=(n + 1,), in_specs=front_specs + post_in_specs, out_specs=out_specs, out_shape=out_shape,
        scratch_shapes=[pltpu.VMEM((tm, D_IN_PAD), F32), pltpu.VMEM((tm, D_IN_PAD), F32)],
        compiler_params=_params(1), name="outproj_proj" if fuse_out else "proj",
    )(*front_args, *post_in_args)


def _gla_call(qd, ki, ke, gv, dec, gg, gla_g, l, S):
    T = qd.shape[0]
    tm = GLA_TM
    rows = lambda w: pl.BlockSpec((tm, w), lambda i: (i, 0))
    return pl.pallas_call(
        functools.partial(_gla_kernel, steps_per_seq=S // tm),
        grid=(T // tm,),
        in_specs=[rows(GLA_KEY_WIDTH), rows(GLA_KEY_WIDTH), rows(GLA_KEY_WIDTH), rows(GLA_WIDTH),
                  pl.BlockSpec((tm // GLA_CHUNK, GLA_KEY_WIDTH), lambda i: (i, 0)),
                  rows(GLA_WIDTH), _layer((1, GLA_DV), l)],
        out_specs=rows(GLA_WIDTH),
        out_shape=jax.ShapeDtypeStruct((T, GLA_WIDTH), BF16),
        scratch_shapes=[pltpu.VMEM((GLA_WIDTH, LANES), F32)],
        compiler_params=_params(1), name="gla",
    )(qd, ki, ke, gv, dec, gg, gla_g)


def _attn_call(q, k, v, mg):
    B, H, S, _ = q.shape
    tq = ATTN_TQ
    nq = S // tq
    q_spec = pl.BlockSpec((None, None, tq, QK_PAD), lambda b, h, i: (b, h, i, 0))
    kv_specs = [pl.BlockSpec((None, None, S, QK_PAD), lambda b, h, i: (b, h, 0, 0)),
                pl.BlockSpec((None, None, S, MLA_DV), lambda b, h, i: (b, h, 0, 0))]
    io_spec = pl.BlockSpec((tq, MLA_DV), lambda b, h, i: (b * nq + i, h))
    return pl.pallas_call(
        _attn_kernel, grid=(B, H, nq),
        in_specs=[q_spec] + kv_specs + [io_spec], out_specs=io_spec,
        out_shape=jax.ShapeDtypeStruct(mg.shape, BF16),
        scratch_shapes=[pltpu.VMEM((tq, 1), F32), pltpu.VMEM((tq, 1), F32), pltpu.VMEM((tq, MLA_DV), F32)],
        compiler_params=_params(3), name="attn_online",
    )(q, k, v, mg)


def _attn_bounded_call(q, k, v, mg):
    B, H, S, _ = q.shape
    hp = ATTN_HEADS_PER_STEP
    seq = lambda w: pl.BlockSpec((None, hp, S, w), lambda b, h: (b, h, 0, 0))
    io_spec = pl.BlockSpec((S, hp * MLA_DV), lambda b, h: (b, h))
    return pl.pallas_call(
        _attn_bounded_kernel, grid=(B, H // hp),
        in_specs=[seq(QK_PAD), seq(QK_PAD), seq(MLA_DV), io_spec], out_specs=io_spec,
        out_shape=jax.ShapeDtypeStruct(mg.shape, BF16),
        compiler_params=_params(2), name="attn_bounded",
    )(q, k, v, mg)


def _out_call(x2, og, om, w_out, l):
    T = x2.shape[0]
    tm = OUT_TM
    rows = lambda w: pl.BlockSpec((tm, w), lambda i: (i, 0))
    return pl.pallas_call(
        _out_kernel, grid=(T // tm,),
        in_specs=[rows(D_MODEL), rows(GLA_WIDTH), rows(MLA_WIDTH), _layer((D_MODEL, D_MODEL), l)],
        out_specs=rows(D_MODEL),
        out_shape=jax.ShapeDtypeStruct((T, D_MODEL), F32),
        compiler_params=_params(1), name="outproj",
    )(x2, og, om, w_out)


def _prep_weights(norm_g, w_in, w_gla_gate_up, b_gla_gate, gla_norm_g, mla_q_norm_g, w_uq,
                  mla_kv_norm_g, w_ukv, q_head_g, k_head_g, w_out):
    depth = w_in.shape[0]
    o = [0]
    for wdt in (GLA_KEY_WIDTH, GLA_KEY_WIDTH, GLA_WIDTH, GLA_GATE_RANK, GLA_WIDTH,
                MLA_Q_RANK, MLA_KV_RANK, MLA_ROPE, MLA_WIDTH):
        o.append(o[-1] + wdt)
    wt = jnp.swapaxes(w_in, 1, 2).astype(BF16)
    pad = jnp.zeros((depth, LANES - MLA_ROPE - GLA_GATE_RANK, D_MODEL), BF16)
    w_mix = jnp.concatenate([wt[:, o[6]:o[8]], wt[:, o[3]:o[4]], pad], axis=1)
    assert w_mix.shape[1] == PROJ_CHUNK and o[9] == D_IN
    assert [o[0], o[1], o[2], o[2] + PROJ_CHUNK, o[4], o[4] + PROJ_CHUNK, o[5], None, o[8], o[8] + PROJ_CHUNK] \
        == list(SRC_ROW_OF_CHUNK)

    w_misc = jnp.zeros((depth, LANES, GLA_KEY_WIDTH), F32)
    w_misc = w_misc.at[:, MLA_ROPE:MLA_ROPE + GLA_GATE_RANK, :].set(w_gla_gate_up).astype(BF16)

    wq = w_uq.reshape(depth, MLA_Q_RANK, MLA_HEADS, MLA_QK)
    w_uq_p = jnp.concatenate([wq[..., :MLA_NOPE].reshape(depth, MLA_Q_RANK, -1),
                              wq[..., MLA_NOPE:].reshape(depth, MLA_Q_RANK, -1)], axis=-1).astype(BF16)
    wkv = w_ukv.reshape(depth, MLA_KV_RANK, MLA_HEADS, MLA_NOPE + MLA_DV)
    w_ukv_p = jnp.concatenate([wkv[..., :MLA_NOPE].reshape(depth, MLA_KV_RANK, -1),
                               wkv[..., MLA_NOPE:].reshape(depth, MLA_KV_RANK, -1)], axis=-1).astype(BF16)

    zeros64 = jnp.zeros((depth, 1, LANES - MLA_ROPE), F32)
    row = lambda a: a[:, None, :]
    return dict(
        norm_g=row(norm_g), w_in=wt, w_mix=w_mix, w_misc=w_misc, b_gate=row(b_gla_gate),
        gla_g=row(gla_norm_g), q_norm_g=row(mla_q_norm_g), w_uq=w_uq_p,
        kv_norm_g=row(mla_kv_norm_g), w_ukv=w_ukv_p,
        qg_nope=row(q_head_g[:, :MLA_NOPE]),
        qg_rope=jnp.concatenate([row(q_head_g[:, MLA_NOPE:])] * 2, axis=-1),
        kg_nope=row(k_head_g[:, :MLA_NOPE]),
        kg_rope=jnp.concatenate([row(k_head_g[:, MLA_NOPE:]), zeros64], axis=-1),
        w_out=w_out.astype(BF16),
    )


def _score_bounds(q_head_g, k_head_g):
    bound = (MLA_QK ** 0.5 * LOG2E * SCORE_BOUND_SLACK
             * jnp.max(jnp.abs(q_head_g), axis=-1) * jnp.max(jnp.abs(k_head_g), axis=-1))
    use = bound <= SCORE_BOUND_MAX_LOG2
    depth = q_head_g.shape[0]
    bias = jnp.zeros((depth, 1, LANES), F32).at[:, 0, MLA_ROPE].set(jnp.where(use, -bound, 0.0))
    return use, bias


def _rope_tables(positions):
    inv_freq = ROPE_THETA ** (-jnp.arange(0, MLA_ROPE, 2, dtype=F32) / MLA_ROPE)
    reps = LANES // inv_freq.shape[0]
    inv_freq = jnp.tile(inv_freq, reps)
    sign = jnp.tile(jnp.repeat(jnp.array([-1.0, 1.0], F32), MLA_ROPE // 2), LANES // MLA_ROPE)
    ang = positions.astype(F32).reshape(-1, 1) * inv_freq
    return jnp.cos(ang), jnp.sin(ang) * sign


def kernel(x, positions, norm_g, w_in, w_gla_gate_up, b_gla_gate, gla_norm_g, mla_q_norm_g, w_uq,
           mla_kv_norm_g, w_ukv, q_head_g, k_head_g, w_out):
    B, S, D = x.shape
    assert D == D_MODEL and S % max(PROJ_TM, GLA_TM, ATTN_TQ, OUT_TM) == 0 and ATTN_TQ % ATTN_TK == 0
    assert MLA_HEADS % ATTN_HEADS_PER_STEP == 0
    depth = w_in.shape[0]
    weights = _prep_weights(norm_g, w_in, w_gla_gate_up, b_gla_gate, gla_norm_g, mla_q_norm_g, w_uq,
                            mla_kv_norm_g, w_ukv, q_head_g, k_head_g, w_out)
    use_bounded, weights["k_bias"] = _score_bounds(q_head_g, k_head_g)
    cos_t, sin_t = _rope_tables(positions)
    x2 = x.reshape(B * S, D)
    prev = None
    for l in range(depth):
        outs = _proj_call(x2, prev, cos_t, sin_t, weights, l, B, S)
        if prev is not None:
            x2, outs = outs[0], outs[1:]
        gv, gg, mg, qd, ki, ke, dec, q, k, v = outs
        o_gla = _gla_call(qd, ki, ke, gv, dec, gg, weights["gla_g"], l, S)
        o_mla = lax.cond(use_bounded[l], _attn_bounded_call, _attn_call, q, k, v, mg)
        prev = (o_gla, o_mla)
    x2 = _out_call(x2, *prev, weights["w_out"], depth - 1)
    return x2.reshape(B, S, D)
```

```python
import functools

import jax
import jax.numpy as jnp
from jax import lax
from jax.experimental import pallas as pl
from jax.experimental.pallas import tpu as pltpu

F32 = jnp.float32
BF16 = jnp.bfloat16

D_MODEL = 1024
GLA_HEADS = 4
GLA_DK = 64
GLA_DV = 128
GLA_KEY_WIDTH = GLA_HEADS * GLA_DK
GLA_WIDTH = GLA_HEADS * GLA_DV
GLA_GATE_RANK = 16
GLA_GATE_TEMP = 16.0
GLA_CHUNK = 64
MLA_HEADS = 4
MLA_NOPE = 128
MLA_ROPE = 64
MLA_QK = MLA_NOPE + MLA_ROPE
MLA_DV = 128
MLA_WIDTH = MLA_HEADS * MLA_DV
MLA_Q_RANK = 256
MLA_KV_RANK = 128
ROPE_THETA = 10000.0
EPS = 1e-6
LOG2E = 1.4426950408889634
SCORE_BOUND_MAX_LOG2 = 40.0 * LOG2E
SCORE_BOUND_SLACK = 1.02

LANES = 128
SUBLANES = 8
QK_PAD = 2 * LANES

OFF_GQ = 0
OFF_GK = 256
OFF_GV = 512
OFF_GGATE = 1024
OFF_CQ = 1536
OFF_CKV = 1792
OFF_MISC = 1920
OFF_MGATE = 2048
D_IN_PAD = 2560
D_IN = 2512
SRC_ROW_OF_CHUNK = (0, 256, 512, 768, 1040, 1296, 1552, None, 2000, 2256)

PROJ_TM = 512
GLA_TM = 2048
ATTN_TQ = 512
ATTN_TK = 512
ATTN_HEADS_PER_STEP = 2
OUT_TM = 1024
ROPE_TM = 2048
CUMSUM_BLOCK = 256
PROJ_CHUNK = 256
VMEM_LIMIT = 56 * 1024 * 1024


def _rms(t, g):
    return t * lax.rsqrt(jnp.mean(t * t, axis=-1, keepdims=True) + EPS) * g


def _silu(t):
    return t * jax.nn.sigmoid(t)


def _split_dot(mat_bf16, t):
    hi = t.astype(BF16)
    lo = (t - hi.astype(F32)).astype(BF16)
    return (jnp.dot(mat_bf16, hi, preferred_element_type=F32)
            + jnp.dot(mat_bf16, lo, preferred_element_type=F32))


def _swap_halves(t):
    lane = lax.broadcasted_iota(jnp.int32, t.shape, 1)
    return jnp.where(lane % MLA_ROPE < MLA_ROPE // 2,
                     pltpu.roll(t, LANES - MLA_ROPE // 2, axis=1),
                     pltpu.roll(t, MLA_ROPE // 2, axis=1))


N_POST_IN = 13
N_POST_OUT = 7
POST_STAGE_AT = (0, 5)


def _proj_kernel(*refs, fuse_out):
    refs = list(refs)
    x_ref = refs.pop(0)
    if fuse_out:
        og_ref, om_ref, wout_ref = refs[:3]
        del refs[:3]
    ng_ref, win_ref, wmix_ref = refs[:3]
    post_in = refs[3:3 + N_POST_IN]
    del refs[:3 + N_POST_IN]
    if fuse_out:
        xo_ref = refs.pop(0)
    gv_ref, gg_ref, mg_ref = refs[:3]
    post_out = refs[3:3 + N_POST_OUT]
    z0_ref, z1_ref = refs[3 + N_POST_OUT:]
    i = pl.program_id(0)
    nt = (((1,), (1,)), ((), ()))
    direct = {}
    for ref, off, fn in ((gv_ref, OFF_GV, None), (gg_ref, OFF_GGATE, _silu), (mg_ref, OFF_MGATE, _silu)):
        for k in range(ref.shape[1] // PROJ_CHUNK):
            direct[off // PROJ_CHUNK + k] = (ref, k, fn)

    @pl.when(i == 0)
    def _():
        z1_ref[...] = jnp.zeros_like(z1_ref)

    def step(z_read, z_write):
        post = _proj_post(z_read, *post_in, *post_out)
        x = x_ref[...]
        if fuse_out:
            w = wout_ref[...]
            x = (x + jnp.dot(og_ref[...], w[:GLA_WIDTH], preferred_element_type=F32)
                 + jnp.dot(om_ref[...], w[GLA_WIDTH:], preferred_element_type=F32))
            xo_ref[...] = x
        h = _rms(x, ng_ref[...]).astype(BF16)
        order = sorted(range(D_IN_PAD // PROJ_CHUNK), key=lambda c: ((c not in direct) == fuse_out, c))
        for pos, c in enumerate(order):
            if pos in POST_STAGE_AT:
                next(post)
            cols = slice(c * PROJ_CHUNK, (c + 1) * PROJ_CHUNK)
            src = SRC_ROW_OF_CHUNK[c]
            w_c = wmix_ref[...] if src is None else win_ref[src:src + PROJ_CHUNK, :]
            zc = lax.dot_general(h, w_c, nt, preferred_element_type=F32)
            if c in direct:
                ref, k, fn = direct[c]
                ref[:, k * PROJ_CHUNK:(k + 1) * PROJ_CHUNK] = (zc if fn is None else fn(zc)).astype(BF16)
            else:
                z_write[:, cols] = zc
        for _ in post:
            pass

    @pl.when(i % 2 == 0)
    def _():
        step(z1_ref, z0_ref)

    @pl.when(i % 2 == 1)
    def _():
        step(z0_ref, z1_ref)


def _proj_post(z, cos_ref, sin_ref, wmisc_ref, bgate_ref,
               qng_ref, wuq_ref, kvng_ref, wukv_ref, qgn_ref, qgr_ref, kgn_ref, kgr_ref, kbias_ref,
               qd_ref, ki_ref, ke_ref, dec_ref, q_ref, k_ref, v_ref):
    tm = z.shape[0]

    misc = z[:, OFF_MISC:OFF_MISC + LANES]
    logit = jnp.dot(misc.astype(BF16), wmisc_ref[...], preferred_element_type=F32) + bgate_ref[...]
    cqn = _rms(z[:, OFF_CQ:OFF_CQ + MLA_Q_RANK], qng_ref[...]).astype(BF16)
    qraw = jnp.dot(cqn, wuq_ref[...], preferred_element_type=F32)
    ckvn = _rms(z[:, OFF_CKV:OFF_CKV + MLA_KV_RANK], kvng_ref[...]).astype(BF16)
    kvraw = jnp.dot(ckvn, wukv_ref[...], preferred_element_type=F32)
    yield

    log_a = (jnp.minimum(logit, 0.0) - jnp.log(1.0 + jnp.exp(-jnp.abs(logit)))) * (1.0 / GLA_GATE_TEMP)

    cb = CUMSUM_BLOCK
    r = lax.broadcasted_iota(jnp.int32, (cb, cb), 0)
    c = lax.broadcasted_iota(jnp.int32, (cb, cb), 1)
    same = (r // GLA_CHUNK) == (c // GLA_CHUNK)
    lower = jnp.where(same & (c <= r), 1.0, 0.0).astype(BF16)
    upper = jnp.where(same & (c > r), 1.0, 0.0).astype(BF16)
    nck = cb // GLA_CHUNK
    sr = lax.broadcasted_iota(jnp.int32, (SUBLANES, cb), 0)
    sc = lax.broadcasted_iota(jnp.int32, (SUBLANES, cb), 1)
    chunk_sum = jnp.where((sc // GLA_CHUNK) == sr, 1.0, 0.0).astype(BF16)

    gq = z[:, OFF_GQ:OFF_GQ + GLA_KEY_WIDTH] * (GLA_DK ** -0.5)
    gk = z[:, OFF_GK:OFF_GK + GLA_KEY_WIDTH]
    for blk in range(tm // cb):
        rows = slice(blk * cb, (blk + 1) * cb)
        la = log_a[rows]
        b = _split_dot(lower, la)
        rest = _split_dot(upper, la)
        tot = _split_dot(chunk_sum, la)
        qd_ref[rows, :] = (gq[rows] * jnp.exp(b)).astype(BF16)
        ki_ref[rows, :] = (gk[rows] * jnp.exp(-b)).astype(BF16)
        ke_ref[rows, :] = (gk[rows] * jnp.exp(rest)).astype(BF16)
        dec_ref[blk * nck:(blk + 1) * nck, :] = jnp.exp(tot[:nck])
    yield

    cos = cos_ref[...]
    sin = sin_ref[...]
    lane = lax.broadcasted_iota(jnp.int32, (tm, LANES), 1)
    low64 = lane < MLA_ROPE

    scale = MLA_QK ** -0.5 * LOG2E
    one_col = jnp.where(lane == MLA_ROPE, 1.0, 0.0)
    nope_w = MLA_HEADS * MLA_NOPE
    for pair in range(MLA_HEADS // 2):
        t = qraw[:, nope_w + pair * LANES:nope_w + (pair + 1) * LANES]
        t2 = t * t
        ssq_lo = jnp.sum(jnp.where(low64, t2, 0.0), axis=-1, keepdims=True)
        ssq_hi = jnp.sum(jnp.where(low64, 0.0, t2), axis=-1, keepdims=True)
        tg = t * qgr_ref[...]
        rot = tg * cos + _swap_halves(tg) * sin
        for sub in range(2):
            hd = 2 * pair + sub
            qn = qraw[:, hd * MLA_NOPE:(hd + 1) * MLA_NOPE]
            ssq = jnp.sum(qn * qn, axis=-1, keepdims=True) + (ssq_lo if sub == 0 else ssq_hi)
            rinv = lax.rsqrt(ssq * (1.0 / MLA_QK) + EPS) * scale
            q_ref[hd, :, 0:LANES] = (qn * rinv * qgn_ref[...]).astype(BF16)
            rsel = rot if sub == 0 else pltpu.roll(rot, MLA_ROPE, axis=1)
            q_ref[hd, :, LANES:QK_PAD] = jnp.where(low64, rsel * rinv, one_col).astype(BF16)

    kpe2 = misc * misc
    ssq_pe = jnp.sum(jnp.where(low64, kpe2, 0.0), axis=-1, keepdims=True)
    kg = misc * kgr_ref[...]
    krot = kg * cos + _swap_halves(kg) * sin
    for hd in range(MLA_HEADS):
        kn = kvraw[:, hd * MLA_NOPE:(hd + 1) * MLA_NOPE]
        ssq = jnp.sum(kn * kn, axis=-1, keepdims=True) + ssq_pe
        rinv = lax.rsqrt(ssq * (1.0 / MLA_QK) + EPS)
        k_ref[hd, :, 0:LANES] = (kn * rinv * kgn_ref[...]).astype(BF16)
        k_ref[hd, :, LANES:QK_PAD] = (krot * rinv + kbias_ref[...]).astype(BF16)
        v_ref[hd, :, :] = kvraw[:, nope_w + hd * MLA_DV:nope_w + (hd + 1) * MLA_DV].astype(BF16)


def _gla_kernel(qd_ref, ki_ref, ke_ref, v_ref, dec_ref, gate_ref, g_ref, o_ref, state_ref, *, steps_per_seq):
    @pl.when(pl.program_id(0) % steps_per_seq == 0)
    def _():
        state_ref[...] = jnp.zeros_like(state_ref)

    for _ in _gla_chunks(qd_ref, ki_ref, ke_ref, v_ref, dec_ref, gate_ref, g_ref, o_ref, state_ref):
        pass


def _gla_chunks(qd_ref, ki_ref, ke_ref, v_ref, dec_ref, gate_ref, g_ref, o_ref, state_ref):
    tm = qd_ref.shape[0]

    C = GLA_CHUNK
    klane_head = lax.broadcasted_iota(jnp.int32, (C, GLA_KEY_WIDTH), 1) // GLA_DK
    vlane_head = lax.broadcasted_iota(jnp.int32, (C, GLA_WIDTH), 1) // GLA_DV
    row = lax.broadcasted_iota(jnp.int32, (C, GLA_KEY_WIDTH), 0)
    col_in_head = lax.broadcasted_iota(jnp.int32, (C, GLA_KEY_WIDTH), 1) % GLA_DK
    causal = row >= col_in_head
    pair_rows = 2 * GLA_DV
    own = ((lax.broadcasted_iota(jnp.int32, (pair_rows, LANES), 0) // GLA_DV)
           == (lax.broadcasted_iota(jnp.int32, (pair_rows, LANES), 1) // GLA_DK))
    nt = (((1,), (1,)), ((), ()))
    tn = (((0,), (0,)), ((), ()))
    g = g_ref[...]
    n_pairs = GLA_HEADS // 2

    for ck in range(tm // C):
        rows = slice(ck * C, (ck + 1) * C)
        qd = qd_ref[rows, :]
        ki = ki_ref[rows, :]
        ke = ke_ref[rows, :]
        v = v_ref[rows, :]
        upd = [lax.dot_general(v[:, p * pair_rows:(p + 1) * pair_rows], ke[:, p * LANES:(p + 1) * LANES],
                               tn, preferred_element_type=F32) for p in range(n_pairs)]
        zk = jnp.zeros_like(ki)
        zv = jnp.zeros_like(v)
        kstack = jnp.concatenate([jnp.where(klane_head == hd, ki, zk) for hd in range(GLA_HEADS)], axis=0)
        vstack = jnp.concatenate([jnp.where(vlane_head == hd, v, zv) for hd in range(GLA_HEADS)], axis=0)
        a = lax.dot_general(qd, kstack, nt, preferred_element_type=F32)
        a = jnp.where(causal, a, 0.0).astype(BF16)
        o = jnp.dot(a, vstack, preferred_element_type=F32)
        state = state_ref[...]
        o = o + jnp.concatenate(
            [lax.dot_general(qd[:, p * LANES:(p + 1) * LANES],
                             state[p * pair_rows:(p + 1) * pair_rows].astype(BF16),
                             nt, preferred_element_type=F32) for p in range(n_pairs)], axis=1)
        for p in range(n_pairs):
            srows = slice(p * pair_rows, (p + 1) * pair_rows)
            state_ref[srows, :] = (state[srows] * dec_ref[ck:ck + 1, p * LANES:(p + 1) * LANES]
                                   + jnp.where(own, upd[p], 0.0))
        for hd in range(GLA_HEADS):
            cols = slice(hd * GLA_DV, (hd + 1) * GLA_DV)
            oh = o[:, cols]
            on = oh * lax.rsqrt(jnp.mean(oh * oh, axis=-1, keepdims=True) + EPS) * g
            o_ref[rows, cols] = (on * gate_ref[rows, cols].astype(F32)).astype(BF16)
        yield


def _attn_kernel(q_ref, k_ref, v_ref, gate_ref, o_ref, m_ref, l_ref, acc_ref):
    tq = q_ref.shape[0]
    tk = ATTN_TK
    i = pl.program_id(2)
    q = q_ref[...]
    nt = (((1,), (1,)), ((), ()))
    m_ref[...] = jnp.full_like(m_ref, -jnp.inf)
    l_ref[...] = jnp.zeros_like(l_ref)
    acc_ref[...] = jnp.zeros_like(acc_ref)

    def step(j, masked):
        start = pl.multiple_of(j * tk, tk)
        k = k_ref[pl.ds(start, tk), :]
        v = v_ref[pl.ds(start, tk), :]
        s = lax.dot_general(q, k, nt, preferred_element_type=F32)
        if masked:
            rr = lax.broadcasted_iota(jnp.int32, s.shape, 0)
            cc = lax.broadcasted_iota(jnp.int32, s.shape, 1)
            s = jnp.where(cc <= rr, s, -jnp.inf)
        m_prev = m_ref[...]
        m_new = jnp.maximum(m_prev, jnp.max(s, axis=-1, keepdims=True))
        alpha = jnp.exp2(m_prev - m_new)
        p = jnp.exp2(s - m_new)
        l_ref[...] = alpha * l_ref[...] + jnp.sum(p, axis=-1, keepdims=True)
        acc_ref[...] = alpha * acc_ref[...] + jnp.dot(p.astype(BF16), v, preferred_element_type=F32)
        m_ref[...] = m_new

    def body(j, carry):
        step(j, False)
        return carry

    lax.fori_loop(0, i * (tq // tk), body, 0)
    step(i * (tq // tk), True)
    o = acc_ref[...] / l_ref[...]
    o_ref[...] = (o * gate_ref[...].astype(F32)).astype(BF16)


def _attn_bounded_kernel(q_ref, k_ref, v_ref, gate_ref, o_ref):
    t = ATTN_TQ
    half = t // 2
    n_tiles = k_ref.shape[1] // t
    nt = (((1,), (1,)), ((), ()))
    ones_col = jnp.where(lax.broadcasted_iota(jnp.int32, (t, LANES), 1) == 0, 1.0, 0.0).astype(BF16)
    causal_a = (lax.broadcasted_iota(jnp.int32, (half, half), 1)
                <= lax.broadcasted_iota(jnp.int32, (half, half), 0))
    causal_b = (lax.broadcasted_iota(jnp.int32, (half, t), 1)
                <= lax.broadcasted_iota(jnp.int32, (half, t), 0) + half)

    def probs(q, k):
        return jnp.exp2(lax.dot_general(q, k, nt, preferred_element_type=F32))

    def key_tile(hd, q, j):
        k = k_ref[hd, j * t:(j + 1) * t, :]
        v_aug = jnp.concatenate([v_ref[hd, j * t:(j + 1) * t, :], ones_col], axis=1)
        return jnp.dot(probs(q, k).astype(BF16), v_aug, preferred_element_type=F32)

    def diagonal_tile(hd, q, j):
        k = k_ref[hd, j * t:(j + 1) * t, :]
        v_aug = jnp.concatenate([v_ref[hd, j * t:(j + 1) * t, :], ones_col], axis=1)
        p_a = jnp.where(causal_a, probs(q[:half], k[:half]), 0.0).astype(BF16)
        p_b = jnp.where(causal_b, probs(q[half:], k), 0.0).astype(BF16)
        return jnp.concatenate([jnp.dot(p_a, v_aug[:half], preferred_element_type=F32),
                                jnp.dot(p_b, v_aug, preferred_element_type=F32)], axis=0)

    for hd in range(q_ref.shape[0]):
        cols = slice(hd * MLA_DV, (hd + 1) * MLA_DV)
        for idx in range(n_tiles):
            rows = slice(idx * t, (idx + 1) * t)
            q = q_ref[hd, rows, :]
            acc = diagonal_tile(hd, q, idx)
            for j in range(idx):
                acc = acc + key_tile(hd, q, j)
            o = acc[:, :MLA_DV] * (1.0 / acc[:, MLA_DV:MLA_DV + 1])
            o_ref[rows, cols] = (o * gate_ref[rows, cols].astype(F32)).astype(BF16)


def _out_kernel(x_ref, og_ref, om_ref, w_ref, o_ref):
    w = w_ref[...]
    o_ref[...] = (x_ref[...]
                  + jnp.dot(og_ref[...], w[:GLA_WIDTH], preferred_element_type=F32)
                  + jnp.dot(om_ref[...], w[GLA_WIDTH:], preferred_element_type=F32))


def _layer(shape, l):
    return pl.BlockSpec((None,) + tuple(shape), lambda *idx: (l,) + (0,) * len(shape),
                        pipeline_mode=pl.Buffered(1))


def _params(n_axes):
    return pltpu.CompilerParams(dimension_semantics=("arbitrary",) * n_axes,
                                vmem_limit_bytes=VMEM_LIMIT)


def _proj_call(x2, prev, cos_t, sin_t, w, l, B, S):
    T = x2.shape[0]
    tm = PROJ_TM
    ns = S // tm
    n = T // tm
    fuse_out = prev is not None
    cur = lambda i: jnp.minimum(i, n - 1)
    lag = lambda i: jnp.maximum(i - 1, 0)
    rows = lambda w, at: pl.BlockSpec((tm, w), lambda i: (at(i), 0))
    heads = lambda w: pl.BlockSpec((None, MLA_HEADS, tm, w), lambda i: (lag(i) // ns, 0, lag(i) % ns, 0))
    front_shapes = (
        jax.ShapeDtypeStruct((T, GLA_WIDTH), BF16),
        jax.ShapeDtypeStruct((T, GLA_WIDTH), BF16),
        jax.ShapeDtypeStruct((T, MLA_WIDTH), BF16),
    )
    front_out_specs = (rows(GLA_WIDTH, cur), rows(GLA_WIDTH, cur), rows(MLA_WIDTH, cur))
    post_shapes = (
        jax.ShapeDtypeStruct((T, GLA_KEY_WIDTH), BF16),
        jax.ShapeDtypeStruct((T, GLA_KEY_WIDTH), BF16),
        jax.ShapeDtypeStruct((T, GLA_KEY_WIDTH), BF16),
        jax.ShapeDtypeStruct((T // GLA_CHUNK, GLA_KEY_WIDTH), F32),
        jax.ShapeDtypeStruct((B, MLA_HEADS, S, QK_PAD), BF16),
        jax.ShapeDtypeStruct((B, MLA_HEADS, S, QK_PAD), BF16),
        jax.ShapeDtypeStruct((B, MLA_HEADS, S, MLA_DV), BF16),
    )
    post_specs = (rows(GLA_KEY_WIDTH, lag), rows(GLA_KEY_WIDTH, lag), rows(GLA_KEY_WIDTH, lag),
                  pl.BlockSpec((tm // GLA_CHUNK, GLA_KEY_WIDTH), lambda i: (lag(i), 0)),
                  heads(QK_PAD), heads(QK_PAD), heads(MLA_DV))
    assert len(post_specs) == N_POST_OUT

    front_specs = [rows(D_MODEL, cur)]
    front_args = [x2]
    if fuse_out:
        o_gla, o_mla = prev
        front_specs += [rows(GLA_WIDTH, cur), rows(MLA_WIDTH, cur), _layer((D_MODEL, D_MODEL), l - 1)]
        front_args += [o_gla, o_mla, w["w_out"]]
    front_names = ["norm_g", "w_in", "w_mix"]
    post_names = ["w_misc", "b_gate", "q_norm_g", "w_uq", "kv_norm_g", "w_ukv",
                  "qg_nope", "qg_rope", "kg_nope", "kg_rope", "k_bias"]
    front_specs += [_layer(w[name].shape[1:], l) for name in front_names]
    front_args += [w[name] for name in front_names]
    post_in_specs = [rows(LANES, lag), rows(LANES, lag)] + [_layer(w[name].shape[1:], l) for name in post_names]
    post_in_args = [cos_t, sin_t] + [w[name] for name in post_names]
    assert len(post_in_specs) == N_POST_IN

    out_shape = front_shapes + post_shapes
    out_specs = front_out_specs + post_specs
    if fuse_out:
        out_shape = (jax.ShapeDtypeStruct((T, D_MODEL), F32),) + out_shape
        out_specs = (rows(D_MODEL, cur),) + out_specs
    return pl.pallas_call(
        functools.partial(_proj_kernel, fuse_out=fuse_out),
        grid=(n + 1,), in_specs=front_specs + post_in_specs, out_specs=out_specs, out_shape=out_shape,
        scratch_shapes=[pltpu.VMEM((tm, D_IN_PAD), F32), pltpu.VMEM((tm, D_IN_PAD), F32)],
        compiler_params=_params(1), name="outproj_proj" if fuse_out else "proj",
    )(*front_args, *post_in_args)


def _gla_call(qd, ki, ke, gv, dec, gg, gla_g, l, S):
    T = qd.shape[0]
    tm = GLA_TM
    rows = lambda w: pl.BlockSpec((tm, w), lambda i: (i, 0))
    return pl.pallas_call(
        functools.partial(_gla_kernel, steps_per_seq=S // tm),
        grid=(T // tm,),
        in_specs=[rows(GLA_KEY_WIDTH), rows(GLA_KEY_WIDTH), rows(GLA_KEY_WIDTH), rows(GLA_WIDTH),
                  pl.BlockSpec((tm // GLA_CHUNK, GLA_KEY_WIDTH), lambda i: (i, 0)),
                  rows(GLA_WIDTH), _layer((1, GLA_DV), l)],
        out_specs=rows(GLA_WIDTH),
        out_shape=jax.ShapeDtypeStruct((T, GLA_WIDTH), BF16),
        scratch_shapes=[pltpu.VMEM((GLA_WIDTH, LANES), F32)],
        compiler_params=_params(1), name="gla",
    )(qd, ki, ke, gv, dec, gg, gla_g)


def _attn_call(q, k, v, mg):
    B, H, S, _ = q.shape
    tq = ATTN_TQ
    nq = S // tq
    q_spec = pl.BlockSpec((None, None, tq, QK_PAD), lambda b, h, i: (b, h, i, 0))
    kv_specs = [pl.BlockSpec((None, None, S, QK_PAD), lambda b, h, i: (b, h, 0, 0)),
                pl.BlockSpec((None, None, S, MLA_DV), lambda b, h, i: (b, h, 0, 0))]
    io_spec = pl.BlockSpec((tq, MLA_DV), lambda b, h, i: (b * nq + i, h))
    return pl.pallas_call(
        _attn_kernel, grid=(B, H, nq),
        in_specs=[q_spec] + kv_specs + [io_spec], out_specs=io_spec,
        out_shape=jax.ShapeDtypeStruct(mg.shape, BF16),
        scratch_shapes=[pltpu.VMEM((tq, 1), F32), pltpu.VMEM((tq, 1), F32), pltpu.VMEM((tq, MLA_DV), F32)],
        compiler_params=_params(3), name="attn_online",
    )(q, k, v, mg)


def _attn_bounded_call(q, k, v, mg):
    B, H, S, _ = q.shape
    hp = ATTN_HEADS_PER_STEP
    seq = lambda w: pl.BlockSpec((None, hp, S, w), lambda b, h: (b, h, 0, 0))
    io_spec = pl.BlockSpec((S, hp * MLA_DV), lambda b, h: (b, h))
    return pl.pallas_call(
        _attn_bounded_kernel, grid=(B, H // hp),
        in_specs=[seq(QK_PAD), seq(QK_PAD), seq(MLA_DV), io_spec], out_specs=io_spec,
        out_shape=jax.ShapeDtypeStruct(mg.shape, BF16),
        compiler_params=_params(2), name="attn_bounded",
    )(q, k, v, mg)


def _out_call(x2, og, om, w_out, l):
    T = x2.shape[0]
    tm = OUT_TM
    rows = lambda w: pl.BlockSpec((tm, w), lambda i: (i, 0))
    return pl.pallas_call(
        _out_kernel, grid=(T // tm,),
        in_specs=[rows(D_MODEL), rows(GLA_WIDTH), rows(MLA_WIDTH), _layer((D_MODEL, D_MODEL), l)],
        out_specs=rows(D_MODEL),
        out_shape=jax.ShapeDtypeStruct((T, D_MODEL), F32),
        compiler_params=_params(1), name="outproj",
    )(x2, og, om, w_out)


def _prep_weights(norm_g, w_in, w_gla_gate_up, b_gla_gate, gla_norm_g, mla_q_norm_g, w_uq,
                  mla_kv_norm_g, w_ukv, q_head_g, k_head_g, w_out):
    depth = w_in.shape[0]
    o = [0]
    for wdt in (GLA_KEY_WIDTH, GLA_KEY_WIDTH, GLA_WIDTH, GLA_GATE_RANK, GLA_WIDTH,
                MLA_Q_RANK, MLA_KV_RANK, MLA_ROPE, MLA_WIDTH):
        o.append(o[-1] + wdt)
    wt = jnp.swapaxes(w_in, 1, 2).astype(BF16)
    pad = jnp.zeros((depth, LANES - MLA_ROPE - GLA_GATE_RANK, D_MODEL), BF16)
    w_mix = jnp.concatenate([wt[:, o[6]:o[8]], wt[:, o[3]:o[4]], pad], axis=1)
    assert w_mix.shape[1] == PROJ_CHUNK and o[9] == D_IN
    assert [o[0], o[1], o[2], o[2] + PROJ_CHUNK, o[4], o[4] + PROJ_CHUNK, o[5], None, o[8], o[8] + PROJ_CHUNK] \
        == list(SRC_ROW_OF_CHUNK)

    w_misc = jnp.zeros((depth, LANES, GLA_KEY_WIDTH), F32)
    w_misc = w_misc.at[:, MLA_ROPE:MLA_ROPE + GLA_GATE_RANK, :].set(w_gla_gate_up).astype(BF16)

    wq = w_uq.reshape(depth, MLA_Q_RANK, MLA_HEADS, MLA_QK)
    w_uq_p = jnp.concatenate([wq[..., :MLA_NOPE].reshape(depth, MLA_Q_RANK, -1),
                              wq[..., MLA_NOPE:].reshape(depth, MLA_Q_RANK, -1)], axis=-1).astype(BF16)
    wkv = w_ukv.reshape(depth, MLA_KV_RANK, MLA_HEADS, MLA_NOPE + MLA_DV)
    w_ukv_p = jnp.concatenate([wkv[..., :MLA_NOPE].reshape(depth, MLA_KV_RANK, -1),
                               wkv[..., MLA_NOPE:].reshape(depth, MLA_KV_RANK, -1)], axis=-1).astype(BF16)

    zeros64 = jnp.zeros((depth, 1, LANES - MLA_ROPE), F32)
    row = lambda a: a[:, None, :]
    return dict(
        norm_g=row(norm_g), w_in=wt, w_mix=w_mix, w_misc=w_misc, b_gate=row(b_gla_gate),
        gla_g=row(gla_norm_g), q_norm_g=row(mla_q_norm_g), w_uq=w_uq_p,
        kv_norm_g=row(mla_kv_norm_g), w_ukv=w_ukv_p,
        qg_nope=row(q_head_g[:, :MLA_NOPE]),
        qg_rope=jnp.concatenate([row(q_head_g[:, MLA_NOPE:])] * 2, axis=-1),
        kg_nope=row(k_head_g[:, :MLA_NOPE]),
        kg_rope=jnp.concatenate([row(k_head_g[:, MLA_NOPE:]), zeros64], axis=-1),
        w_out=w_out.astype(BF16),
    )


def _score_bounds(q_head_g, k_head_g):
    bound = (MLA_QK ** 0.5 * LOG2E * SCORE_BOUND_SLACK
             * jnp.max(jnp.abs(q_head_g), axis=-1) * jnp.max(jnp.abs(k_head_g), axis=-1))
    use = bound <= SCORE_BOUND_MAX_LOG2
    depth = q_head_g.shape[0]
    bias = jnp.zeros((depth, 1, LANES), F32).at[:, 0, MLA_ROPE].set(jnp.where(use, -bound, 0.0))
    return use, bias


def _rope_tables(positions):
    inv_freq = ROPE_THETA ** (-jnp.arange(0, MLA_ROPE, 2, dtype=F32) / MLA_ROPE)
    reps = LANES // inv_freq.shape[0]
    inv_freq = jnp.tile(inv_freq, reps).reshape(1, LANES)
    sign = jnp.tile(jnp.repeat(jnp.array([-1.0, 1.0], F32), MLA_ROPE // 2), LANES // MLA_ROPE).reshape(1, LANES)
    pos = positions.astype(F32).reshape(-1, 1)
    T = pos.shape[0]
    tm = ROPE_TM
    assert T % tm == 0
    row = pl.BlockSpec((1, LANES), lambda i: (0, 0))
    tab = pl.BlockSpec((tm, LANES), lambda i: (i, 0))
    return pl.pallas_call(
        _rope_kernel, grid=(T // tm,),
        in_specs=[pl.BlockSpec((tm, 1), lambda i: (i, 0)), row, row], out_specs=(tab, tab),
        out_shape=(jax.ShapeDtypeStruct((T, LANES), F32),) * 2,
        compiler_params=_params(1), name="rope_tables",
    )(pos, inv_freq, sign)


def _rope_kernel(pos_ref, freq_ref, sign_ref, cos_ref, sin_ref):
    ang = pos_ref[...] * freq_ref[...]
    cos_ref[...] = jnp.cos(ang)
    sin_ref[...] = jnp.sin(ang) * sign_ref[...]


def kernel(x, positions, norm_g, w_in, w_gla_gate_up, b_gla_gate, gla_norm_g, mla_q_norm_g, w_uq,
           mla_kv_norm_g, w_ukv, q_head_g, k_head_g, w_out):
    B, S, D = x.shape
    assert D == D_MODEL and S % max(PROJ_TM, GLA_TM, ATTN_TQ, OUT_TM) == 0 and ATTN_TQ % ATTN_TK == 0
    assert MLA_HEADS % ATTN_HEADS_PER_STEP == 0
    depth = w_in.shape[0]
    weights = _prep_weights(norm_g, w_in, w_gla_gate_up, b_gla_gate, gla_norm_g, mla_q_norm_g, w_uq,
                            mla_kv_norm_g, w_ukv, q_head_g, k_head_g, w_out)
    use_bounded, weights["k_bias"] = _score_bounds(q_head_g, k_head_g)
    cos_t, sin_t = _rope_tables(positions)
    x2 = x.reshape(B * S, D)
    prev = None
    for l in range(depth):
        outs = _proj_call(x2, prev, cos_t, sin_t, weights, l, B, S)
        if prev is not None:
            x2, outs = outs[0], outs[1:]
        gv, gg, mg, qd, ki, ke, dec, q, k, v = outs
        o_gla = _gla_call(qd, ki, ke, gv, dec, gg, weights["gla_g"], l, S)
        o_mla = lax.cond(use_bounded[l], _attn_bounded_call, _attn_call, q, k, v, mg)
        prev = (o_gla, o_mla)
    x2 = _out_call(x2, *prev, weights["w_out"], depth - 1)
    return x2.reshape(B, S, D)
```

```python
import functools

import jax
import jax.numpy as jnp
from jax import lax
from jax.experimental import pallas as pl
from jax.experimental.pallas import tpu as pltpu

F32 = jnp.float32
BF16 = jnp.bfloat16

D_MODEL = 1024
GLA_HEADS = 4
GLA_DK = 64
GLA_DV = 128
GLA_KEY_WIDTH = GLA_HEADS * GLA_DK
GLA_WIDTH = GLA_HEADS * GLA_DV
GLA_GATE_RANK = 16
GLA_GATE_TEMP = 16.0
GLA_CHUNK = 64
MLA_HEADS = 4
MLA_NOPE = 128
MLA_ROPE = 64
MLA_QK = MLA_NOPE + MLA_ROPE
MLA_DV = 128
MLA_WIDTH = MLA_HEADS * MLA_DV
MLA_Q_RANK = 256
MLA_KV_RANK = 128
ROPE_THETA = 10000.0
EPS = 1e-6
LOG2E = 1.4426950408889634
SCORE_BOUND_MAX_LOG2 = 40.0 * LOG2E
SCORE_BOUND_SLACK = 1.02

LANES = 128
SUBLANES = 8
QK_PAD = 2 * LANES

OFF_GQ = 0
OFF_GK = 256
OFF_GV = 512
OFF_GGATE = 1024
OFF_CQ = 1536
OFF_CKV = 1792
OFF_MISC = 1920
OFF_MGATE = 2048
D_IN_PAD = 2560
D_IN = 2512
SRC_ROW_OF_CHUNK = (0, 256, 512, 768, 1040, 1296, 1552, None, 2000, 2256)

PROJ_TM = 512
GLA_TM = 2048
ATTN_TQ = 512
ATTN_TK = 512
ATTN_HEADS_PER_STEP = 2
OUT_TM = 1024
CUMSUM_BLOCK = 256
PROJ_CHUNK = 256
VMEM_LIMIT = 56 * 1024 * 1024


def _rms(t, g):
    return t * lax.rsqrt(jnp.mean(t * t, axis=-1, keepdims=True) + EPS) * g


def _silu(t):
    half = 0.5 * t
    return half * jnp.tanh(half) + half


def _split_dot(mat_bf16, t):
    hi = t.astype(BF16)
    lo = (t - hi.astype(F32)).astype(BF16)
    return (jnp.dot(mat_bf16, hi, preferred_element_type=F32)
            + jnp.dot(mat_bf16, lo, preferred_element_type=F32))


def _swap_halves(t):
    lane = lax.broadcasted_iota(jnp.int32, t.shape, 1)
    return jnp.where(lane % MLA_ROPE < MLA_ROPE // 2,
                     pltpu.roll(t, LANES - MLA_ROPE // 2, axis=1),
                     pltpu.roll(t, MLA_ROPE // 2, axis=1))


N_POST_IN = 13
N_POST_OUT = 7
POST_STAGE_AT = (0, 5)


def _proj_kernel(*refs, fuse_out):
    refs = list(refs)
    x_ref = refs.pop(0)
    if fuse_out:
        og_ref, om_ref, wout_ref = refs[:3]
        del refs[:3]
    ng_ref, win_ref, wmix_ref = refs[:3]
    post_in = refs[3:3 + N_POST_IN]
    del refs[:3 + N_POST_IN]
    if fuse_out:
        xo_ref = refs.pop(0)
    gv_ref, gg_ref, mg_ref = refs[:3]
    post_out = refs[3:3 + N_POST_OUT]
    z0_ref, z1_ref = refs[3 + N_POST_OUT:]
    i = pl.program_id(0)
    nt = (((1,), (1,)), ((), ()))
    direct = {}
    for ref, off, fn in ((gv_ref, OFF_GV, None), (gg_ref, OFF_GGATE, _silu), (mg_ref, OFF_MGATE, _silu)):
        for k in range(ref.shape[1] // PROJ_CHUNK):
            direct[off // PROJ_CHUNK + k] = (ref, k, fn)

    @pl.when(i == 0)
    def _():
        z1_ref[...] = jnp.zeros_like(z1_ref)

    def step(z_read, z_write):
        post = _proj_post(z_read, *post_in, *post_out)
        x = x_ref[...]
        if fuse_out:
            w = wout_ref[...]
            x = (x + jnp.dot(og_ref[...], w[:GLA_WIDTH], preferred_element_type=F32)
                 + jnp.dot(om_ref[...], w[GLA_WIDTH:], preferred_element_type=F32))
            xo_ref[...] = x
        h = _rms(x, ng_ref[...]).astype(BF16)
        order = sorted(range(D_IN_PAD // PROJ_CHUNK), key=lambda c: ((c not in direct) == fuse_out, c))
        for pos, c in enumerate(order):
            if pos in POST_STAGE_AT:
                next(post)
            cols = slice(c * PROJ_CHUNK, (c + 1) * PROJ_CHUNK)
            src = SRC_ROW_OF_CHUNK[c]
            w_c = wmix_ref[...] if src is None else win_ref[src:src + PROJ_CHUNK, :]
            zc = lax.dot_general(h, w_c, nt, preferred_element_type=F32)
            if c in direct:
                ref, k, fn = direct[c]
                ref[:, k * PROJ_CHUNK:(k + 1) * PROJ_CHUNK] = (zc if fn is None else fn(zc)).astype(BF16)
            else:
                z_write[:, cols] = zc
        for _ in post:
            pass

    @pl.when(i % 2 == 0)
    def _():
        step(z1_ref, z0_ref)

    @pl.when(i % 2 == 1)
    def _():
        step(z0_ref, z1_ref)


def _proj_post(z, cos_ref, sin_ref, wmisc_ref, bgate_ref,
               qng_ref, wuq_ref, kvng_ref, wukv_ref, qgn_ref, qgr_ref, kgn_ref, kgr_ref, kbias_ref,
               qd_ref, ki_ref, ke_ref, dec_ref, q_ref, k_ref, v_ref):
    tm = z.shape[0]

    misc = z[:, OFF_MISC:OFF_MISC + LANES]
    logit = jnp.dot(misc.astype(BF16), wmisc_ref[...], preferred_element_type=F32) + bgate_ref[...]
    cqn = _rms(z[:, OFF_CQ:OFF_CQ + MLA_Q_RANK], qng_ref[...]).astype(BF16)
    qraw = jnp.dot(cqn, wuq_ref[...], preferred_element_type=F32)
    ckvn = _rms(z[:, OFF_CKV:OFF_CKV + MLA_KV_RANK], kvng_ref[...]).astype(BF16)
    kvraw = jnp.dot(ckvn, wukv_ref[...], preferred_element_type=F32)
    yield

    log_a = (jnp.minimum(logit, 0.0) - jnp.log(1.0 + jnp.exp(-jnp.abs(logit)))) * (1.0 / GLA_GATE_TEMP)

    cb = CUMSUM_BLOCK
    r = lax.broadcasted_iota(jnp.int32, (cb, cb), 0)
    c = lax.broadcasted_iota(jnp.int32, (cb, cb), 1)
    same = (r // GLA_CHUNK) == (c // GLA_CHUNK)
    lower = jnp.where(same & (c <= r), 1.0, 0.0).astype(BF16)
    upper = jnp.where(same & (c > r), 1.0, 0.0).astype(BF16)
    nck = cb // GLA_CHUNK
    sr = lax.broadcasted_iota(jnp.int32, (SUBLANES, cb), 0)
    sc = lax.broadcasted_iota(jnp.int32, (SUBLANES, cb), 1)
    chunk_sum = jnp.where((sc // GLA_CHUNK) == sr, 1.0, 0.0).astype(BF16)

    gq = z[:, OFF_GQ:OFF_GQ + GLA_KEY_WIDTH] * (GLA_DK ** -0.5)
    gk = z[:, OFF_GK:OFF_GK + GLA_KEY_WIDTH]
    for blk in range(tm // cb):
        rows = slice(blk * cb, (blk + 1) * cb)
        la = log_a[rows]
        b = _split_dot(lower, la)
        rest = _split_dot(upper, la)
        tot = _split_dot(chunk_sum, la)
        qd_ref[rows, :] = (gq[rows] * jnp.exp(b)).astype(BF16)
        ki_ref[rows, :] = (gk[rows] * jnp.exp(-b)).astype(BF16)
        ke_ref[rows, :] = (gk[rows] * jnp.exp(rest)).astype(BF16)
        dec_ref[blk * nck:(blk + 1) * nck, :] = jnp.exp(tot[:nck])
    yield

    cos = cos_ref[...]
    sin = sin_ref[...]
    lane = lax.broadcasted_iota(jnp.int32, (tm, LANES), 1)
    low64 = lane < MLA_ROPE

    scale = MLA_QK ** -0.5 * LOG2E
    one_col = jnp.where(lane == MLA_ROPE, 1.0, 0.0)
    nope_w = MLA_HEADS * MLA_NOPE
    for pair in range(MLA_HEADS // 2):
        t = qraw[:, nope_w + pair * LANES:nope_w + (pair + 1) * LANES]
        t2 = t * t
        ssq_lo = jnp.sum(jnp.where(low64, t2, 0.0), axis=-1, keepdims=True)
        ssq_hi = jnp.sum(jnp.where(low64, 0.0, t2), axis=-1, keepdims=True)
        tg = t * qgr_ref[...]
        rot = tg * cos + _swap_halves(tg) * sin
        for sub in range(2):
            hd = 2 * pair + sub
            qn = qraw[:, hd * MLA_NOPE:(hd + 1) * MLA_NOPE]
            ssq = jnp.sum(qn * qn, axis=-1, keepdims=True) + (ssq_lo if sub == 0 else ssq_hi)
            rinv = lax.rsqrt(ssq * (1.0 / MLA_QK) + EPS) * scale
            q_ref[hd, :, 0:LANES] = (qn * rinv * qgn_ref[...]).astype(BF16)
            rsel = rot if sub == 0 else pltpu.roll(rot, MLA_ROPE, axis=1)
            q_ref[hd, :, LANES:QK_PAD] = jnp.where(low64, rsel * rinv, one_col).astype(BF16)

    kpe2 = misc * misc
    ssq_pe = jnp.sum(jnp.where(low64, kpe2, 0.0), axis=-1, keepdims=True)
    kg = misc * kgr_ref[...]
    krot = kg * cos + _swap_halves(kg) * sin
    for hd in range(MLA_HEADS):
        kn = kvraw[:, hd * MLA_NOPE:(hd + 1) * MLA_NOPE]
        ssq = jnp.sum(kn * kn, axis=-1, keepdims=True) + ssq_pe
        rinv = lax.rsqrt(ssq * (1.0 / MLA_QK) + EPS)
        k_ref[hd, :, 0:LANES] = (kn * rinv * kgn_ref[...]).astype(BF16)
        k_ref[hd, :, LANES:QK_PAD] = (krot * rinv + kbias_ref[...]).astype(BF16)
        v_ref[hd, :, :] = kvraw[:, nope_w + hd * MLA_DV:nope_w + (hd + 1) * MLA_DV].astype(BF16)


def _gla_kernel(qd_ref, ki_ref, ke_ref, v_ref, dec_ref, gate_ref, g_ref, o_ref, state_ref, *, steps_per_seq):
    @pl.when(pl.program_id(0) % steps_per_seq == 0)
    def _():
        state_ref[...] = jnp.zeros_like(state_ref)

    for _ in _gla_chunks(qd_ref, ki_ref, ke_ref, v_ref, dec_ref, gate_ref, g_ref, o_ref, state_ref):
        pass


def _gla_chunks(qd_ref, ki_ref, ke_ref, v_ref, dec_ref, gate_ref, g_ref, o_ref, state_ref):
    tm = qd_ref.shape[0]

    C = GLA_CHUNK
    klane_head = lax.broadcasted_iota(jnp.int32, (C, GLA_KEY_WIDTH), 1) // GLA_DK
    vlane_head = lax.broadcasted_iota(jnp.int32, (C, GLA_WIDTH), 1) // GLA_DV
    row = lax.broadcasted_iota(jnp.int32, (C, GLA_KEY_WIDTH), 0)
    col_in_head = lax.broadcasted_iota(jnp.int32, (C, GLA_KEY_WIDTH), 1) % GLA_DK
    causal = row >= col_in_head
    pair_rows = 2 * GLA_DV
    own = ((lax.broadcasted_iota(jnp.int32, (pair_rows, LANES), 0) // GLA_DV)
           == (lax.broadcasted_iota(jnp.int32, (pair_rows, LANES), 1) // GLA_DK))
    nt = (((1,), (1,)), ((), ()))
    tn = (((0,), (0,)), ((), ()))
    g = g_ref[...]
    n_pairs = GLA_HEADS // 2

    for ck in range(tm // C):
        rows = slice(ck * C, (ck + 1) * C)
        qd = qd_ref[rows, :]
        ki = ki_ref[rows, :]
        ke = ke_ref[rows, :]
        v = v_ref[rows, :]
        upd = [lax.dot_general(v[:, p * pair_rows:(p + 1) * pair_rows], ke[:, p * LANES:(p + 1) * LANES],
                               tn, preferred_element_type=F32) for p in range(n_pairs)]
        zk = jnp.zeros_like(ki)
        zv = jnp.zeros_like(v)
        kstack = jnp.concatenate([jnp.where(klane_head == hd, ki, zk) for hd in range(GLA_HEADS)], axis=0)
        vstack = jnp.concatenate([jnp.where(vlane_head == hd, v, zv) for hd in range(GLA_HEADS)], axis=0)
        a = lax.dot_general(qd, kstack, nt, preferred_element_type=F32)
        a = jnp.where(causal, a, 0.0).astype(BF16)
        o = jnp.dot(a, vstack, preferred_element_type=F32)
        state = state_ref[...]
        o = o + jnp.concatenate(
            [lax.dot_general(qd[:, p * LANES:(p + 1) * LANES],
                             state[p * pair_rows:(p + 1) * pair_rows].astype(BF16),
                             nt, preferred_element_type=F32) for p in range(n_pairs)], axis=1)
        for p in range(n_pairs):
            srows = slice(p * pair_rows, (p + 1) * pair_rows)
            state_ref[srows, :] = (state[srows] * dec_ref[ck:ck + 1, p * LANES:(p + 1) * LANES]
                                   + jnp.where(own, upd[p], 0.0))
        for hd in range(GLA_HEADS):
            cols = slice(hd * GLA_DV, (hd + 1) * GLA_DV)
            oh = o[:, cols]
            on = oh * lax.rsqrt(jnp.mean(oh * oh, axis=-1, keepdims=True) + EPS) * g
            o_ref[rows, cols] = (on * gate_ref[rows, cols].astype(F32)).astype(BF16)
        yield


def _attn_kernel(q_ref, k_ref, v_ref, gate_ref, o_ref, m_ref, l_ref, acc_ref):
    tq = q_ref.shape[0]
    tk = ATTN_TK
    i = pl.program_id(2)
    q = q_ref[...]
    nt = (((1,), (1,)), ((), ()))
    m_ref[...] = jnp.full_like(m_ref, -jnp.inf)
    l_ref[...] = jnp.zeros_like(l_ref)
    acc_ref[...] = jnp.zeros_like(acc_ref)

    def step(j, masked):
        start = pl.multiple_of(j * tk, tk)
        k = k_ref[pl.ds(start, tk), :]
        v = v_ref[pl.ds(start, tk), :]
        s = lax.dot_general(q, k, nt, preferred_element_type=F32)
        if masked:
            rr = lax.broadcasted_iota(jnp.int32, s.shape, 0)
            cc = lax.broadcasted_iota(jnp.int32, s.shape, 1)
            s = jnp.where(cc <= rr, s, -jnp.inf)
        m_prev = m_ref[...]
        m_new = jnp.maximum(m_prev, jnp.max(s, axis=-1, keepdims=True))
        alpha = jnp.exp2(m_prev - m_new)
        p = jnp.exp2(s - m_new)
        l_ref[...] = alpha * l_ref[...] + jnp.sum(p, axis=-1, keepdims=True)
        acc_ref[...] = alpha * acc_ref[...] + jnp.dot(p.astype(BF16), v, preferred_element_type=F32)
        m_ref[...] = m_new

    def body(j, carry):
        step(j, False)
        return carry

    lax.fori_loop(0, i * (tq // tk), body, 0)
    step(i * (tq // tk), True)
    o = acc_ref[...] / l_ref[...]
    o_ref[...] = (o * gate_ref[...].astype(F32)).astype(BF16)


def _attn_bounded_kernel(q_ref, k_ref, v_ref, gate_ref, o_ref):
    t = ATTN_TQ
    half = t // 2
    n_tiles = k_ref.shape[1] // t
    nt = (((1,), (1,)), ((), ()))
    ones_col = jnp.where(lax.broadcasted_iota(jnp.int32, (t, LANES), 1) == 0, 1.0, 0.0).astype(BF16)
    causal_a = (lax.broadcasted_iota(jnp.int32, (half, half), 1)
                <= lax.broadcasted_iota(jnp.int32, (half, half), 0))
    causal_b = (lax.broadcasted_iota(jnp.int32, (half, t), 1)
                <= lax.broadcasted_iota(jnp.int32, (half, t), 0) + half)

    def probs(q, k):
        return jnp.exp2(lax.dot_general(q, k, nt, preferred_element_type=F32))

    def key_tile(hd, q, j):
        k = k_ref[hd, j * t:(j + 1) * t, :]
        v_aug = jnp.concatenate([v_ref[hd, j * t:(j + 1) * t, :], ones_col], axis=1)
        return jnp.dot(probs(q, k).astype(BF16), v_aug, preferred_element_type=F32)

    def diagonal_tile(hd, q, j):
        k = k_ref[hd, j * t:(j + 1) * t, :]
        v_aug = jnp.concatenate([v_ref[hd, j * t:(j + 1) * t, :], ones_col], axis=1)
        p_a = jnp.where(causal_a, probs(q[:half], k[:half]), 0.0).astype(BF16)
        p_b = jnp.where(causal_b, probs(q[half:], k), 0.0).astype(BF16)
        return jnp.concatenate([jnp.dot(p_a, v_aug[:half], preferred_element_type=F32),
                                jnp.dot(p_b, v_aug, preferred_element_type=F32)], axis=0)

    for hd in range(q_ref.shape[0]):
        cols = slice(hd * MLA_DV, (hd + 1) * MLA_DV)
        for idx in range(n_tiles):
            rows = slice(idx * t, (idx + 1) * t)
            q = q_ref[hd, rows, :]
            acc = diagonal_tile(hd, q, idx)
            for j in range(idx):
                acc = acc + key_tile(hd, q, j)
            o = acc[:, :MLA_DV] * (1.0 / acc[:, MLA_DV:MLA_DV + 1])
            o_ref[rows, cols] = (o * gate_ref[rows, cols].astype(F32)).astype(BF16)


def _out_kernel(x_ref, og_ref, om_ref, w_ref, o_ref):
    w = w_ref[...]
    o_ref[...] = (x_ref[...]
                  + jnp.dot(og_ref[...], w[:GLA_WIDTH], preferred_element_type=F32)
                  + jnp.dot(om_ref[...], w[GLA_WIDTH:], preferred_element_type=F32))


def _layer(shape, l):
    return pl.BlockSpec((None,) + tuple(shape), lambda *idx: (l,) + (0,) * len(shape),
                        pipeline_mode=pl.Buffered(1))


def _params(n_axes):
    return pltpu.CompilerParams(dimension_semantics=("arbitrary",) * n_axes,
                                vmem_limit_bytes=VMEM_LIMIT)


def _proj_call(x2, prev, cos_t, sin_t, w, l, B, S):
    T = x2.shape[0]
    tm = PROJ_TM
    ns = S // tm
    n = T // tm
    fuse_out = prev is not None
    cur = lambda i: jnp.minimum(i, n - 1)
    lag = lambda i: jnp.maximum(i - 1, 0)
    rows = lambda w, at: pl.BlockSpec((tm, w), lambda i: (at(i), 0))
    heads = lambda w: pl.BlockSpec((None, MLA_HEADS, tm, w), lambda i: (lag(i) // ns, 0, lag(i) % ns, 0))
    front_shapes = (
        jax.ShapeDtypeStruct((T, GLA_WIDTH), BF16),
        jax.ShapeDtypeStruct((T, GLA_WIDTH), BF16),
        jax.ShapeDtypeStruct((T, MLA_WIDTH), BF16),
    )
    front_out_specs = (rows(GLA_WIDTH, cur), rows(GLA_WIDTH, cur), rows(MLA_WIDTH, cur))
    post_shapes = (
        jax.ShapeDtypeStruct((T, GLA_KEY_WIDTH), BF16),
        jax.ShapeDtypeStruct((T, GLA_KEY_WIDTH), BF16),
        jax.ShapeDtypeStruct((T, GLA_KEY_WIDTH), BF16),
        jax.ShapeDtypeStruct((T // GLA_CHUNK, GLA_KEY_WIDTH), F32),
        jax.ShapeDtypeStruct((B, MLA_HEADS, S, QK_PAD), BF16),
        jax.ShapeDtypeStruct((B, MLA_HEADS, S, QK_PAD), BF16),
        jax.ShapeDtypeStruct((B, MLA_HEADS, S, MLA_DV), BF16),
    )
    post_specs = (rows(GLA_KEY_WIDTH, lag), rows(GLA_KEY_WIDTH, lag), rows(GLA_KEY_WIDTH, lag),
                  pl.BlockSpec((tm // GLA_CHUNK, GLA_KEY_WIDTH), lambda i: (lag(i), 0)),
                  heads(QK_PAD), heads(QK_PAD), heads(MLA_DV))
    assert len(post_specs) == N_POST_OUT

    front_specs = [rows(D_MODEL, cur)]
    front_args = [x2]
    if fuse_out:
        o_gla, o_mla = prev
        front_specs += [rows(GLA_WIDTH, cur), rows(MLA_WIDTH, cur), _layer((D_MODEL, D_MODEL), l - 1)]
        front_args += [o_gla, o_mla, w["w_out"]]
    front_names = ["norm_g", "w_in", "w_mix"]
    post_names = ["w_misc", "b_gate", "q_norm_g", "w_uq", "kv_norm_g", "w_ukv",
                  "qg_nope", "qg_rope", "kg_nope", "kg_rope", "k_bias"]
    front_specs += [_layer(w[name].shape[1:], l) for name in front_names]
    front_args += [w[name] for name in front_names]
    post_in_specs = [rows(LANES, lag), rows(LANES, lag)] + [_layer(w[name].shape[1:], l) for name in post_names]
    post_in_args = [cos_t, sin_t] + [w[name] for name in post_names]
    assert len(post_in_specs) == N_POST_IN

    out_shape = front_shapes + post_shapes
    out_specs = front_out_specs + post_specs
    if fuse_out:
        out_shape = (jax.ShapeDtypeStruct((T, D_MODEL), F32),) + out_shape
        out_specs = (rows(D_MODEL, cur),) + out_specs
    return pl.pallas_call(
        functools.partial(_proj_kernel, fuse_out=fuse_out),
        grid=(n + 1,), in_specs=front_specs + post_in_specs, out_specs=out_specs, out_shape=out_shape,
        scratch_shapes=[pltpu.VMEM((tm, D_IN_PAD), F32), pltpu.VMEM((tm, D_IN_PAD), F32)],
        compiler_params=_params(1), name="outproj_proj" if fuse_out else "proj",
    )(*front_args, *post_in_args)


def _gla_call(qd, ki, ke, gv, dec, gg, gla_g, l, S):
    T = qd.shape[0]
    tm = GLA_TM
    rows = lambda w: pl.BlockSpec((tm, w), lambda i: (i, 0))
    return pl.pallas_call(
        functools.partial(_gla_kernel, steps_per_seq=S // tm),
        grid=(T // tm,),
        in_specs=[rows(GLA_KEY_WIDTH), rows(GLA_KEY_WIDTH), rows(GLA_KEY_WIDTH), rows(GLA_WIDTH),
                  pl.BlockSpec((tm // GLA_CHUNK, GLA_KEY_WIDTH), lambda i: (i, 0)),
                  rows(GLA_WIDTH), _layer((1, GLA_DV), l)],
        out_specs=rows(GLA_WIDTH),
        out_shape=jax.ShapeDtypeStruct((T, GLA_WIDTH), BF16),
        scratch_shapes=[pltpu.VMEM((GLA_WIDTH, LANES), F32)],
        compiler_params=_params(1), name="gla",
    )(qd, ki, ke, gv, dec, gg, gla_g)


def _attn_call(q, k, v, mg):
    B, H, S, _ = q.shape
    tq = ATTN_TQ
    nq = S // tq
    q_spec = pl.BlockSpec((None, None, tq, QK_PAD), lambda b, h, i: (b, h, i, 0))
    kv_specs = [pl.BlockSpec((None, None, S, QK_PAD), lambda b, h, i: (b, h, 0, 0)),
                pl.BlockSpec((None, None, S, MLA_DV), lambda b, h, i: (b, h, 0, 0))]
    io_spec = pl.BlockSpec((tq, MLA_DV), lambda b, h, i: (b * nq + i, h))
    return pl.pallas_call(
        _attn_kernel, grid=(B, H, nq),
        in_specs=[q_spec] + kv_specs + [io_spec], out_specs=io_spec,
        out_shape=jax.ShapeDtypeStruct(mg.shape, BF16),
        scratch_shapes=[pltpu.VMEM((tq, 1), F32), pltpu.VMEM((tq, 1), F32), pltpu.VMEM((tq, MLA_DV), F32)],
        compiler_params=_params(3), name="attn_online",
    )(q, k, v, mg)


def _attn_bounded_call(q, k, v, mg):
    B, H, S, _ = q.shape
    hp = ATTN_HEADS_PER_STEP
    seq = lambda w: pl.BlockSpec((None, hp, S, w), lambda b, h: (b, h, 0, 0))
    io_spec = pl.BlockSpec((S, hp * MLA_DV), lambda b, h: (b, h))
    return pl.pallas_call(
        _attn_bounded_kernel, grid=(B, H // hp),
        in_specs=[seq(QK_PAD), seq(QK_PAD), seq(MLA_DV), io_spec], out_specs=io_spec,
        out_shape=jax.ShapeDtypeStruct(mg.shape, BF16),
        compiler_params=_params(2), name="attn_bounded",
    )(q, k, v, mg)


def _out_call(x2, og, om, w_out, l):
    T = x2.shape[0]
    tm = OUT_TM
    rows = lambda w: pl.BlockSpec((tm, w), lambda i: (i, 0))
    return pl.pallas_call(
        _out_kernel, grid=(T // tm,),
        in_specs=[rows(D_MODEL), rows(GLA_WIDTH), rows(MLA_WIDTH), _layer((D_MODEL, D_MODEL), l)],
        out_specs=rows(D_MODEL),
        out_shape=jax.ShapeDtypeStruct((T, D_MODEL), F32),
        compiler_params=_params(1), name="outproj",
    )(x2, og, om, w_out)


def _prep_weights(norm_g, w_in, w_gla_gate_up, b_gla_gate, gla_norm_g, mla_q_norm_g, w_uq,
                  mla_kv_norm_g, w_ukv, q_head_g, k_head_g, w_out):
    depth = w_in.shape[0]
    o = [0]
    for wdt in (GLA_KEY_WIDTH, GLA_KEY_WIDTH, GLA_WIDTH, GLA_GATE_RANK, GLA_WIDTH,
                MLA_Q_RANK, MLA_KV_RANK, MLA_ROPE, MLA_WIDTH):
        o.append(o[-1] + wdt)
    wt = jnp.swapaxes(w_in, 1, 2).astype(BF16)
    pad = jnp.zeros((depth, LANES - MLA_ROPE - GLA_GATE_RANK, D_MODEL), BF16)
    w_mix = jnp.concatenate([wt[:, o[6]:o[8]], wt[:, o[3]:o[4]], pad], axis=1)
    assert w_mix.shape[1] == PROJ_CHUNK and o[9] == D_IN
    assert [o[0], o[1], o[2], o[2] + PROJ_CHUNK, o[4], o[4] + PROJ_CHUNK, o[5], None, o[8], o[8] + PROJ_CHUNK] \
        == list(SRC_ROW_OF_CHUNK)

    w_misc = jnp.zeros((depth, LANES, GLA_KEY_WIDTH), F32)
    w_misc = w_misc.at[:, MLA_ROPE:MLA_ROPE + GLA_GATE_RANK, :].set(w_gla_gate_up).astype(BF16)

    wq = w_uq.reshape(depth, MLA_Q_RANK, MLA_HEADS, MLA_QK)
    w_uq_p = jnp.concatenate([wq[..., :MLA_NOPE].reshape(depth, MLA_Q_RANK, -1),
                              wq[..., MLA_NOPE:].reshape(depth, MLA_Q_RANK, -1)], axis=-1).astype(BF16)
    wkv = w_ukv.reshape(depth, MLA_KV_RANK, MLA_HEADS, MLA_NOPE + MLA_DV)
    w_ukv_p = jnp.concatenate([wkv[..., :MLA_NOPE].reshape(depth, MLA_KV_RANK, -1),
                               wkv[..., MLA_NOPE:].reshape(depth, MLA_KV_RANK, -1)], axis=-1).astype(BF16)

    zeros64 = jnp.zeros((depth, 1, LANES - MLA_ROPE), F32)
    row = lambda a: a[:, None, :]
    return dict(
        norm_g=row(norm_g), w_in=wt, w_mix=w_mix, w_misc=w_misc, b_gate=row(b_gla_gate),
        gla_g=row(gla_norm_g), q_norm_g=row(mla_q_norm_g), w_uq=w_uq_p,
        kv_norm_g=row(mla_kv_norm_g), w_ukv=w_ukv_p,
        qg_nope=row(q_head_g[:, :MLA_NOPE]),
        qg_rope=jnp.concatenate([row(q_head_g[:, MLA_NOPE:])] * 2, axis=-1),
        kg_nope=row(k_head_g[:, :MLA_NOPE]),
        kg_rope=jnp.concatenate([row(k_head_g[:, MLA_NOPE:]), zeros64], axis=-1),
        w_out=w_out.astype(BF16),
    )


def _score_bounds(q_head_g, k_head_g):
    bound = (MLA_QK ** 0.5 * LOG2E * SCORE_BOUND_SLACK
             * jnp.max(jnp.abs(q_head_g), axis=-1) * jnp.max(jnp.abs(k_head_g), axis=-1))
    use = bound <= SCORE_BOUND_MAX_LOG2
    depth = q_head_g.shape[0]
    bias = jnp.zeros((depth, 1, LANES), F32).at[:, 0, MLA_ROPE].set(jnp.where(use, -bound, 0.0))
    return use, bias


def _rope_tables(positions):
    inv_freq = ROPE_THETA ** (-jnp.arange(0, MLA_ROPE, 2, dtype=F32) / MLA_ROPE)
    reps = LANES // inv_freq.shape[0]
    inv_freq = jnp.tile(inv_freq, reps)
    sign = jnp.tile(jnp.repeat(jnp.array([-1.0, 1.0], F32), MLA_ROPE // 2), LANES // MLA_ROPE)
    ang = positions.astype(F32).reshape(-1, 1) * inv_freq
    return jnp.cos(ang), jnp.sin(ang) * sign


def kernel(x, positions, norm_g, w_in, w_gla_gate_up, b_gla_gate, gla_norm_g, mla_q_norm_g, w_uq,
           mla_kv_norm_g, w_ukv, q_head_g, k_head_g, w_out):
    B, S, D = x.shape
    assert D == D_MODEL and S % max(PROJ_TM, GLA_TM, ATTN_TQ, OUT_TM) == 0 and ATTN_TQ % ATTN_TK == 0
    assert MLA_HEADS % ATTN_HEADS_PER_STEP == 0
    depth = w_in.shape[0]
    weights = _prep_weights(norm_g, w_in, w_gla_gate_up, b_gla_gate, gla_norm_g, mla_q_norm_g, w_uq,
                            mla_kv_norm_g, w_ukv, q_head_g, k_head_g, w_out)
    use_bounded, weights["k_bias"] = _score_bounds(q_head_g, k_head_g)
    cos_t, sin_t = _rope_tables(positions)
    x2 = x.reshape(B * S, D)
    prev = None
    for l in range(depth):
        outs = _proj_call(x2, prev, cos_t, sin_t, weights, l, B, S)
        if prev is not None:
            x2, outs = outs[0], outs[1:]
        gv, gg, mg, qd, ki, ke, dec, q, k, v = outs
        o_gla = _gla_call(qd, ki, ke, gv, dec, gg, weights["gla_g"], l, S)
        o_mla = lax.cond(use_bounded[l], _attn_bounded_call, _attn_call, q, k, v, mg)
        prev = (o_gla, o_mla)
    x2 = _out_call(x2, *prev, weights["w_out"], depth - 1)
    return x2.reshape(B, S, D)
```
